```python
import jax, jax.numpy as jnp
from jax import lax
import numpy as np

D_MODEL = 2048
BATCH = 2
SEQ = 16384
DEPTH = 4

CHUNK = 64
N_A_LAYERS = DEPTH // 2
N_B_LAYERS = DEPTH - N_A_LAYERS
LRU_WIDTH = D_MODEL
LRU_BLOCKS = 8
LRU_BLOCK_DIM = LRU_WIDTH // LRU_BLOCKS
CONV_WIDTH = 4
LRU_C = 8.0
N_HEADS = 16
HEAD_DIM = D_MODEL // N_HEADS
D_FF = 4 * D_MODEL
Q_BLOCK = 128
KV_BLOCK = 128
NORM_EPS = 1e-6
NEG_INF = -1e30
N_MOD = 6

kernel_name = "yoco_rglru_fox_hybrid"


def rms_norm(x, g):
    xf = x.astype(jnp.float32)
    y = xf * lax.rsqrt(jnp.mean(jnp.square(xf), axis=-1, keepdims=True) + NORM_EPS)
    return (y * g.astype(jnp.float32)).astype(x.dtype)


def modulate(h, shift, scale):
    return h * (1 + scale) + shift


def block_diag_linear(x, w, b):
    xb = x.reshape(x.shape[:-1] + (LRU_BLOCKS, LRU_BLOCK_DIM))
    y = jnp.einsum('bsnc,ncd->bsnd', xb, w)
    return y.reshape(x.shape) + b


def causal_depthwise_conv(x, w, b):
    S = x.shape[1]
    xp = jnp.pad(x, ((0, 0), (CONV_WIDTH - 1, 0), (0, 0)))
    y = b
    for k in range(CONV_WIDTH):
        y = y + xp[:, k:k + S, :] * w[k]
    return y


def rg_lru(x, w_a, b_a, w_i, b_i, lam):
    f32 = jnp.float32
    xf = x.astype(f32)
    r = jax.nn.sigmoid(block_diag_linear(xf, w_a.astype(f32), b_a.astype(f32)))
    i = jax.nn.sigmoid(block_diag_linear(xf, w_i.astype(f32), b_i.astype(f32)))
    log_a = -LRU_C * r * jax.nn.softplus(-lam.astype(f32))
    a = jnp.exp(log_a)
    u = jnp.sqrt(-jnp.expm1(2.0 * log_a)) * (i * xf)

    def step(h, au):
        a_t, u_t = au
        h = a_t * h + u_t
        return h, h

    h0 = jnp.zeros((x.shape[0], x.shape[2]), f32)
    _, hs = lax.scan(step, h0, (jnp.swapaxes(a, 0, 1), jnp.swapaxes(u, 0, 1)))
    return jnp.swapaxes(hs, 0, 1).astype(x.dtype)


def shared_kv_stream(x, cs, kv_norm_g, kv_w_mod, kv_b_mod, w_kv, k_norm_g, w_forget, b_forget):
    B, S, _ = x.shape
    shift, scale = jnp.split((cs @ kv_w_mod + kv_b_mod)[:, None, :], 2, axis=-1)
    h = modulate(rms_norm(x, kv_norm_g), shift, scale)
    k, v = jnp.split(h @ w_kv, 2, axis=-1)
    k = rms_norm(k.reshape(B, S, N_HEADS, HEAD_DIM), k_norm_g)
    v = v.reshape(B, S, N_HEADS, HEAD_DIM)
    log_f = jax.nn.log_sigmoid((h @ w_forget + b_forget).astype(jnp.float32))
    F = jnp.cumsum(log_f, axis=1)
    return jnp.swapaxes(k, 1, 2), jnp.swapaxes(v, 1, 2), jnp.swapaxes(F, 1, 2)


def forgetting_attention(q, k, v, F):
    B, H, S, Dh = q.shape
    f32 = jnp.float32
    nq = S // Q_BLOCK
    qb = q.reshape(B, H, nq, Q_BLOCK, Dh).transpose(2, 0, 1, 3, 4)
    Fqb = F.reshape(B, H, nq, Q_BLOCK).transpose(2, 0, 1, 3)
    scale = Dh ** -0.5
    q_offs = jnp.arange(Q_BLOCK)
    k_offs = jnp.arange(KV_BLOCK)

    def one_block(args):
        qi, q_blk, fq = args
        q_blk = q_blk.astype(f32) * scale
        q_pos = qi * Q_BLOCK + q_offs

        def body(j, carry):
            m, l, acc = carry
            start = j * KV_BLOCK
            k_blk = lax.dynamic_slice_in_dim(k, start, KV_BLOCK, axis=2).astype(f32)
            v_blk = lax.dynamic_slice_in_dim(v, start, KV_BLOCK, axis=2).astype(f32)
            fk = lax.dynamic_slice_in_dim(F, start, KV_BLOCK, axis=2)
            s = jnp.einsum('bhqd,bhkd->bhqk', q_blk, k_blk) + fq[..., :, None] - fk[..., None, :]
            allowed = (start + k_offs)[None, :] <= q_pos[:, None]
            s = jnp.where(allowed, s, NEG_INF)
            m_new = jnp.maximum(m, jnp.max(s, axis=-1))
            p = jnp.exp(s - m_new[..., None])
            alpha = jnp.exp(m - m_new)
            l = l * alpha + jnp.sum(p, axis=-1)
            acc = acc * alpha[..., None] + jnp.einsum('bhqk,bhkd->bhqd', p, v_blk)
            return m_new, l, acc

        init = (jnp.full((B, H, Q_BLOCK), NEG_INF, f32),
                jnp.zeros((B, H, Q_BLOCK), f32),
                jnp.zeros((B, H, Q_BLOCK, Dh), f32))
        n_kv = ((qi + 1) * Q_BLOCK + KV_BLOCK - 1) // KV_BLOCK
        _, l, acc = lax.fori_loop(0, n_kv, body, init)
        return acc / l[..., None]

    out = lax.map(one_block, (jnp.arange(nq), qb, Fqb))
    out = out.transpose(1, 0, 3, 2, 4).reshape(B, S, H * Dh)
    return out.astype(q.dtype)


def setup_inputs(seed: int = 0) -> dict:
    key = jax.random.key(seed)
    ks = iter(jax.random.split(key, 40))
    f32 = jnp.float32
    D, W, F, H, Dh = D_MODEL, LRU_WIDTH, D_FF, N_HEADS, HEAD_DIM
    NA, NB = N_A_LAYERS, N_B_LAYERS

    def nrm(shape, s):
        return jax.random.normal(next(ks), shape, f32) * s

    def gain(shape):
        return 1.0 + 0.02 * jax.random.normal(next(ks), shape, f32)

    u = jax.random.uniform(next(ks), (NA, W), f32, minval=0.9, maxval=0.999)
    a0 = u ** (1.0 / LRU_C)
    lam = jnp.log(a0) - jnp.log1p(-a0)
    return {
        "x": nrm((BATCH, SEQ, D), 1.0),
        "c": nrm((BATCH, D), 1.0),
        "mix_norm_g": gain((DEPTH, D)),
        "mlp_norm_g": gain((DEPTH, D)),
        "w_mod": nrm((DEPTH, D, N_MOD * D), D ** -0.5),
        "b_mod": nrm((DEPTH, N_MOD * D), 0.01),
        "w_mlp_in": nrm((DEPTH, D, F), D ** -0.5),
        "w_mlp_out": nrm((DEPTH, F, D), F ** -0.5),
        "lru_w_in": nrm((NA, D, 2 * W), D ** -0.5),
        "lru_conv_w": nrm((NA, CONV_WIDTH, W), CONV_WIDTH ** -0.5),
        "lru_conv_b": nrm((NA, W), 0.01),
        "lru_w_a": nrm((NA, LRU_BLOCKS, LRU_BLOCK_DIM, LRU_BLOCK_DIM), LRU_BLOCK_DIM ** -0.5),
        "lru_b_a": nrm((NA, W), 0.01),
        "lru_w_i": nrm((NA, LRU_BLOCKS, LRU_BLOCK_DIM, LRU_BLOCK_DIM), LRU_BLOCK_DIM ** -0.5),
        "lru_b_i": nrm((NA, W), 0.01),
        "lru_lambda": lam,
        "lru_w_out": nrm((NA, W, D), W ** -0.5),
        "kv_norm_g": gain((D,)),
        "kv_w_mod": nrm((D, 2 * D), D ** -0.5),
        "kv_b_mod": nrm((2 * D,), 0.01),
        "w_kv": nrm((D, 2 * D), D ** -0.5),
        "k_norm_g": gain((Dh,)),
        "w_forget": nrm((D, H), D ** -0.5),
        "b_forget": jax.random.uniform(next(ks), (H,), f32, minval=1.0, maxval=5.0),
        "attn_w_q": nrm((NB, D, D), D ** -0.5),
        "q_norm_g": gain((NB, Dh)),
        "attn_w_o": nrm((NB, D, D), D ** -0.5),
    }


def reference(x, c, mix_norm_g, mlp_norm_g, w_mod, b_mod, w_mlp_in, w_mlp_out,
              lru_w_in, lru_conv_w, lru_conv_b, lru_w_a, lru_b_a, lru_w_i, lru_b_i,
              lru_lambda, lru_w_out, kv_norm_g, kv_w_mod, kv_b_mod, w_kv, k_norm_g,
              w_forget, b_forget, attn_w_q, q_norm_g, attn_w_o):
    B, S, _ = x.shape
    cs = jax.nn.silu(c)
    for layer in range(DEPTH):
        mod = (cs @ w_mod[layer] + b_mod[layer])[:, None, :]
        sh1, sc1, g1, sh2, sc2, g2 = jnp.split(mod, N_MOD, axis=-1)
        if layer == N_A_LAYERS:
            k, v, F = shared_kv_stream(x, cs, kv_norm_g, kv_w_mod, kv_b_mod, w_kv,
                                       k_norm_g, w_forget, b_forget)
        h = modulate(rms_norm(x, mix_norm_g[layer]), sh1, sc1)
        if layer < N_A_LAYERS:
            a = layer
            xb, yb = jnp.split(h @ lru_w_in[a], 2, axis=-1)
            xb = causal_depthwise_conv(xb, lru_conv_w[a], lru_conv_b[a])
            xb = rg_lru(xb, lru_w_a[a], lru_b_a[a], lru_w_i[a], lru_b_i[a], lru_lambda[a])
            mix = (xb * jax.nn.gelu(yb)) @ lru_w_out[a]
        else:
            bl = layer - N_A_LAYERS
            q = (h @ attn_w_q[bl]).reshape(B, S, N_HEADS, HEAD_DIM)
            q = jnp.swapaxes(rms_norm(q, q_norm_g[bl]), 1, 2)
            mix = forgetting_attention(q, k, v, F) @ attn_w_o[bl]
        x = x + g1 * mix
        h = modulate(rms_norm(x, mlp_norm_g[layer]), sh2, sc2)
        x = x + g2 * (jnp.square(jax.nn.relu(h @ w_mlp_in[layer])) @ w_mlp_out[layer])
    return x
```

```python
import functools
import math

import jax
import jax.numpy as jnp
from jax import lax
from jax.experimental import pallas as pl
from jax.experimental.pallas import tpu as pltpu

F32 = jnp.float32
BF16 = jnp.bfloat16

NORM_EPS = 1e-6
LRU_C = 8.0
NEG_INF = -1e30
N_MOD = 6

V7X_LANES = 128
V7X_SUBLANES = 8
V7X_BF16_ROWS = 16
V7X_VMEM_BYTES = 64 * 1024 * 1024
V7X_VMEM_RESERVE = 6 * 1024 * 1024

N_F_PARTS = 3


def _vmem_limit(nbytes):
    return int(min(V7X_VMEM_BYTES - V7X_VMEM_RESERVE, max(nbytes * 5 // 4 + (8 << 20), 32 << 20)))


def _params(n_grid, nbytes):
    return pltpu.CompilerParams(dimension_semantics=("arbitrary",) * n_grid,
                                vmem_limit_bytes=_vmem_limit(nbytes))


def _tile(n, want):
    t = min(n, want)
    while n % t:
        t -= 1
    return t


def _resident(shape):
    nd = len(shape)
    return pl.BlockSpec(shape, lambda *_: (0,) * nd, pipeline_mode=pl.Buffered(1))


def _rms_mod(x, g, scale, shift):
    ms = jnp.mean(x * x, axis=-1, keepdims=True)
    return (x * lax.rsqrt(ms + NORM_EPS)) * (g * (1.0 + scale)) + shift


def _gelu_tanh(x):
    c = math.sqrt(2.0 / math.pi)
    return x * (0.5 * (1.0 + jnp.tanh(c * (x + 0.044715 * (x * x * x)))))


def _sigmoid(x):
    return 1.0 / (1.0 + jnp.exp(-x))


def _log_sigmoid(x):
    return jnp.minimum(x, 0.0) - jnp.log1p(jnp.exp(-jnp.abs(x)))


def _softplus(x):
    return jnp.maximum(x, 0.0) + jnp.log1p(jnp.exp(-jnp.abs(x)))


def _mod_kernel(cb_ref, w_ref, b_ref, o_ref):
    n_batch = cb_ref.shape[0]
    tn = w_ref.shape[2]
    for b in range(n_batch):
        cv = cb_ref[b]
        cs = cv * _sigmoid(cv)
        for ch in range(tn // V7X_LANES):
            cols = slice(ch * V7X_LANES, (ch + 1) * V7X_LANES)
            s = jnp.sum(w_ref[0, :, cols] * cs, axis=0, keepdims=True)
            o_ref[0, b:b + 1, cols] = s + b_ref[0, :, cols]


def _modulation(c, w, b):
    n_l, d, m = w.shape
    n_batch = c.shape[0]
    tn = _tile(m, 1024)
    cb = jnp.broadcast_to(c[:, :, None], (n_batch, d, V7X_LANES))
    nbytes = cb.size * 4 + 2 * d * tn * 4 + 4 * n_batch * tn * 4
    return pl.pallas_call(
        _mod_kernel,
        grid=(n_l, m // tn),
        in_specs=[
            pl.BlockSpec((n_batch, d, V7X_LANES), lambda l, j: (0, 0, 0)),
            pl.BlockSpec((1, d, tn), lambda l, j: (l, 0, j)),
            pl.BlockSpec((1, 1, tn), lambda l, j: (l, 0, j)),
        ],
        out_specs=pl.BlockSpec((1, n_batch, tn), lambda l, j: (l, 0, j)),
        out_shape=jax.ShapeDtypeStruct((n_l, n_batch, m), F32),
        compiler_params=_params(2, nbytes),
        name="modulation",
    )(cb, w, b.reshape(n_l, 1, m))


def _norm_kernel(x_ref, g_ref, sc_ref, sh_ref, o_ref):
    o_ref[...] = _rms_mod(x_ref[...], g_ref[...], sc_ref[0], sh_ref[0]).astype(o_ref.dtype)


def _norm_mod(x2, g, scale, shift, seq):
    n, d = x2.shape
    tm = _tile(seq, 512)
    tpb = seq // tm
    row = lambda i: (i, 0)
    per_batch = lambda i: (i // tpb, 0, 0)
    nbytes = 2 * tm * d * 4 + 2 * tm * d * 2
    return pl.pallas_call(
        _norm_kernel,
        grid=(n // tm,),
        in_specs=[
            pl.BlockSpec((tm, d), row),
            pl.BlockSpec((1, d), lambda i: (0, 0)),
            pl.BlockSpec((1, 1, d), per_batch),
            pl.BlockSpec((1, 1, d), per_batch),
        ],
        out_specs=pl.BlockSpec((tm, d), row),
        out_shape=jax.ShapeDtypeStruct((n, d), BF16),
        compiler_params=_params(1, nbytes),
        name="norm_mod",
    )(x2, g.reshape(1, d), scale, shift)


def _lru_in_kernel(h_ref, w_ref, o_ref, *, tc):
    h = h_ref[...]
    n_cols = w_ref.shape[1]
    for c in range(n_cols // tc):
        cols = slice(c * tc, (c + 1) * tc)
        acc = jnp.dot(h, w_ref[:, cols], preferred_element_type=F32)
        if c * tc >= n_cols // 2:
            acc = _gelu_tanh(acc)
        o_ref[:, cols] = acc


def _lru_in(h, w):
    n, d = h.shape
    m = w.shape[1]
    tm = _tile(n, 512)
    tc = _tile(m // 2, 512)
    nbytes = 2 * tm * d * 2 + d * m * 2 + 2 * tm * m * 4 + 2 * tm * tc * 4
    return pl.pallas_call(
        functools.partial(_lru_in_kernel, tc=tc),
        grid=(n // tm,),
        in_specs=[pl.BlockSpec((tm, d), lambda i: (i, 0)), _resident((d, m))],
        out_specs=pl.BlockSpec((tm, m), lambda i: (i, 0)),
        out_shape=jax.ShapeDtypeStruct((n, m), F32),
        compiler_params=_params(1, nbytes),
        name="lru_in_proj",
    )(h, w)


def _lru_scan_kernel(xb_ref, gy_ref, cw_ref, cb_ref, wa_ref, ba_ref, wi_ref, bi_ref, lam_ref,
                     o_ref, xpad_s, a_s, u_s, h_s, *, lane_chunk):
    ts, width = xb_ref.shape
    conv_width = cw_ref.shape[0]
    n_blocks, bd, _ = wa_ref.shape
    halo = V7X_SUBLANES
    assert conv_width - 1 <= halo

    @pl.when(pl.program_id(1) == 0)
    def _():
        xpad_s[0:halo, :] = jnp.zeros((halo, width), F32)
        h_s[...] = jnp.zeros(h_s.shape, F32)

    xpad_s[halo:halo + ts, :] = xb_ref[...]

    decay = -LRU_C * _softplus(-lam_ref[...])

    for nb in range(n_blocks):
        cols = slice(nb * bd, (nb + 1) * bd)
        xc = cb_ref[:, cols] + cw_ref[conv_width - 1:conv_width, cols] * xpad_s[halo:halo + ts, cols]
        for k in range(conv_width - 1):
            back = conv_width - 1 - k
            xc = xc + cw_ref[k:k + 1, cols] * xpad_s[halo - back:halo - back + ts, cols]
        xcb = xc.astype(BF16)
        r = _sigmoid(jnp.dot(xcb, wa_ref[nb], preferred_element_type=F32) + ba_ref[:, cols])
        gi = _sigmoid(jnp.dot(xcb, wi_ref[nb], preferred_element_type=F32) + bi_ref[:, cols])
        log_a = decay[:, cols] * r
        a = jnp.exp(log_a)
        a_s[:, cols] = a
        u_s[:, cols] = jnp.sqrt(-jnp.tanh(log_a) * (a * a + 1.0)) * (gi * xc)

    xpad_s[0:halo, :] = xpad_s[ts:ts + halo, :]

    rows16 = V7X_BF16_ROWS
    row_id = lax.broadcasted_iota(jnp.int32, (V7X_SUBLANES, lane_chunk), 0)

    def group(g, carry):
        r0 = pl.multiple_of(g * rows16, rows16)
        for lc in range(width // lane_chunk):
            cols = slice(lc * lane_chunk, (lc + 1) * lane_chunk)
            hprev = h_s[:, cols]
            halves = []
            for half in range(rows16 // V7X_SUBLANES):
                rr = pl.ds(r0 + half * V7X_SUBLANES, V7X_SUBLANES)
                a = a_s[rr, cols]
                u = u_s[rr, cols]
                for sh in (1, 2, 4):
                    a_sh = pltpu.roll(a, sh, axis=0)
                    u_sh = pltpu.roll(u, sh, axis=0)
                    live = row_id >= sh
                    u = jnp.where(live, a * u_sh + u, u)
                    a = jnp.where(live, a * a_sh, a)
                hcur = a * hprev + u
                hprev = jnp.broadcast_to(hcur[V7X_SUBLANES - 1:V7X_SUBLANES, :], hcur.shape)
                halves.append(hcur)
            h_s[:, cols] = hprev
            h16 = jnp.concatenate(halves, axis=0)
            o_ref[pl.ds(r0, rows16), cols] = (h16 * gy_ref[pl.ds(r0, rows16), cols]).astype(o_ref.dtype)
        return carry

    lax.fori_loop(0, ts // rows16, group, 0)


def _lru_scan(xg, conv_w, conv_b, w_a, b_a, w_i, b_i, lam, n_batch, seq):
    n, two_w = xg.shape
    width = two_w // 2
    ts = _tile(seq, 256)
    tpb = seq // ts
    n_blocks, bd, _ = w_a.shape
    lane_chunk = _tile(width, 512)
    row = lambda b, t: (b * tpb + t, 0)
    vec = lambda a: a.reshape(1, width)
    small = lambda shape: pl.BlockSpec(shape, lambda b, t: (0,) * len(shape))
    nbytes = (4 * ts * width * 4 + 2 * ts * width * 2 + (3 * ts + 16) * width * 4
              + 4 * n_blocks * bd * bd * 2)
    return pl.pallas_call(
        functools.partial(_lru_scan_kernel, lane_chunk=lane_chunk),
        grid=(n_batch, tpb),
        in_specs=[
            pl.BlockSpec((ts, width), row),
            pl.BlockSpec((ts, width), lambda b, t: (b * tpb + t, 1)),
            small(conv_w.shape), small((1, width)),
            small(w_a.shape), small((1, width)),
            small(w_i.shape), small((1, width)),
            small((1, width)),
        ],
        out_specs=pl.BlockSpec((ts, width), row),
        out_shape=jax.ShapeDtypeStruct((n, width), BF16),
        scratch_shapes=[
            pltpu.VMEM((ts + 2 * V7X_SUBLANES, width), F32),
            pltpu.VMEM((ts, width), F32),
            pltpu.VMEM((ts, width), F32),
            pltpu.VMEM((V7X_SUBLANES, width), F32),
        ],
        compiler_params=_params(2, nbytes),
        name="lru_scan",
    )(xg, xg, conv_w, vec(conv_b), w_a, vec(b_a), w_i, vec(b_i), vec(lam))


def _proj_res_kernel(a_ref, w_ref, x_ref, g1_ref, ng_ref, sc_ref, sh_ref, xo_ref, h_ref, *, rc):
    tm = a_ref.shape[0]
    for r in range(tm // rc):
        rows = slice(r * rc, (r + 1) * rc)
        acc = jnp.dot(a_ref[rows, :], w_ref[...], preferred_element_type=F32)
        xn = x_ref[rows, :] + g1_ref[0] * acc
        xo_ref[rows, :] = xn
        h_ref[rows, :] = _rms_mod(xn, ng_ref[...], sc_ref[0], sh_ref[0]).astype(h_ref.dtype)


def _proj_res(a, w, x2, gate, norm_g, scale, shift, seq):
    n, k = a.shape
    d = w.shape[1]
    tm = _tile(seq, 512)
    rc = _tile(tm, 256)
    tpb = seq // tm
    row = lambda i: (i, 0)
    per_batch = lambda i: (i // tpb, 0, 0)
    nbytes = 2 * tm * k * 2 + k * d * 2 + 4 * tm * d * 4 + 2 * tm * d * 2 + 3 * rc * d * 4
    return pl.pallas_call(
        functools.partial(_proj_res_kernel, rc=rc),
        grid=(n // tm,),
        in_specs=[
            pl.BlockSpec((tm, k), row), _resident((k, d)), pl.BlockSpec((tm, d), row),
            pl.BlockSpec((1, 1, d), per_batch), pl.BlockSpec((1, d), lambda i: (0, 0)),
            pl.BlockSpec((1, 1, d), per_batch), pl.BlockSpec((1, 1, d), per_batch),
        ],
        out_specs=[pl.BlockSpec((tm, d), row), pl.BlockSpec((tm, d), row)],
        out_shape=[jax.ShapeDtypeStruct((n, d), F32), jax.ShapeDtypeStruct((n, d), BF16)],
        compiler_params=_params(1, nbytes),
        name="mix_out_proj",
    )(a, w, x2, gate, norm_g.reshape(1, d), scale, shift)


def _mlp_kernel(*refs, n_norm, rc):
    h_ref, win_ref, wout_ref, x_ref, g2_ref = refs[:5]
    norm_refs = refs[5:5 + 3 * n_norm]
    xo_ref = refs[5 + 3 * n_norm]
    ho_refs = refs[6 + 3 * n_norm:6 + 4 * n_norm]
    acc_s = refs[6 + 4 * n_norm]
    k = pl.program_id(1)

    a = jnp.dot(h_ref[...], win_ref[...], preferred_element_type=F32)
    a = jnp.square(jnp.maximum(a, 0.0)).astype(BF16)

    @pl.when(k == 0)
    def _():
        acc_s[...] = jnp.dot(a, wout_ref[...], preferred_element_type=F32)

    @pl.when(k > 0)
    def _():
        acc_s[...] += jnp.dot(a, wout_ref[...], preferred_element_type=F32)

    @pl.when(k == pl.num_programs(1) - 1)
    def _():
        tm = x_ref.shape[0]

        def rows_body(r, carry):
            rows = pl.ds(pl.multiple_of(r * rc, rc), rc)
            xn = x_ref[rows, :] + g2_ref[0] * acc_s[rows, :]
            xo_ref[rows, :] = xn
            for j in range(n_norm):
                ng, sc, sh = norm_refs[3 * j:3 * j + 3]
                ho_refs[j][rows, :] = _rms_mod(xn, ng[...], sc[0], sh[0]).astype(BF16)
            return carry

        lax.fori_loop(0, tm // rc, rows_body, 0)


def _mlp(h2, x2, w_in, w_out, gate, norms, seq):
    n, d = x2.shape
    f = w_in.shape[1]
    tm = _tile(seq, 512)
    tf = _tile(f, 512)
    rc = _tile(tm, 64)
    tpb = seq // tm
    n_norm = len(norms)
    row = lambda i, k: (i, 0)
    per_batch = lambda i, k: (i // tpb, 0, 0)
    in_specs = [
        pl.BlockSpec((tm, d), row),
        pl.BlockSpec((d, tf), lambda i, k: (0, k)),
        pl.BlockSpec((tf, d), lambda i, k: (k, 0)),
        pl.BlockSpec((tm, d), row),
        pl.BlockSpec((1, 1, d), per_batch),
    ]
    args = [h2, w_in, w_out, x2, gate]
    for ng, sc, sh in norms:
        in_specs += [pl.BlockSpec((1, d), lambda i, k: (0, 0)),
                     pl.BlockSpec((1, 1, d), per_batch), pl.BlockSpec((1, 1, d), per_batch)]
        args += [ng.reshape(1, d), sc, sh]
    out_specs = [pl.BlockSpec((tm, d), row)] * (1 + n_norm)
    out_shape = [jax.ShapeDtypeStruct((n, d), F32)] + [jax.ShapeDtypeStruct((n, d), BF16)] * n_norm
    nbytes = (2 * tm * d * 2 + 4 * d * tf * 2 + 4 * tm * d * 4 + tm * d * 4
              + 2 * n_norm * tm * d * 2 + 2 * tm * tf * 4 + tm * d * 4)
    outs = pl.pallas_call(
        functools.partial(_mlp_kernel, n_norm=n_norm, rc=rc),
        grid=(n // tm, f // tf),
        in_specs=in_specs,
        out_specs=out_specs,
        out_shape=out_shape,
        scratch_shapes=[pltpu.VMEM((tm, d), F32)],
        compiler_params=_params(2, nbytes),
        name="relu2_mlp",
    )(*args)
    return outs[0], list(outs[1:])


def _head_norm(a, g):
    ms = jnp.mean(a * a, axis=-1, keepdims=True)
    return a * lax.rsqrt(ms + NORM_EPS) * g


def _q_kernel(h_ref, w_ref, g_ref, o_ref, *, hc, scale):
    tm = h_ref.shape[0]
    n_heads, dh = o_ref.shape[1], g_ref.shape[1]
    h = h_ref[...]
    lane = lax.broadcasted_iota(jnp.int32, (tm, dh), 1)
    ones_cols = jnp.where(lane < N_F_PARTS, 1.0, 0.0).astype(o_ref.dtype)
    gs = g_ref[...] * scale
    for c in range(n_heads // hc):
        acc = jnp.dot(h, w_ref[:, c * hc * dh:(c + 1) * hc * dh], preferred_element_type=F32)
        for hh in range(hc):
            head = c * hc + hh
            qn = _head_norm(acc[:, hh * dh:(hh + 1) * dh], gs)
            o_ref[0, head, :, 0:dh] = qn.astype(o_ref.dtype)
            o_ref[0, head, :, dh:2 * dh] = ones_cols


def _q_proj(h, w, g, n_batch, seq, n_heads):
    n, d = h.shape
    dh = d // n_heads
    tm = _tile(seq, 512)
    tpb = seq // tm
    hc = _tile(n_heads, 4)
    nbytes = 2 * tm * d * 2 + d * d * 2 + 2 * n_heads * tm * 2 * dh * 2 + 2 * tm * hc * dh * 4
    return pl.pallas_call(
        functools.partial(_q_kernel, hc=hc, scale=dh ** -0.5),
        grid=(n // tm,),
        in_specs=[pl.BlockSpec((tm, d), lambda i: (i, 0)), _resident((d, d)),
                  pl.BlockSpec((1, dh), lambda i: (0, 0))],
        out_specs=pl.BlockSpec((1, n_heads, tm, 2 * dh), lambda i: (i // tpb, 0, i % tpb, 0)),
        out_shape=jax.ShapeDtypeStruct((n_batch, n_heads, seq, 2 * dh), BF16),
        compiler_params=_params(1, nbytes),
        name="q_proj",
    )(h, w, g.reshape(1, dh))


def _kv_kernel(h_ref, w_ref, wf_ref, bf_ref, g_ref, ko_ref, vo_ref, carry_s, *, hc, tpb):
    tm, d = h_ref.shape
    n_heads, dh = ko_ref.shape[1], g_ref.shape[1]

    @pl.when(pl.program_id(0) % tpb == 0)
    def _():
        carry_s[...] = jnp.zeros(carry_s.shape, F32)

    h = h_ref[...]

    z = jnp.dot(h, wf_ref[...], preferred_element_type=F32) + bf_ref[...]
    log_f = _log_sigmoid(z)
    ri = lax.broadcasted_iota(jnp.int32, (tm, tm), 0)
    ci = lax.broadcasted_iota(jnp.int32, (tm, tm), 1)
    tri = jnp.where(ri >= ci, 1.0, 0.0).astype(F32)
    cum = jnp.dot(tri, log_f, preferred_element_type=F32, precision=lax.Precision.HIGHEST)
    cum = cum + carry_s[0:1, :]
    carry_s[...] = jnp.broadcast_to(cum[tm - 1:tm, :], carry_s.shape)

    parts = []
    rest = -cum
    for _ in range(N_F_PARTS):
        piece = rest.astype(BF16).astype(F32)
        parts.append(piece)
        rest = rest - piece

    lane = lax.broadcasted_iota(jnp.int32, (tm, dh), 1)
    g = g_ref[...]
    for c in range(n_heads // hc):
        acc = jnp.dot(h, w_ref[:, c * hc * dh:(c + 1) * hc * dh], preferred_element_type=F32)
        for hh in range(hc):
            head = c * hc + hh
            kn = _head_norm(acc[:, hh * dh:(hh + 1) * dh], g)
            ko_ref[0, head, :, 0:dh] = kn.astype(ko_ref.dtype)
            aug = jnp.zeros((tm, dh), F32)
            for p in range(N_F_PARTS):
                col = jnp.broadcast_to(parts[p][:, head:head + 1], (tm, dh))
                aug = jnp.where(lane == p, col, aug)
            ko_ref[0, head, :, dh:2 * dh] = aug.astype(ko_ref.dtype)
    for c in range(n_heads // hc):
        acc = jnp.dot(h, w_ref[:, d + c * hc * dh:d + (c + 1) * hc * dh], preferred_element_type=F32)
        for hh in range(hc):
            head = c * hc + hh
            vo_ref[0, head, 0] = acc[:, hh * dh:(hh + 1) * dh].T.astype(vo_ref.dtype)


def _kv_proj(h, w, w_forget, b_forget, g, n_batch, seq, n_heads, tk):
    n, d = h.shape
    dh = d // n_heads
    tm = tk
    tpb = seq // tm
    hc = _tile(n_heads, 4)
    assert n_heads <= V7X_LANES
    wf = jnp.zeros((d, V7X_LANES), BF16).at[:, :n_heads].set(w_forget.astype(BF16))
    bf = jnp.zeros((1, V7X_LANES), F32).at[0, :n_heads].set(b_forget)
    nbytes = (2 * tm * d * 2 + 2 * d * d * 2 + 2 * n_heads * tm * 3 * dh * 2
              + 2 * tm * hc * dh * 4 + 3 * tm * tm * 4)
    return pl.pallas_call(
        functools.partial(_kv_kernel, hc=hc, tpb=tpb),
        grid=(n // tm,),
        in_specs=[pl.BlockSpec((tm, d), lambda i: (i, 0)), _resident((d, 2 * d)),
                  _resident((d, V7X_LANES)), pl.BlockSpec((1, V7X_LANES), lambda i: (0, 0)),
                  pl.BlockSpec((1, dh), lambda i: (0, 0))],
        out_specs=[
            pl.BlockSpec((1, n_heads, tm, 2 * dh), lambda i: (i // tpb, 0, i % tpb, 0)),
            pl.BlockSpec((1, n_heads, 1, dh, tm), lambda i: (i // tpb, 0, i % tpb, 0, 0)),
        ],
        out_shape=[jax.ShapeDtypeStruct((n_batch, n_heads, seq, 2 * dh), BF16),
                   jax.ShapeDtypeStruct((n_batch, n_heads, seq // tk, dh, tk), BF16)],
        scratch_shapes=[pltpu.VMEM((V7X_SUBLANES, V7X_LANES), F32)],
        compiler_params=_params(1, nbytes),
        name="kv_proj",
    )(h, w, wf, bf, g.reshape(1, dh))


def _attn_kernel(q_ref, k_ref, vt_ref, o_ref, m_s, l_s, acc_s, *, qc):
    tq = q_ref.shape[2]
    n_kv, dh, tk = vt_ref.shape[2:]
    assert tq == tk
    qi = pl.program_id(2)
    n_groups = tq // qc

    m_s[...] = jnp.full(m_s.shape, NEG_INF, F32)
    l_s[...] = jnp.zeros(l_s.shape, F32)
    acc_s[...] = jnp.zeros(acc_s.shape, F32)

    def block(j, diagonal):
        kblk = k_ref[0, 0, pl.ds(pl.multiple_of(j * tk, tk), tk), :]
        vt = vt_ref[0, 0, j]
        for gq in range(n_groups):
            cols = slice(gq * qc, (gq + 1) * qc)
            s = lax.dot_general(kblk, q_ref[0, 0, cols, :], (((1,), (1,)), ((), ())),
                                preferred_element_type=F32)
            if diagonal:
                kpos = lax.broadcasted_iota(jnp.int32, (tk, qc), 0)
                qpos = lax.broadcasted_iota(jnp.int32, (tk, qc), 1) + gq * qc
                s = jnp.where(kpos <= qpos, s, NEG_INF)
            m_old = m_s[0:1, cols]
            m_new = jnp.maximum(m_old, jnp.max(s, axis=0, keepdims=True))
            p = jnp.exp(s - m_new)
            alpha = jnp.exp(m_old - m_new)
            l_s[0:1, cols] = alpha * l_s[0:1, cols] + jnp.sum(p, axis=0, keepdims=True)
            m_s[0:1, cols] = m_new
            pv = jnp.dot(vt, p.astype(vt.dtype), preferred_element_type=F32)
            acc_s[:, cols] = alpha * acc_s[:, cols] + pv

    def full_block(j, carry):
        block(j, False)
        return carry

    lax.fori_loop(0, qi, full_block, 0)
    block(qi, True)

    out = acc_s[...] * (1.0 / l_s[0:1, :])
    o_ref[0] = out.T.astype(o_ref.dtype)


def _attention(qp, kp, vt):
    n_batch, n_heads, seq, dh2 = qp.shape
    n_kv, dh, tk = vt.shape[2:]
    tq = tk
    qc = _tile(tq, 256)
    nbytes = (2 * tq * dh2 * 2 + 2 * seq * dh2 * 2 + 2 * seq * dh * 2 + 2 * tq * dh * 2
              + (dh + 16) * tq * 4 + 6 * tk * qc * 4)
    return pl.pallas_call(
        functools.partial(_attn_kernel, qc=qc),
        grid=(n_batch, n_heads, seq // tq),
        in_specs=[
            pl.BlockSpec((1, 1, tq, dh2), lambda b, h, i: (b, h, i, 0)),
            pl.BlockSpec((1, 1, seq, dh2), lambda b, h, i: (b, h, 0, 0)),
            pl.BlockSpec((1, 1, n_kv, dh, tk), lambda b, h, i: (b, h, 0, 0, 0)),
        ],
        out_specs=pl.BlockSpec((1, tq, dh), lambda b, h, i: (b, i, h)),
        out_shape=jax.ShapeDtypeStruct((n_batch, seq, n_heads * dh), BF16),
        scratch_shapes=[pltpu.VMEM((V7X_SUBLANES, tq), F32), pltpu.VMEM((V7X_SUBLANES, tq), F32),
                        pltpu.VMEM((dh, tq), F32)],
        compiler_params=_params(3, nbytes),
        name="forgetting_attention",
    )(qp, kp, vt)


def kernel(x, c, mix_norm_g, mlp_norm_g, w_mod, b_mod, w_mlp_in, w_mlp_out, lru_w_in, lru_conv_w, lru_conv_b, lru_w_a, lru_b_a, lru_w_i, lru_b_i, lru_lambda, lru_w_out, kv_norm_g, kv_w_mod, kv_b_mod, w_kv, k_norm_g, w_forget, b_forget, attn_w_q, q_norm_g, attn_w_o):
    n_batch, seq, d = x.shape
    depth = w_mod.shape[0]
    n_a = lru_w_in.shape[0]
    n_heads = w_forget.shape[1]
    assert 1 <= n_a < depth and w_mod.shape[2] == N_MOD * d
    n = n_batch * seq
    tk = _tile(seq, 512)

    mod = _modulation(c, w_mod, b_mod)
    kv_mod = _modulation(c, kv_w_mod[None], kv_b_mod[None])[0]
    per_batch = lambda a: a.reshape(n_batch, 1, d)
    sh1, sc1, g1, sh2, sc2, g2 = [[per_batch(mod[l, :, j * d:(j + 1) * d]) for l in range(depth)]
                                  for j in range(N_MOD)]
    kv_shift, kv_scale = per_batch(kv_mod[:, :d]), per_batch(kv_mod[:, d:])

    bf = lambda w: w.astype(BF16)
    x2 = x.reshape(n, d)
    h = _norm_mod(x2, mix_norm_g[0], sc1[0], sh1[0], seq)
    h_kv = kp = vt = None
    for layer in range(depth):
        if layer < n_a:
            a = layer
            xg = _lru_in(h, bf(lru_w_in[a]))
            mix_in = _lru_scan(xg, lru_conv_w[a], lru_conv_b[a], bf(lru_w_a[a]), lru_b_a[a],
                               bf(lru_w_i[a]), lru_b_i[a], lru_lambda[a], n_batch, seq)
            w_o = lru_w_out[a]
        else:
            bl = layer - n_a
            if layer == n_a:
                kp, vt = _kv_proj(h_kv, bf(w_kv), w_forget, b_forget, k_norm_g, n_batch, seq, n_heads, tk)
            qp = _q_proj(h, bf(attn_w_q[bl]), q_norm_g[bl], n_batch, seq, n_heads)
            mix_in = _attention(qp, kp, vt).reshape(n, d)
            w_o = attn_w_o[bl]
        x2, h2 = _proj_res(mix_in, bf(w_o), x2, g1[layer], mlp_norm_g[layer], sc2[layer], sh2[layer], seq)
        norms = []
        if layer + 1 < depth:
            norms.append((mix_norm_g[layer + 1], sc1[layer + 1], sh1[layer + 1]))
        if layer + 1 == n_a:
            norms.append((kv_norm_g, kv_scale, kv_shift))
        x2, hs = _mlp(h2, x2, bf(w_mlp_in[layer]), bf(w_mlp_out[layer]), g2[layer], norms, seq)
        if hs:
            h = hs[0]
        if layer + 1 == n_a:
            h_kv = hs[1]
    return x2.reshape(n_batch, seq, d)
```

```python
import functools
import math

import jax
import jax.numpy as jnp
from jax import lax
from jax.experimental import pallas as pl
from jax.experimental.pallas import tpu as pltpu

F32 = jnp.float32
BF16 = jnp.bfloat16

NORM_EPS = 1e-6
LRU_C = 8.0
NEG_INF = -1e30
N_MOD = 6

V7X_LANES = 128
V7X_SUBLANES = 8
V7X_BF16_ROWS = 16
V7X_VMEM_BYTES = 64 * 1024 * 1024
V7X_VMEM_RESERVE = 6 * 1024 * 1024

N_F_PARTS = 3
LOG2_E = 1.4426950408889634


def _vmem_limit(nbytes):
    return int(min(V7X_VMEM_BYTES - V7X_VMEM_RESERVE, max(nbytes * 5 // 4 + (8 << 20), 32 << 20)))


def _params(n_grid, nbytes):
    return pltpu.CompilerParams(dimension_semantics=("arbitrary",) * n_grid,
                                vmem_limit_bytes=_vmem_limit(nbytes))


def _tile(n, want):
    t = min(n, want)
    while n % t:
        t -= 1
    return t


def _resident(shape):
    nd = len(shape)
    return pl.BlockSpec(shape, lambda *_: (0,) * nd, pipeline_mode=pl.Buffered(1))


def _rms_mod(x, g, scale, shift):
    ms = jnp.mean(x * x, axis=-1, keepdims=True)
    return (x * lax.rsqrt(ms + NORM_EPS)) * (g * (1.0 + scale)) + shift


def _gelu_tanh(x):
    c = math.sqrt(2.0 / math.pi)
    return x * (0.5 * (1.0 + jnp.tanh(c * (x + 0.044715 * (x * x * x)))))


def _sigmoid(x):
    return 1.0 / (1.0 + jnp.exp(-x))


def _log_sigmoid(x):
    return jnp.minimum(x, 0.0) - jnp.log1p(jnp.exp(-jnp.abs(x)))


def _softplus(x):
    return jnp.maximum(x, 0.0) + jnp.log1p(jnp.exp(-jnp.abs(x)))


def _mod_kernel(cb_ref, w_ref, b_ref, o_ref):
    n_batch = cb_ref.shape[0]
    tn = w_ref.shape[2]
    for b in range(n_batch):
        cv = cb_ref[b]
        cs = cv * _sigmoid(cv)
        for ch in range(tn // V7X_LANES):
            cols = slice(ch * V7X_LANES, (ch + 1) * V7X_LANES)
            s = jnp.sum(w_ref[0, :, cols] * cs, axis=0, keepdims=True)
            o_ref[0, b:b + 1, cols] = s + b_ref[0, :, cols]


def _modulation(c, w, b):
    n_l, d, m = w.shape
    n_batch = c.shape[0]
    tn = _tile(m, 1024)
    cb = jnp.broadcast_to(c[:, :, None], (n_batch, d, V7X_LANES))
    nbytes = cb.size * 4 + 2 * d * tn * 4 + 4 * n_batch * tn * 4
    return pl.pallas_call(
        _mod_kernel,
        grid=(n_l, m // tn),
        in_specs=[
            pl.BlockSpec((n_batch, d, V7X_LANES), lambda l, j: (0, 0, 0)),
            pl.BlockSpec((1, d, tn), lambda l, j: (l, 0, j)),
            pl.BlockSpec((1, 1, tn), lambda l, j: (l, 0, j)),
        ],
        out_specs=pl.BlockSpec((1, n_batch, tn), lambda l, j: (l, 0, j)),
        out_shape=jax.ShapeDtypeStruct((n_l, n_batch, m), F32),
        compiler_params=_params(2, nbytes),
        name="modulation",
    )(cb, w, b.reshape(n_l, 1, m))


def _norm_kernel(x_ref, g_ref, sc_ref, sh_ref, o_ref):
    o_ref[...] = _rms_mod(x_ref[...], g_ref[...], sc_ref[0], sh_ref[0]).astype(o_ref.dtype)


def _norm_mod(x2, g, scale, shift, seq):
    n, d = x2.shape
    tm = _tile(seq, 512)
    tpb = seq // tm
    row = lambda i: (i, 0)
    per_batch = lambda i: (i // tpb, 0, 0)
    nbytes = 2 * tm * d * 4 + 2 * tm * d * 2
    return pl.pallas_call(
        _norm_kernel,
        grid=(n // tm,),
        in_specs=[
            pl.BlockSpec((tm, d), row),
            pl.BlockSpec((1, d), lambda i: (0, 0)),
            pl.BlockSpec((1, 1, d), per_batch),
            pl.BlockSpec((1, 1, d), per_batch),
        ],
        out_specs=pl.BlockSpec((tm, d), row),
        out_shape=jax.ShapeDtypeStruct((n, d), BF16),
        compiler_params=_params(1, nbytes),
        name="norm_mod",
    )(x2, g.reshape(1, d), scale, shift)


def _lru_in_kernel(h_ref, w_ref, o_ref, *, tc):
    h = h_ref[...]
    n_cols = w_ref.shape[1]
    for c in range(n_cols // tc):
        cols = slice(c * tc, (c + 1) * tc)
        acc = jnp.dot(h, w_ref[:, cols], preferred_element_type=F32)
        if c * tc >= n_cols // 2:
            acc = _gelu_tanh(acc)
        o_ref[:, cols] = acc


def _lru_in(h, w):
    n, d = h.shape
    m = w.shape[1]
    tm = _tile(n, 512)
    tc = _tile(m // 2, 512)
    nbytes = 2 * tm * d * 2 + d * m * 2 + 2 * tm * m * 4 + 2 * tm * tc * 4
    return pl.pallas_call(
        functools.partial(_lru_in_kernel, tc=tc),
        grid=(n // tm,),
        in_specs=[pl.BlockSpec((tm, d), lambda i: (i, 0)), _resident((d, m))],
        out_specs=pl.BlockSpec((tm, m), lambda i: (i, 0)),
        out_shape=jax.ShapeDtypeStruct((n, m), F32),
        compiler_params=_params(1, nbytes),
        name="lru_in_proj",
    )(h, w)


def _lru_scan_kernel(xb_ref, gy_ref, cw_ref, cb_ref, wa_ref, ba_ref, wi_ref, bi_ref, lam_ref,
                     o_ref, xpad_s, a_s, u_s, h_s, *, lane_chunk):
    ts, width = xb_ref.shape
    conv_width = cw_ref.shape[0]
    n_blocks, bd, _ = wa_ref.shape
    halo = V7X_SUBLANES
    assert conv_width - 1 <= halo

    @pl.when(pl.program_id(1) == 0)
    def _():
        xpad_s[0:halo, :] = jnp.zeros((halo, width), F32)
        h_s[...] = jnp.zeros(h_s.shape, F32)

    xpad_s[halo:halo + ts, :] = xb_ref[...]

    decay = -LRU_C * _softplus(-lam_ref[...])

    for nb in range(n_blocks):
        cols = slice(nb * bd, (nb + 1) * bd)
        xc = cb_ref[:, cols] + cw_ref[conv_width - 1:conv_width, cols] * xpad_s[halo:halo + ts, cols]
        for k in range(conv_width - 1):
            back = conv_width - 1 - k
            xc = xc + cw_ref[k:k + 1, cols] * xpad_s[halo - back:halo - back + ts, cols]
        xcb = xc.astype(BF16)
        r = _sigmoid(jnp.dot(xcb, wa_ref[nb], preferred_element_type=F32) + ba_ref[:, cols])
        gi = _sigmoid(jnp.dot(xcb, wi_ref[nb], preferred_element_type=F32) + bi_ref[:, cols])
        log_a = decay[:, cols] * r
        a = jnp.exp(log_a)
        a_s[:, cols] = a
        u_s[:, cols] = jnp.sqrt(-jnp.tanh(log_a) * (a * a + 1.0)) * (gi * xc)

    xpad_s[0:halo, :] = xpad_s[ts:ts + halo, :]

    rows16 = V7X_BF16_ROWS
    row_id = lax.broadcasted_iota(jnp.int32, (V7X_SUBLANES, lane_chunk), 0)

    def group(g, carry):
        r0 = pl.multiple_of(g * rows16, rows16)
        for lc in range(width // lane_chunk):
            cols = slice(lc * lane_chunk, (lc + 1) * lane_chunk)
            hprev = h_s[:, cols]
            halves = []
            for half in range(rows16 // V7X_SUBLANES):
                rr = pl.ds(r0 + half * V7X_SUBLANES, V7X_SUBLANES)
                a = a_s[rr, cols]
                u = u_s[rr, cols]
                for sh in (1, 2, 4):
                    a_sh = pltpu.roll(a, sh, axis=0)
                    u_sh = pltpu.roll(u, sh, axis=0)
                    live = row_id >= sh
                    u = jnp.where(live, a * u_sh + u, u)
                    a = jnp.where(live, a * a_sh, a)
                hcur = a * hprev + u
                hprev = jnp.broadcast_to(hcur[V7X_SUBLANES - 1:V7X_SUBLANES, :], hcur.shape)
                halves.append(hcur)
            h_s[:, cols] = hprev
            h16 = jnp.concatenate(halves, axis=0)
            o_ref[pl.ds(r0, rows16), cols] = (h16 * gy_ref[pl.ds(r0, rows16), cols]).astype(o_ref.dtype)
        return carry

    lax.fori_loop(0, ts // rows16, group, 0)


def _lru_scan(xg, conv_w, conv_b, w_a, b_a, w_i, b_i, lam, n_batch, seq):
    n, two_w = xg.shape
    width = two_w // 2
    ts = _tile(seq, 256)
    tpb = seq // ts
    n_blocks, bd, _ = w_a.shape
    lane_chunk = _tile(width, 512)
    row = lambda b, t: (b * tpb + t, 0)
    vec = lambda a: a.reshape(1, width)
    small = lambda shape: pl.BlockSpec(shape, lambda b, t: (0,) * len(shape))
    nbytes = (4 * ts * width * 4 + 2 * ts * width * 2 + (3 * ts + 16) * width * 4
              + 4 * n_blocks * bd * bd * 2)
    return pl.pallas_call(
        functools.partial(_lru_scan_kernel, lane_chunk=lane_chunk),
        grid=(n_batch, tpb),
        in_specs=[
            pl.BlockSpec((ts, width), row),
            pl.BlockSpec((ts, width), lambda b, t: (b * tpb + t, 1)),
            small(conv_w.shape), small((1, width)),
            small(w_a.shape), small((1, width)),
            small(w_i.shape), small((1, width)),
            small((1, width)),
        ],
        out_specs=pl.BlockSpec((ts, width), row),
        out_shape=jax.ShapeDtypeStruct((n, width), BF16),
        scratch_shapes=[
            pltpu.VMEM((ts + 2 * V7X_SUBLANES, width), F32),
            pltpu.VMEM((ts, width), F32),
            pltpu.VMEM((ts, width), F32),
            pltpu.VMEM((V7X_SUBLANES, width), F32),
        ],
        compiler_params=_params(2, nbytes),
        name="lru_scan",
    )(xg, xg, conv_w, vec(conv_b), w_a, vec(b_a), w_i, vec(b_i), vec(lam))


def _proj_res_kernel(a_ref, w_ref, x_ref, g1_ref, ng_ref, sc_ref, sh_ref, xo_ref, h_ref, *, rc):
    tm = a_ref.shape[0]
    for r in range(tm // rc):
        rows = slice(r * rc, (r + 1) * rc)
        acc = jnp.dot(a_ref[rows, :], w_ref[...], preferred_element_type=F32)
        xn = x_ref[rows, :] + g1_ref[0] * acc
        xo_ref[rows, :] = xn
        h_ref[rows, :] = _rms_mod(xn, ng_ref[...], sc_ref[0], sh_ref[0]).astype(h_ref.dtype)


def _proj_res(a, w, x2, gate, norm_g, scale, shift, seq):
    n, k = a.shape
    d = w.shape[1]
    tm = _tile(seq, 512)
    rc = _tile(tm, 256)
    tpb = seq // tm
    row = lambda i: (i, 0)
    per_batch = lambda i: (i // tpb, 0, 0)
    nbytes = 2 * tm * k * 2 + k * d * 2 + 4 * tm * d * 4 + 2 * tm * d * 2 + 3 * rc * d * 4
    return pl.pallas_call(
        functools.partial(_proj_res_kernel, rc=rc),
        grid=(n // tm,),
        in_specs=[
            pl.BlockSpec((tm, k), row), _resident((k, d)), pl.BlockSpec((tm, d), row),
            pl.BlockSpec((1, 1, d), per_batch), pl.BlockSpec((1, d), lambda i: (0, 0)),
            pl.BlockSpec((1, 1, d), per_batch), pl.BlockSpec((1, 1, d), per_batch),
        ],
        out_specs=[pl.BlockSpec((tm, d), row), pl.BlockSpec((tm, d), row)],
        out_shape=[jax.ShapeDtypeStruct((n, d), F32), jax.ShapeDtypeStruct((n, d), BF16)],
        compiler_params=_params(1, nbytes),
        name="mix_out_proj",
    )(a, w, x2, gate, norm_g.reshape(1, d), scale, shift)


def _mlp_kernel(*refs, n_norm, rc):
    h_ref, win_ref, wout_ref, x_ref, g2_ref = refs[:5]
    norm_refs = refs[5:5 + 3 * n_norm]
    xo_ref = refs[5 + 3 * n_norm]
    ho_refs = refs[6 + 3 * n_norm:6 + 4 * n_norm]
    acc_s = refs[6 + 4 * n_norm]
    k = pl.program_id(1)

    a = jnp.dot(h_ref[...], win_ref[...], preferred_element_type=F32)
    a = jnp.square(jnp.maximum(a, 0.0)).astype(BF16)

    @pl.when(k == 0)
    def _():
        acc_s[...] = jnp.dot(a, wout_ref[...], preferred_element_type=F32)

    @pl.when(k > 0)
    def _():
        acc_s[...] += jnp.dot(a, wout_ref[...], preferred_element_type=F32)

    @pl.when(k == pl.num_programs(1) - 1)
    def _():
        tm = x_ref.shape[0]

        def rows_body(r, carry):
            rows = pl.ds(pl.multiple_of(r * rc, rc), rc)
            xn = x_ref[rows, :] + g2_ref[0] * acc_s[rows, :]
            xo_ref[rows, :] = xn
            for j in range(n_norm):
                ng, sc, sh = norm_refs[3 * j:3 * j + 3]
                ho_refs[j][rows, :] = _rms_mod(xn, ng[...], sc[0], sh[0]).astype(BF16)
            return carry

        lax.fori_loop(0, tm // rc, rows_body, 0)


def _mlp(h2, x2, w_in, w_out, gate, norms, seq):
    n, d = x2.shape
    f = w_in.shape[1]
    tm = _tile(seq, 512)
    tf = _tile(f, 512)
    rc = _tile(tm, 64)
    tpb = seq // tm
    n_norm = len(norms)
    row = lambda i, k: (i, 0)
    per_batch = lambda i, k: (i // tpb, 0, 0)
    in_specs = [
        pl.BlockSpec((tm, d), row),
        pl.BlockSpec((d, tf), lambda i, k: (0, k)),
        pl.BlockSpec((tf, d), lambda i, k: (k, 0)),
        pl.BlockSpec((tm, d), row),
        pl.BlockSpec((1, 1, d), per_batch),
    ]
    args = [h2, w_in, w_out, x2, gate]
    for ng, sc, sh in norms:
        in_specs += [pl.BlockSpec((1, d), lambda i, k: (0, 0)),
                     pl.BlockSpec((1, 1, d), per_batch), pl.BlockSpec((1, 1, d), per_batch)]
        args += [ng.reshape(1, d), sc, sh]
    out_specs = [pl.BlockSpec((tm, d), row)] * (1 + n_norm)
    out_shape = [jax.ShapeDtypeStruct((n, d), F32)] + [jax.ShapeDtypeStruct((n, d), BF16)] * n_norm
    nbytes = (2 * tm * d * 2 + 4 * d * tf * 2 + 4 * tm * d * 4 + tm * d * 4
              + 2 * n_norm * tm * d * 2 + 2 * tm * tf * 4 + tm * d * 4)
    outs = pl.pallas_call(
        functools.partial(_mlp_kernel, n_norm=n_norm, rc=rc),
        grid=(n // tm, f // tf),
        in_specs=in_specs,
        out_specs=out_specs,
        out_shape=out_shape,
        scratch_shapes=[pltpu.VMEM((tm, d), F32)],
        compiler_params=_params(2, nbytes),
        name="relu2_mlp",
    )(*args)
    return outs[0], list(outs[1:])


def _head_norm(a, g):
    ms = jnp.mean(a * a, axis=-1, keepdims=True)
    return a * lax.rsqrt(ms + NORM_EPS) * g


def _q_kernel(h_ref, w_ref, g_ref, o_ref, *, hc, scale):
    tm = h_ref.shape[0]
    n_heads, dh = o_ref.shape[1], g_ref.shape[1]
    h = h_ref[...]
    row = lax.broadcasted_iota(jnp.int32, (dh, tm), 0)
    ones_rows = jnp.where(row < N_F_PARTS, 1.0, 0.0).astype(o_ref.dtype)
    gs = g_ref[...] * scale
    for c in range(n_heads // hc):
        acc = jnp.dot(h, w_ref[:, c * hc * dh:(c + 1) * hc * dh], preferred_element_type=F32)
        for hh in range(hc):
            head = c * hc + hh
            qn = _head_norm(acc[:, hh * dh:(hh + 1) * dh], gs)
            o_ref[0, head, 0:dh, :] = qn.T.astype(o_ref.dtype)
            o_ref[0, head, dh:2 * dh, :] = ones_rows


def _q_proj(h, w, g, n_batch, seq, n_heads):
    n, d = h.shape
    dh = d // n_heads
    tm = _tile(seq, 512)
    tpb = seq // tm
    hc = _tile(n_heads, 4)
    nbytes = 2 * tm * d * 2 + d * d * 2 + 2 * n_heads * tm * 2 * dh * 2 + 2 * tm * hc * dh * 4
    return pl.pallas_call(
        functools.partial(_q_kernel, hc=hc, scale=dh ** -0.5 * LOG2_E),
        grid=(n // tm,),
        in_specs=[pl.BlockSpec((tm, d), lambda i: (i, 0)), _resident((d, d)),
                  pl.BlockSpec((1, dh), lambda i: (0, 0))],
        out_specs=pl.BlockSpec((1, n_heads, 2 * dh, tm), lambda i: (i // tpb, 0, 0, i % tpb)),
        out_shape=jax.ShapeDtypeStruct((n_batch, n_heads, 2 * dh, seq), BF16),
        compiler_params=_params(1, nbytes),
        name="q_proj",
    )(h, w, g.reshape(1, dh))


def _kv_kernel(h_ref, w_ref, wf_ref, bf_ref, g_ref, ko_ref, vo_ref, carry_s, *, hc, tpb):
    tm, d = h_ref.shape
    n_heads, dh = ko_ref.shape[1], g_ref.shape[1]

    @pl.when(pl.program_id(0) % tpb == 0)
    def _():
        carry_s[...] = jnp.zeros(carry_s.shape, F32)

    h = h_ref[...]

    z = jnp.dot(h, wf_ref[...], preferred_element_type=F32) + bf_ref[...]
    log_f = _log_sigmoid(z)
    ri = lax.broadcasted_iota(jnp.int32, (tm, tm), 0)
    ci = lax.broadcasted_iota(jnp.int32, (tm, tm), 1)
    tri = jnp.where(ri >= ci, 1.0, 0.0).astype(F32)
    cum = jnp.dot(tri, log_f, preferred_element_type=F32, precision=lax.Precision.HIGHEST)
    cum = cum + carry_s[0:1, :]
    carry_s[...] = jnp.broadcast_to(cum[tm - 1:tm, :], carry_s.shape)

    parts = []
    rest = -LOG2_E * cum
    for _ in range(N_F_PARTS):
        piece = rest.astype(BF16).astype(F32)
        parts.append(piece)
        rest = rest - piece

    lane = lax.broadcasted_iota(jnp.int32, (tm, dh), 1)
    g = g_ref[...]
    for c in range(n_heads // hc):
        acc = jnp.dot(h, w_ref[:, c * hc * dh:(c + 1) * hc * dh], preferred_element_type=F32)
        for hh in range(hc):
            head = c * hc + hh
            kn = _head_norm(acc[:, hh * dh:(hh + 1) * dh], g)
            ko_ref[0, head, :, 0:dh] = kn.astype(ko_ref.dtype)
            aug = jnp.zeros((tm, dh), F32)
            for p in range(N_F_PARTS):
                col = jnp.broadcast_to(parts[p][:, head:head + 1], (tm, dh))
                aug = jnp.where(lane == p, col, aug)
            ko_ref[0, head, :, dh:2 * dh] = aug.astype(ko_ref.dtype)
    for c in range(n_heads // hc):
        acc = jnp.dot(h, w_ref[:, d + c * hc * dh:d + (c + 1) * hc * dh], preferred_element_type=F32)
        for hh in range(hc):
            head = c * hc + hh
            vo_ref[0, head, 0] = acc[:, hh * dh:(hh + 1) * dh].T.astype(vo_ref.dtype)


def _kv_proj(h, w, w_forget, b_forget, g, n_batch, seq, n_heads, tk):
    n, d = h.shape
    dh = d // n_heads
    tm = tk
    tpb = seq // tm
    hc = _tile(n_heads, 4)
    assert n_heads <= V7X_LANES
    wf = jnp.zeros((d, V7X_LANES), BF16).at[:, :n_heads].set(w_forget.astype(BF16))
    bf = jnp.zeros((1, V7X_LANES), F32).at[0, :n_heads].set(b_forget)
    nbytes = (2 * tm * d * 2 + 2 * d * d * 2 + 2 * n_heads * tm * 3 * dh * 2
              + 2 * tm * hc * dh * 4 + 3 * tm * tm * 4)
    return pl.pallas_call(
        functools.partial(_kv_kernel, hc=hc, tpb=tpb),
        grid=(n // tm,),
        in_specs=[pl.BlockSpec((tm, d), lambda i: (i, 0)), _resident((d, 2 * d)),
                  _resident((d, V7X_LANES)), pl.BlockSpec((1, V7X_LANES), lambda i: (0, 0)),
                  pl.BlockSpec((1, dh), lambda i: (0, 0))],
        out_specs=[
            pl.BlockSpec((1, n_heads, tm, 2 * dh), lambda i: (i // tpb, 0, i % tpb, 0)),
            pl.BlockSpec((1, n_heads, 1, dh, tm), lambda i: (i // tpb, 0, i % tpb, 0, 0)),
        ],
        out_shape=[jax.ShapeDtypeStruct((n_batch, n_heads, seq, 2 * dh), BF16),
                   jax.ShapeDtypeStruct((n_batch, n_heads, seq // tk, dh, tk), BF16)],
        scratch_shapes=[pltpu.VMEM((V7X_SUBLANES, V7X_LANES), F32)],
        compiler_params=_params(1, nbytes),
        name="kv_proj",
    )(h, w, wf, bf, g.reshape(1, dh))


def _attn_kernel(q_ref, k_ref, vt_ref, o_ref, s_s, p_s, acc_s):
    tq = q_ref.shape[3]
    n_kv, dh, tk = vt_ref.shape[2:]
    assert tq == 2 * tk
    qi = pl.program_id(2)
    q = q_ref[0, 0]
    key_minus_query = (lax.broadcasted_iota(jnp.int32, (tk, tq), 0)
                       - lax.broadcasted_iota(jnp.int32, (tk, tq), 1))

    def scores(j):
        kblk = k_ref[0, 0, pl.ds(pl.multiple_of(j * tk, tk), tk), :]
        return jnp.dot(kblk, q, preferred_element_type=F32)

    def softmax(s, m_old, l_old, j, masked):
        if masked:
            s = jnp.where(key_minus_query <= qi * tq - j * tk, s, NEG_INF)
        m_new = jnp.maximum(m_old, jnp.max(s, axis=0, keepdims=True))
        p = jnp.exp2(s - m_new)
        alpha = jnp.exp2(m_old - m_new)
        l_new = alpha * l_old + jnp.sum(p, axis=0, keepdims=True)
        return p.astype(p_s.dtype), alpha, m_new, l_new

    def add_values(p, alpha, j):
        acc_s[...] = alpha * acc_s[...] + jnp.dot(vt_ref[0, 0, j], p, preferred_element_type=F32)

    def pair(u, carry, last):
        m, l, alpha_b = carry
        a = 2 * u
        s_a = s_s[...]
        s_b = scores(a + 1)
        p_a, alpha_a, m, l = softmax(s_a, m, l, a, last)
        add_values(p_s[...], alpha_b, jnp.maximum(a - 1, 0))
        if not last:
            s_s[...] = scores(a + 2)
        p_b, alpha_b, m, l = softmax(s_b, m, l, a + 1, last)
        add_values(p_a, alpha_a, a)
        if last:
            add_values(p_b, alpha_b, a + 1)
        else:
            p_s[...] = p_b
        return m, l, alpha_b

    def two_pairs(v, carry):
        return pair(2 * v + 1, pair(2 * v, carry, False), False)

    s_s[...] = scores(0)
    p_s[...] = jnp.zeros(p_s.shape, p_s.dtype)
    acc_s[...] = jnp.zeros(acc_s.shape, F32)
    carry = (jnp.full((1, tq), NEG_INF, F32), jnp.zeros((1, tq), F32), jnp.ones((1, tq), F32))
    carry = lax.fori_loop(0, qi // 2, two_pairs, carry)
    carry = lax.fori_loop(0, qi % 2, lambda _, cr: pair(qi - 1, cr, False), carry)
    _, l, _ = pair(qi, carry, True)

    out = acc_s[...] * (1.0 / l)
    o_ref[0] = out.T.astype(o_ref.dtype)


def _attention(qp, kp, vt):
    n_batch, n_heads, dh2, seq = qp.shape
    n_kv, dh, tk = vt.shape[2:]
    tq = 2 * tk
    nbytes = (2 * tq * dh2 * 2 + 2 * seq * dh2 * 2 + 2 * seq * dh * 2 + 2 * tq * dh * 2
              + dh * tq * 4 + tk * tq * 6 + 4 * tk * tq * 4)
    return pl.pallas_call(
        _attn_kernel,
        grid=(n_batch, n_heads, seq // tq),
        in_specs=[
            pl.BlockSpec((1, 1, dh2, tq), lambda b, h, i: (b, h, 0, i)),
            pl.BlockSpec((1, 1, seq, dh2), lambda b, h, i: (b, h, 0, 0)),
            pl.BlockSpec((1, 1, n_kv, dh, tk), lambda b, h, i: (b, h, 0, 0, 0)),
        ],
        out_specs=pl.BlockSpec((1, tq, dh), lambda b, h, i: (b, i, h)),
        out_shape=jax.ShapeDtypeStruct((n_batch, seq, n_heads * dh), BF16),
        scratch_shapes=[pltpu.VMEM((tk, tq), F32), pltpu.VMEM((tk, tq), BF16),
                        pltpu.VMEM((dh, tq), F32)],
        compiler_params=_params(3, nbytes),
        name="forgetting_attention",
    )(qp, kp, vt)


def kernel(x, c, mix_norm_g, mlp_norm_g, w_mod, b_mod, w_mlp_in, w_mlp_out, lru_w_in, lru_conv_w, lru_conv_b, lru_w_a, lru_b_a, lru_w_i, lru_b_i, lru_lambda, lru_w_out, kv_norm_g, kv_w_mod, kv_b_mod, w_kv, k_norm_g, w_forget, b_forget, attn_w_q, q_norm_g, attn_w_o):
    n_batch, seq, d = x.shape
    depth = w_mod.shape[0]
    n_a = lru_w_in.shape[0]
    n_heads = w_forget.shape[1]
    assert 1 <= n_a < depth and w_mod.shape[2] == N_MOD * d
    n = n_batch * seq
    tk = _tile(seq, 512)

    mod = _modulation(c, w_mod, b_mod)
    kv_mod = _modulation(c, kv_w_mod[None], kv_b_mod[None])[0]
    per_batch = lambda a: a.reshape(n_batch, 1, d)
    sh1, sc1, g1, sh2, sc2, g2 = [[per_batch(mod[l, :, j * d:(j + 1) * d]) for l in range(depth)]
                                  for j in range(N_MOD)]
    kv_shift, kv_scale = per_batch(kv_mod[:, :d]), per_batch(kv_mod[:, d:])

    bf = lambda w: w.astype(BF16)
    x2 = x.reshape(n, d)
    h = _norm_mod(x2, mix_norm_g[0], sc1[0], sh1[0], seq)
    h_kv = kp = vt = None
    for layer in range(depth):
        if layer < n_a:
            a = layer
            xg = _lru_in(h, bf(lru_w_in[a]))
            mix_in = _lru_scan(xg, lru_conv_w[a], lru_conv_b[a], bf(lru_w_a[a]), lru_b_a[a],
                               bf(lru_w_i[a]), lru_b_i[a], lru_lambda[a], n_batch, seq)
            w_o = lru_w_out[a]
        else:
            bl = layer - n_a
            if layer == n_a:
                kp, vt = _kv_proj(h_kv, bf(w_kv), w_forget, b_forget, k_norm_g, n_batch, seq, n_heads, tk)
            qp = _q_proj(h, bf(attn_w_q[bl]), q_norm_g[bl], n_batch, seq, n_heads)
            mix_in = _attention(qp, kp, vt).reshape(n, d)
            w_o = attn_w_o[bl]
        x2, h2 = _proj_res(mix_in, bf(w_o), x2, g1[layer], mlp_norm_g[layer], sc2[layer], sh2[layer], seq)
        norms = []
        if layer + 1 < depth:
            norms.append((mix_norm_g[layer + 1], sc1[layer + 1], sh1[layer + 1]))
        if layer + 1 == n_a:
            norms.append((kv_norm_g, kv_scale, kv_shift))
        x2, hs = _mlp(h2, x2, bf(w_mlp_in[layer]), bf(w_mlp_out[layer]), g2[layer], norms, seq)
        if hs:
            h = hs[0]
        if layer + 1 == n_a:
            h_kv = hs[1]
    return x2.reshape(n_batch, seq, d)
```

```python
import functools
import math

import jax
import jax.numpy as jnp
from jax import lax
from jax.experimental import pallas as pl
from jax.experimental.pallas import tpu as pltpu

F32 = jnp.float32
BF16 = jnp.bfloat16

NORM_EPS = 1e-6
LRU_C = 8.0
NEG_INF = -1e30
N_MOD = 6

V7X_LANES = 128
V7X_SUBLANES = 8
V7X_BF16_ROWS = 16
V7X_VMEM_BYTES = 64 * 1024 * 1024
V7X_VMEM_RESERVE = 6 * 1024 * 1024

N_F_PARTS = 3
LOG2_E = 1.4426950408889634


def _vmem_limit(nbytes):
    return int(min(V7X_VMEM_BYTES - V7X_VMEM_RESERVE, max(nbytes * 5 // 4 + (8 << 20), 32 << 20)))


def _params(n_grid, nbytes):
    return pltpu.CompilerParams(dimension_semantics=("arbitrary",) * n_grid,
                                vmem_limit_bytes=_vmem_limit(nbytes))


def _tile(n, want):
    t = min(n, want)
    while n % t:
        t -= 1
    return t


def _resident(shape):
    nd = len(shape)
    return pl.BlockSpec(shape, lambda *_: (0,) * nd, pipeline_mode=pl.Buffered(1))


def _rms_mod(x, g, scale, shift):
    ms = jnp.mean(x * x, axis=-1, keepdims=True)
    return (x * lax.rsqrt(ms + NORM_EPS)) * (g * (1.0 + scale)) + shift


def _gelu_tanh(x):
    c = math.sqrt(2.0 / math.pi)
    return x * (0.5 * (1.0 + jnp.tanh(c * (x + 0.044715 * (x * x * x)))))


def _sigmoid(x):
    return 1.0 / (1.0 + jnp.exp(-x))


def _log_sigmoid(x):
    return jnp.minimum(x, 0.0) - jnp.log1p(jnp.exp(-jnp.abs(x)))


def _softplus(x):
    return jnp.maximum(x, 0.0) + jnp.log1p(jnp.exp(-jnp.abs(x)))


def _mod_kernel(cb_ref, w_ref, b_ref, o_ref):
    n_batch = cb_ref.shape[0]
    tn = w_ref.shape[2]
    for b in range(n_batch):
        cv = cb_ref[b]
        cs = cv * _sigmoid(cv)
        for ch in range(tn // V7X_LANES):
            cols = slice(ch * V7X_LANES, (ch + 1) * V7X_LANES)
            s = jnp.sum(w_ref[0, :, cols] * cs, axis=0, keepdims=True)
            o_ref[0, b:b + 1, cols] = s + b_ref[0, :, cols]


def _modulation(c, w, b):
    n_l, d, m = w.shape
    n_batch = c.shape[0]
    tn = _tile(m, 1024)
    cb = jnp.broadcast_to(c[:, :, None], (n_batch, d, V7X_LANES))
    nbytes = cb.size * 4 + 2 * d * tn * 4 + 4 * n_batch * tn * 4
    return pl.pallas_call(
        _mod_kernel,
        grid=(n_l, m // tn),
        in_specs=[
            pl.BlockSpec((n_batch, d, V7X_LANES), lambda l, j: (0, 0, 0)),
            pl.BlockSpec((1, d, tn), lambda l, j: (l, 0, j)),
            pl.BlockSpec((1, 1, tn), lambda l, j: (l, 0, j)),
        ],
        out_specs=pl.BlockSpec((1, n_batch, tn), lambda l, j: (l, 0, j)),
        out_shape=jax.ShapeDtypeStruct((n_l, n_batch, m), F32),
        compiler_params=_params(2, nbytes),
        name="modulation",
    )(cb, w, b.reshape(n_l, 1, m))


def _norm_kernel(x_ref, g_ref, sc_ref, sh_ref, o_ref):
    o_ref[...] = _rms_mod(x_ref[...], g_ref[...], sc_ref[0], sh_ref[0]).astype(o_ref.dtype)


def _norm_mod(x2, g, scale, shift, seq):
    n, d = x2.shape
    tm = _tile(seq, 512)
    tpb = seq // tm
    row = lambda i: (i, 0)
    per_batch = lambda i: (i // tpb, 0, 0)
    nbytes = 2 * tm * d * 4 + 2 * tm * d * 2
    return pl.pallas_call(
        _norm_kernel,
        grid=(n // tm,),
        in_specs=[
            pl.BlockSpec((tm, d), row),
            pl.BlockSpec((1, d), lambda i: (0, 0)),
            pl.BlockSpec((1, 1, d), per_batch),
            pl.BlockSpec((1, 1, d), per_batch),
        ],
        out_specs=pl.BlockSpec((tm, d), row),
        out_shape=jax.ShapeDtypeStruct((n, d), BF16),
        compiler_params=_params(1, nbytes),
        name="norm_mod",
    )(x2, g.reshape(1, d), scale, shift)


def _lru_in_kernel(h_ref, w_ref, o_ref, *, tc):
    h = h_ref[...]
    n_cols = w_ref.shape[1]
    for c in range(n_cols // tc):
        cols = slice(c * tc, (c + 1) * tc)
        acc = jnp.dot(h, w_ref[:, cols], preferred_element_type=F32)
        if c * tc >= n_cols // 2:
            acc = _gelu_tanh(acc)
        o_ref[:, cols] = acc


def _lru_in(h, w):
    n, d = h.shape
    m = w.shape[1]
    tm = _tile(n, 512)
    tc = _tile(m // 2, 512)
    nbytes = 2 * tm * d * 2 + d * m * 2 + 2 * tm * m * 4 + 2 * tm * tc * 4
    return pl.pallas_call(
        functools.partial(_lru_in_kernel, tc=tc),
        grid=(n // tm,),
        in_specs=[pl.BlockSpec((tm, d), lambda i: (i, 0)), _resident((d, m))],
        out_specs=pl.BlockSpec((tm, m), lambda i: (i, 0)),
        out_shape=jax.ShapeDtypeStruct((n, m), F32),
        compiler_params=_params(1, nbytes),
        name="lru_in_proj",
    )(h, w)


def _lru_scan_kernel(xb_ref, gy_ref, cw_ref, cb_ref, wa_ref, ba_ref, wi_ref, bi_ref, lam_ref,
                     o_ref, xpad_s, a_s, u_s, h_s, *, lane_chunk):
    ts, width = xb_ref.shape
    conv_width = cw_ref.shape[0]
    n_blocks, bd, _ = wa_ref.shape
    halo = V7X_SUBLANES
    assert conv_width - 1 <= halo

    @pl.when(pl.program_id(1) == 0)
    def _():
        xpad_s[0:halo, :] = jnp.zeros((halo, width), F32)
        h_s[...] = jnp.zeros(h_s.shape, F32)

    xpad_s[halo:halo + ts, :] = xb_ref[...]

    decay = -LRU_C * _softplus(-lam_ref[...])

    for nb in range(n_blocks):
        cols = slice(nb * bd, (nb + 1) * bd)
        xc = cb_ref[:, cols] + cw_ref[conv_width - 1:conv_width, cols] * xpad_s[halo:halo + ts, cols]
        for k in range(conv_width - 1):
            back = conv_width - 1 - k
            xc = xc + cw_ref[k:k + 1, cols] * xpad_s[halo - back:halo - back + ts, cols]
        xcb = xc.astype(BF16)
        r = _sigmoid(jnp.dot(xcb, wa_ref[nb], preferred_element_type=F32) + ba_ref[:, cols])
        gi = _sigmoid(jnp.dot(xcb, wi_ref[nb], preferred_element_type=F32) + bi_ref[:, cols])
        log_a = decay[:, cols] * r
        a = jnp.exp(log_a)
        a_s[:, cols] = a
        u_s[:, cols] = jnp.sqrt(-jnp.tanh(log_a) * (a * a + 1.0)) * (gi * xc)

    xpad_s[0:halo, :] = xpad_s[ts:ts + halo, :]

    rows16 = V7X_BF16_ROWS
    row_id = lax.broadcasted_iota(jnp.int32, (V7X_SUBLANES, lane_chunk), 0)

    def group(g, carry):
        r0 = pl.multiple_of(g * rows16, rows16)
        for lc in range(width // lane_chunk):
            cols = slice(lc * lane_chunk, (lc + 1) * lane_chunk)
            hprev = h_s[:, cols]
            halves = []
            for half in range(rows16 // V7X_SUBLANES):
                rr = pl.ds(r0 + half * V7X_SUBLANES, V7X_SUBLANES)
                a = a_s[rr, cols]
                u = u_s[rr, cols]
                for sh in (1, 2, 4):
                    a_sh = pltpu.roll(a, sh, axis=0)
                    u_sh = pltpu.roll(u, sh, axis=0)
                    live = row_id >= sh
                    u = jnp.where(live, a * u_sh + u, u)
                    a = jnp.where(live, a * a_sh, a)
                hcur = a * hprev + u
                hprev = jnp.broadcast_to(hcur[V7X_SUBLANES - 1:V7X_SUBLANES, :], hcur.shape)
                halves.append(hcur)
            h_s[:, cols] = hprev
            h16 = jnp.concatenate(halves, axis=0)
            o_ref[pl.ds(r0, rows16), cols] = (h16 * gy_ref[pl.ds(r0, rows16), cols]).astype(o_ref.dtype)
        return carry

    lax.fori_loop(0, ts // rows16, group, 0)


def _lru_scan(xg, conv_w, conv_b, w_a, b_a, w_i, b_i, lam, n_batch, seq):
    n, two_w = xg.shape
    width = two_w // 2
    ts = _tile(seq, 256)
    tpb = seq // ts
    n_blocks, bd, _ = w_a.shape
    lane_chunk = _tile(width, 512)
    row = lambda b, t: (b * tpb + t, 0)
    vec = lambda a: a.reshape(1, width)
    small = lambda shape: pl.BlockSpec(shape, lambda b, t: (0,) * len(shape))
    nbytes = (4 * ts * width * 4 + 2 * ts * width * 2 + (3 * ts + 16) * width * 4
              + 4 * n_blocks * bd * bd * 2)
    return pl.pallas_call(
        functools.partial(_lru_scan_kernel, lane_chunk=lane_chunk),
        grid=(n_batch, tpb),
        in_specs=[
            pl.BlockSpec((ts, width), row),
            pl.BlockSpec((ts, width), lambda b, t: (b * tpb + t, 1)),
            small(conv_w.shape), small((1, width)),
            small(w_a.shape), small((1, width)),
            small(w_i.shape), small((1, width)),
            small((1, width)),
        ],
        out_specs=pl.BlockSpec((ts, width), row),
        out_shape=jax.ShapeDtypeStruct((n, width), BF16),
        scratch_shapes=[
            pltpu.VMEM((ts + 2 * V7X_SUBLANES, width), F32),
            pltpu.VMEM((ts, width), F32),
            pltpu.VMEM((ts, width), F32),
            pltpu.VMEM((V7X_SUBLANES, width), F32),
        ],
        compiler_params=_params(2, nbytes),
        name="lru_scan",
    )(xg, xg, conv_w, vec(conv_b), w_a, vec(b_a), w_i, vec(b_i), vec(lam))


def _proj_res_kernel(a_ref, w_ref, x_ref, g1_ref, ng_ref, sc_ref, sh_ref, xo_ref, h_ref, *, rc):
    tm = a_ref.shape[0]
    for r in range(tm // rc):
        rows = slice(r * rc, (r + 1) * rc)
        acc = jnp.dot(a_ref[rows, :], w_ref[...], preferred_element_type=F32)
        xn = x_ref[rows, :] + g1_ref[0] * acc
        xo_ref[rows, :] = xn
        h_ref[rows, :] = _rms_mod(xn, ng_ref[...], sc_ref[0], sh_ref[0]).astype(h_ref.dtype)


def _proj_res(a, w, x2, gate, norm_g, scale, shift, seq):
    n, k = a.shape
    d = w.shape[1]
    tm = _tile(seq, 512)
    rc = _tile(tm, 256)
    tpb = seq // tm
    row = lambda i: (i, 0)
    per_batch = lambda i: (i // tpb, 0, 0)
    nbytes = 2 * tm * k * 2 + k * d * 2 + 4 * tm * d * 4 + 2 * tm * d * 2 + 3 * rc * d * 4
    return pl.pallas_call(
        functools.partial(_proj_res_kernel, rc=rc),
        grid=(n // tm,),
        in_specs=[
            pl.BlockSpec((tm, k), row), _resident((k, d)), pl.BlockSpec((tm, d), row),
            pl.BlockSpec((1, 1, d), per_batch), pl.BlockSpec((1, d), lambda i: (0, 0)),
            pl.BlockSpec((1, 1, d), per_batch), pl.BlockSpec((1, 1, d), per_batch),
        ],
        out_specs=[pl.BlockSpec((tm, d), row), pl.BlockSpec((tm, d), row)],
        out_shape=[jax.ShapeDtypeStruct((n, d), F32), jax.ShapeDtypeStruct((n, d), BF16)],
        compiler_params=_params(1, nbytes),
        name="mix_out_proj",
    )(a, w, x2, gate, norm_g.reshape(1, d), scale, shift)


def _mlp_kernel(*refs, n_norm, rc, fc):
    h_ref, win_ref, wout_ref, x_ref, g2_ref = refs[:5]
    norm_refs = refs[5:5 + 3 * n_norm]
    xo_ref = refs[5 + 3 * n_norm]
    ho_refs = refs[6 + 3 * n_norm:6 + 4 * n_norm]
    acc_s = refs[6 + 4 * n_norm]
    k = pl.program_id(1)

    h = h_ref[...]
    tf = win_ref.shape[1]
    pieces = []
    for c in range(tf // fc):
        a = jnp.dot(h, win_ref[:, c * fc:(c + 1) * fc], preferred_element_type=F32)
        pieces.append(jnp.square(jnp.maximum(a, 0.0)).astype(BF16))
    a = pieces[0] if len(pieces) == 1 else jnp.concatenate(pieces, axis=1)

    prev = jnp.where(k > 0, acc_s[...], 0.0)
    acc_s[...] = prev + jnp.dot(a, wout_ref[...], preferred_element_type=F32)

    @pl.when(k == pl.num_programs(1) - 1)
    def _():
        tm = x_ref.shape[0]

        def rows_body(r, carry):
            rows = pl.ds(pl.multiple_of(r * rc, rc), rc)
            xn = x_ref[rows, :] + g2_ref[0] * acc_s[rows, :]
            xo_ref[rows, :] = xn
            for j in range(n_norm):
                ng, sc, sh = norm_refs[3 * j:3 * j + 3]
                ho_refs[j][rows, :] = _rms_mod(xn, ng[...], sc[0], sh[0]).astype(BF16)
            return carry

        lax.fori_loop(0, tm // rc, rows_body, 0)


def _mlp(h2, x2, w_in, w_out, gate, norms, seq):
    n, d = x2.shape
    f = w_in.shape[1]
    tm = _tile(seq, 512)
    tf = _tile(f, 1024)
    fc = _tile(tf, 512)
    rc = _tile(tm, 64)
    tpb = seq // tm
    n_norm = len(norms)
    row = lambda i, k: (i, 0)
    per_batch = lambda i, k: (i // tpb, 0, 0)
    in_specs = [
        pl.BlockSpec((tm, d), row),
        pl.BlockSpec((d, tf), lambda i, k: (0, k)),
        pl.BlockSpec((tf, d), lambda i, k: (k, 0)),
        pl.BlockSpec((tm, d), row),
        pl.BlockSpec((1, 1, d), per_batch),
    ]
    args = [h2, w_in, w_out, x2, gate]
    for ng, sc, sh in norms:
        in_specs += [pl.BlockSpec((1, d), lambda i, k: (0, 0)),
                     pl.BlockSpec((1, 1, d), per_batch), pl.BlockSpec((1, 1, d), per_batch)]
        args += [ng.reshape(1, d), sc, sh]
    out_specs = [pl.BlockSpec((tm, d), row)] * (1 + n_norm)
    out_shape = [jax.ShapeDtypeStruct((n, d), F32)] + [jax.ShapeDtypeStruct((n, d), BF16)] * n_norm
    nbytes = (2 * tm * d * 2 + 4 * d * tf * 2 + 4 * tm * d * 4 + tm * d * 4
              + 2 * n_norm * tm * d * 2 + 2 * tm * tf * 4 + tm * d * 4)
    outs = pl.pallas_call(
        functools.partial(_mlp_kernel, n_norm=n_norm, rc=rc, fc=fc),
        grid=(n // tm, f // tf),
        in_specs=in_specs,
        out_specs=out_specs,
        out_shape=out_shape,
        scratch_shapes=[pltpu.VMEM((tm, d), F32)],
        compiler_params=_params(2, nbytes),
        name="relu2_mlp",
    )(*args)
    return outs[0], list(outs[1:])


def _head_norm(a, g):
    ms = jnp.mean(a * a, axis=-1, keepdims=True)
    return a * lax.rsqrt(ms + NORM_EPS) * g


def _q_kernel(h_ref, w_ref, g_ref, o_ref, *, hc, scale):
    tm = h_ref.shape[0]
    n_heads, dh = o_ref.shape[1], g_ref.shape[1]
    h = h_ref[...]
    row = lax.broadcasted_iota(jnp.int32, (dh, tm), 0)
    ones_rows = jnp.where(row < N_F_PARTS, 1.0, 0.0).astype(o_ref.dtype)
    gs = g_ref[...] * scale
    for c in range(n_heads // hc):
        acc = jnp.dot(h, w_ref[:, c * hc * dh:(c + 1) * hc * dh], preferred_element_type=F32)
        for hh in range(hc):
            head = c * hc + hh
            qn = _head_norm(acc[:, hh * dh:(hh + 1) * dh], gs)
            o_ref[0, head, 0:dh, :] = qn.T.astype(o_ref.dtype)
            o_ref[0, head, dh:2 * dh, :] = ones_rows


def _q_proj(h, w, g, n_batch, seq, n_heads):
    n, d = h.shape
    dh = d // n_heads
    tm = _tile(seq, 512)
    tpb = seq // tm
    hc = _tile(n_heads, 4)
    nbytes = 2 * tm * d * 2 + d * d * 2 + 2 * n_heads * tm * 2 * dh * 2 + 2 * tm * hc * dh * 4
    return pl.pallas_call(
        functools.partial(_q_kernel, hc=hc, scale=dh ** -0.5 * LOG2_E),
        grid=(n // tm,),
        in_specs=[pl.BlockSpec((tm, d), lambda i: (i, 0)), _resident((d, d)),
                  pl.BlockSpec((1, dh), lambda i: (0, 0))],
        out_specs=pl.BlockSpec((1, n_heads, 2 * dh, tm), lambda i: (i // tpb, 0, 0, i % tpb)),
        out_shape=jax.ShapeDtypeStruct((n_batch, n_heads, 2 * dh, seq), BF16),
        compiler_params=_params(1, nbytes),
        name="q_proj",
    )(h, w, g.reshape(1, dh))


def _kv_kernel(h_ref, w_ref, wf_ref, bf_ref, g_ref, ko_ref, vo_ref, carry_s, *, hc, tpb):
    tm, d = h_ref.shape
    n_heads, dh = ko_ref.shape[1], g_ref.shape[1]

    @pl.when(pl.program_id(0) % tpb == 0)
    def _():
        carry_s[...] = jnp.zeros(carry_s.shape, F32)

    h = h_ref[...]

    z = jnp.dot(h, wf_ref[...], preferred_element_type=F32) + bf_ref[...]
    log_f = _log_sigmoid(z)
    ri = lax.broadcasted_iota(jnp.int32, (tm, tm), 0)
    ci = lax.broadcasted_iota(jnp.int32, (tm, tm), 1)
    tri = jnp.where(ri >= ci, 1.0, 0.0).astype(F32)
    cum = jnp.dot(tri, log_f, preferred_element_type=F32, precision=lax.Precision.HIGHEST)
    cum = cum + carry_s[0:1, :]
    carry_s[...] = jnp.broadcast_to(cum[tm - 1:tm, :], carry_s.shape)

    parts = []
    rest = -LOG2_E * cum
    for _ in range(N_F_PARTS):
        piece = rest.astype(BF16).astype(F32)
        parts.append(piece)
        rest = rest - piece

    lane = lax.broadcasted_iota(jnp.int32, (tm, dh), 1)
    g = g_ref[...]
    for c in range(n_heads // hc):
        acc = jnp.dot(h, w_ref[:, c * hc * dh:(c + 1) * hc * dh], preferred_element_type=F32)
        for hh in range(hc):
            head = c * hc + hh
            kn = _head_norm(acc[:, hh * dh:(hh + 1) * dh], g)
            ko_ref[0, head, :, 0:dh] = kn.astype(ko_ref.dtype)
            aug = jnp.zeros((tm, dh), F32)
            for p in range(N_F_PARTS):
                col = jnp.broadcast_to(parts[p][:, head:head + 1], (tm, dh))
                aug = jnp.where(lane == p, col, aug)
            ko_ref[0, head, :, dh:2 * dh] = aug.astype(ko_ref.dtype)
    for c in range(n_heads // hc):
        acc = jnp.dot(h, w_ref[:, d + c * hc * dh:d + (c + 1) * hc * dh], preferred_element_type=F32)
        for hh in range(hc):
            head = c * hc + hh
            vo_ref[0, head, 0] = acc[:, hh * dh:(hh + 1) * dh].T.astype(vo_ref.dtype)


def _kv_proj(h, w, w_forget, b_forget, g, n_batch, seq, n_heads, tk):
    n, d = h.shape
    dh = d // n_heads
    tm = tk
    tpb = seq // tm
    hc = _tile(n_heads, 4)
    assert n_heads <= V7X_LANES
    wf = jnp.zeros((d, V7X_LANES), BF16).at[:, :n_heads].set(w_forget.astype(BF16))
    bf = jnp.zeros((1, V7X_LANES), F32).at[0, :n_heads].set(b_forget)
    nbytes = (2 * tm * d * 2 + 2 * d * d * 2 + 2 * n_heads * tm * 3 * dh * 2
              + 2 * tm * hc * dh * 4 + 3 * tm * tm * 4)
    return pl.pallas_call(
        functools.partial(_kv_kernel, hc=hc, tpb=tpb),
        grid=(n // tm,),
        in_specs=[pl.BlockSpec((tm, d), lambda i: (i, 0)), _resident((d, 2 * d)),
                  _resident((d, V7X_LANES)), pl.BlockSpec((1, V7X_LANES), lambda i: (0, 0)),
                  pl.BlockSpec((1, dh), lambda i: (0, 0))],
        out_specs=[
            pl.BlockSpec((1, n_heads, tm, 2 * dh), lambda i: (i // tpb, 0, i % tpb, 0)),
            pl.BlockSpec((1, n_heads, 1, dh, tm), lambda i: (i // tpb, 0, i % tpb, 0, 0)),
        ],
        out_shape=[jax.ShapeDtypeStruct((n_batch, n_heads, seq, 2 * dh), BF16),
                   jax.ShapeDtypeStruct((n_batch, n_heads, seq // tk, dh, tk), BF16)],
        scratch_shapes=[pltpu.VMEM((V7X_SUBLANES, V7X_LANES), F32)],
        compiler_params=_params(1, nbytes),
        name="kv_proj",
    )(h, w, wf, bf, g.reshape(1, dh))


def _attn_kernel(q_ref, k_ref, vt_ref, o_ref, s_s, p_s, acc_s, *, qc):
    tq = q_ref.shape[3]
    n_kv, dh, tk = vt_ref.shape[2:]
    assert tq == 2 * tk
    qi = pl.program_id(2)
    whole = [slice(0, tq)]
    groups = [slice(g * qc, (g + 1) * qc) for g in range(tq // qc)]
    key_minus_query = (lax.broadcasted_iota(jnp.int32, (tk, qc), 0)
                       - lax.broadcasted_iota(jnp.int32, (tk, qc), 1))

    def scores(j, cols):
        kblk = k_ref[0, 0, pl.ds(pl.multiple_of(j * tk, tk), tk), :]
        return jnp.dot(kblk, q_ref[0, 0, :, cols], preferred_element_type=F32)

    def softmax(s, state, j, cols, masked):
        m_old, l_old, _ = state
        if masked:
            s = jnp.where(key_minus_query <= qi * tq + cols.start - j * tk, s, NEG_INF)
        m_new = jnp.maximum(m_old, jnp.max(s, axis=0, keepdims=True))
        p = jnp.exp2(s - m_new)
        alpha = jnp.exp2(m_old - m_new)
        l_new = alpha * l_old + jnp.sum(p, axis=0, keepdims=True)
        return p.astype(p_s.dtype), (m_new, l_new, alpha)

    def add_values(p, alpha, j, cols):
        acc_s[:, cols] = alpha * acc_s[:, cols] + jnp.dot(vt_ref[0, 0, j], p, preferred_element_type=F32)

    def pair(u, carry, last):
        a = 2 * u
        new_carry = []
        for cols, state in zip(groups if last else whole, carry):
            mask_a = last and cols.start < tk
            skip_b = last and cols.stop <= tk
            alpha_prev = state[2]
            s_a = s_s[:, cols]
            if not skip_b:
                s_b = scores(a + 1, cols)
            p_a, state = softmax(s_a, state, a, cols, mask_a)
            add_values(p_s[:, cols], alpha_prev, jnp.maximum(a - 1, 0), cols)
            if not last:
                s_s[:, cols] = scores(a + 2, cols)
            alpha_a = state[2]
            if not skip_b:
                p_b, state = softmax(s_b, state, a + 1, cols, last)
            add_values(p_a, alpha_a, a, cols)
            if not last:
                p_s[:, cols] = p_b
            elif not skip_b:
                add_values(p_b, state[2], a + 1, cols)
            new_carry.append(state)
        return tuple(new_carry)

    def two_pairs(v, carry):
        return pair(2 * v + 1, pair(2 * v, carry, False), False)

    s_s[...] = scores(0, whole[0])
    p_s[...] = jnp.zeros(p_s.shape, p_s.dtype)
    acc_s[...] = jnp.zeros(acc_s.shape, F32)
    carry = ((jnp.full((1, tq), NEG_INF, F32), jnp.zeros((1, tq), F32), jnp.ones((1, tq), F32)),)
    carry = lax.fori_loop(0, qi // 2, two_pairs, carry)
    carry = lax.fori_loop(0, qi % 2, lambda _, cr: pair(qi - 1, cr, False), carry)
    carry = tuple(tuple(v[:, cols] for v in carry[0]) for cols in groups)
    carry = pair(qi, carry, True)

    for cols, state in zip(groups, carry):
        out = acc_s[:, cols] * (1.0 / state[1])
        o_ref[0, cols, :] = out.T.astype(o_ref.dtype)


def _attention(qp, kp, vt):
    n_batch, n_heads, dh2, seq = qp.shape
    n_kv, dh, tk = vt.shape[2:]
    tq = 2 * tk
    qc = _tile(tq, 512)
    nbytes = (2 * tq * dh2 * 2 + 2 * seq * dh2 * 2 + 2 * seq * dh * 2 + 2 * tq * dh * 2
              + dh * tq * 4 + tk * tq * 6 + 4 * tk * tq * 4)
    return pl.pallas_call(
        functools.partial(_attn_kernel, qc=qc),
        grid=(n_batch, n_heads, seq // tq),
        in_specs=[
            pl.BlockSpec((1, 1, dh2, tq), lambda b, h, i: (b, h, 0, i)),
            pl.BlockSpec((1, 1, seq, dh2), lambda b, h, i: (b, h, 0, 0)),
            pl.BlockSpec((1, 1, n_kv, dh, tk), lambda b, h, i: (b, h, 0, 0, 0)),
        ],
        out_specs=pl.BlockSpec((1, tq, dh), lambda b, h, i: (b, i, h)),
        out_shape=jax.ShapeDtypeStruct((n_batch, seq, n_heads * dh), BF16),
        scratch_shapes=[pltpu.VMEM((tk, tq), F32), pltpu.VMEM((tk, tq), BF16),
                        pltpu.VMEM((dh, tq), F32)],
        compiler_params=_params(3, nbytes),
        name="forgetting_attention",
    )(qp, kp, vt)


def kernel(x, c, mix_norm_g, mlp_norm_g, w_mod, b_mod, w_mlp_in, w_mlp_out, lru_w_in, lru_conv_w, lru_conv_b, lru_w_a, lru_b_a, lru_w_i, lru_b_i, lru_lambda, lru_w_out, kv_norm_g, kv_w_mod, kv_b_mod, w_kv, k_norm_g, w_forget, b_forget, attn_w_q, q_norm_g, attn_w_o):
    n_batch, seq, d = x.shape
    depth = w_mod.shape[0]
    n_a = lru_w_in.shape[0]
    n_heads = w_forget.shape[1]
    assert 1 <= n_a < depth and w_mod.shape[2] == N_MOD * d
    n = n_batch * seq
    tk = _tile(seq, 512)

    mod = _modulation(c, w_mod, b_mod)
    kv_mod = _modulation(c, kv_w_mod[None], kv_b_mod[None])[0]
    per_batch = lambda a: a.reshape(n_batch, 1, d)
    sh1, sc1, g1, sh2, sc2, g2 = [[per_batch(mod[l, :, j * d:(j + 1) * d]) for l in range(depth)]
                                  for j in range(N_MOD)]
    kv_shift, kv_scale = per_batch(kv_mod[:, :d]), per_batch(kv_mod[:, d:])

    bf = lambda w: w.astype(BF16)
    x2 = x.reshape(n, d)
    h = _norm_mod(x2, mix_norm_g[0], sc1[0], sh1[0], seq)
    h_kv = kp = vt = None
    for layer in range(depth):
        if layer < n_a:
            a = layer
            xg = _lru_in(h, bf(lru_w_in[a]))
            mix_in = _lru_scan(xg, lru_conv_w[a], lru_conv_b[a], bf(lru_w_a[a]), lru_b_a[a],
                               bf(lru_w_i[a]), lru_b_i[a], lru_lambda[a], n_batch, seq)
            w_o = lru_w_out[a]
        else:
            bl = layer - n_a
            if layer == n_a:
                kp, vt = _kv_proj(h_kv, bf(w_kv), w_forget, b_forget, k_norm_g, n_batch, seq, n_heads, tk)
            qp = _q_proj(h, bf(attn_w_q[bl]), q_norm_g[bl], n_batch, seq, n_heads)
            mix_in = _attention(qp, kp, vt).reshape(n, d)
            w_o = attn_w_o[bl]
        x2, h2 = _proj_res(mix_in, bf(w_o), x2, g1[layer], mlp_norm_g[layer], sc2[layer], sh2[layer], seq)
        norms = []
        if layer + 1 < depth:
            norms.append((mix_norm_g[layer + 1], sc1[layer + 1], sh1[layer + 1]))
        if layer + 1 == n_a:
            norms.append((kv_norm_g, kv_scale, kv_shift))
        x2, hs = _mlp(h2, x2, bf(w_mlp_in[layer]), bf(w_mlp_out[layer]), g2[layer], norms, seq)
        if hs:
            h = hs[0]
        if layer + 1 == n_a:
            h_kv = hs[1]
    return x2.reshape(n_batch, seq, d)
```

```python
import functools
import math

import jax
import jax.numpy as jnp
from jax import lax
from jax.experimental import pallas as pl
from jax.experimental.pallas import tpu as pltpu

F32 = jnp.float32
BF16 = jnp.bfloat16

NORM_EPS = 1e-6
LRU_C = 8.0
NEG_INF = -1e30
N_MOD = 6

V7X_LANES = 128
V7X_SUBLANES = 8
V7X_BF16_ROWS = 16
V7X_VMEM_BYTES = 64 * 1024 * 1024
V7X_VMEM_RESERVE = 6 * 1024 * 1024

N_F_PARTS = 3
LOG2_E = 1.4426950408889634
F32_UNDERFLOW_LOG2 = 150.0
QK_BOUND_MARGIN = 1.02


def _vmem_limit(nbytes):
    return int(min(V7X_VMEM_BYTES - V7X_VMEM_RESERVE, max(nbytes * 5 // 4 + (8 << 20), 32 << 20)))


def _params(n_grid, nbytes, flags=None):
    return pltpu.CompilerParams(dimension_semantics=("arbitrary",) * n_grid,
                                vmem_limit_bytes=_vmem_limit(nbytes), flags=flags)


def _tile(n, want):
    t = min(n, want)
    while n % t:
        t -= 1
    return t


def _resident(shape):
    nd = len(shape)
    return pl.BlockSpec(shape, lambda *_: (0,) * nd, pipeline_mode=pl.Buffered(1))


def _rms_mod(x, g, scale, shift):
    ms = jnp.mean(x * x, axis=-1, keepdims=True)
    return (x * lax.rsqrt(ms + NORM_EPS)) * (g * (1.0 + scale)) + shift


def _gelu_tanh(x):
    c = math.sqrt(2.0 / math.pi)
    return x * (0.5 * (1.0 + jnp.tanh(c * (x + 0.044715 * (x * x * x)))))


def _sigmoid(x):
    return 1.0 / (1.0 + jnp.exp(-x))


def _log_sigmoid(x):
    return jnp.minimum(x, 0.0) - jnp.log1p(jnp.exp(-jnp.abs(x)))


def _softplus(x):
    return jnp.maximum(x, 0.0) + jnp.log1p(jnp.exp(-jnp.abs(x)))


def _mod_kernel(cb_ref, w_ref, b_ref, o_ref):
    n_batch = cb_ref.shape[0]
    tn = w_ref.shape[2]
    for b in range(n_batch):
        cv = cb_ref[b]
        cs = cv * _sigmoid(cv)
        for ch in range(tn // V7X_LANES):
            cols = slice(ch * V7X_LANES, (ch + 1) * V7X_LANES)
            s = jnp.sum(w_ref[0, :, cols] * cs, axis=0, keepdims=True)
            o_ref[0, b:b + 1, cols] = s + b_ref[0, :, cols]


def _modulation(c, w, b):
    n_l, d, m = w.shape
    n_batch = c.shape[0]
    tn = _tile(m, 1024)
    cb = jnp.broadcast_to(c[:, :, None], (n_batch, d, V7X_LANES))
    nbytes = cb.size * 4 + 2 * d * tn * 4 + 4 * n_batch * tn * 4
    return pl.pallas_call(
        _mod_kernel,
        grid=(n_l, m // tn),
        in_specs=[
            pl.BlockSpec((n_batch, d, V7X_LANES), lambda l, j: (0, 0, 0)),
            pl.BlockSpec((1, d, tn), lambda l, j: (l, 0, j)),
            pl.BlockSpec((1, 1, tn), lambda l, j: (l, 0, j)),
        ],
        out_specs=pl.BlockSpec((1, n_batch, tn), lambda l, j: (l, 0, j)),
        out_shape=jax.ShapeDtypeStruct((n_l, n_batch, m), F32),
        compiler_params=_params(2, nbytes),
        name="modulation",
    )(cb, w, b.reshape(n_l, 1, m))


def _norm_kernel(x_ref, g_ref, sc_ref, sh_ref, o_ref):
    o_ref[...] = _rms_mod(x_ref[...], g_ref[...], sc_ref[0], sh_ref[0]).astype(o_ref.dtype)


def _norm_mod(x2, g, scale, shift, seq):
    n, d = x2.shape
    tm = _tile(seq, 512)
    tpb = seq // tm
    row = lambda i: (i, 0)
    per_batch = lambda i: (i // tpb, 0, 0)
    nbytes = 2 * tm * d * 4 + 2 * tm * d * 2
    return pl.pallas_call(
        _norm_kernel,
        grid=(n // tm,),
        in_specs=[
            pl.BlockSpec((tm, d), row),
            pl.BlockSpec((1, d), lambda i: (0, 0)),
            pl.BlockSpec((1, 1, d), per_batch),
            pl.BlockSpec((1, 1, d), per_batch),
        ],
        out_specs=pl.BlockSpec((tm, d), row),
        out_shape=jax.ShapeDtypeStruct((n, d), BF16),
        compiler_params=_params(1, nbytes),
        name="norm_mod",
    )(x2, g.reshape(1, d), scale, shift)


def _lru_in_kernel(h_ref, w_ref, o_ref, *, tc):
    h = h_ref[...]
    n_cols = w_ref.shape[1]
    for c in range(n_cols // tc):
        cols = slice(c * tc, (c + 1) * tc)
        acc = jnp.dot(h, w_ref[:, cols], preferred_element_type=F32)
        if c * tc >= n_cols // 2:
            acc = _gelu_tanh(acc)
        o_ref[:, cols] = acc


def _lru_in(h, w):
    n, d = h.shape
    m = w.shape[1]
    tm = _tile(n, 512)
    tc = _tile(m // 2, 512)
    nbytes = 2 * tm * d * 2 + d * m * 2 + 2 * tm * m * 4 + 2 * tm * tc * 4
    return pl.pallas_call(
        functools.partial(_lru_in_kernel, tc=tc),
        grid=(n // tm,),
        in_specs=[pl.BlockSpec((tm, d), lambda i: (i, 0)), _resident((d, m))],
        out_specs=pl.BlockSpec((tm, m), lambda i: (i, 0)),
        out_shape=jax.ShapeDtypeStruct((n, m), F32),
        compiler_params=_params(1, nbytes),
        name="lru_in_proj",
    )(h, w)


def _lru_scan_kernel(xb_ref, gy_ref, cw_ref, cb_ref, wa_ref, ba_ref, wi_ref, bi_ref, lam_ref,
                     o_ref, xpad_s, a_s, u_s, h_s, *, lane_chunk):
    ts, width = xb_ref.shape
    conv_width = cw_ref.shape[0]
    n_blocks, bd, _ = wa_ref.shape
    halo = V7X_SUBLANES
    assert conv_width - 1 <= halo

    @pl.when(pl.program_id(1) == 0)
    def _():
        xpad_s[0:halo, :] = jnp.zeros((halo, width), F32)
        h_s[...] = jnp.zeros(h_s.shape, F32)

    xpad_s[halo:halo + ts, :] = xb_ref[...]

    decay = -LRU_C * _softplus(-lam_ref[...])

    for nb in range(n_blocks):
        cols = slice(nb * bd, (nb + 1) * bd)
        xc = cb_ref[:, cols] + cw_ref[conv_width - 1:conv_width, cols] * xpad_s[halo:halo + ts, cols]
        for k in range(conv_width - 1):
            back = conv_width - 1 - k
            xc = xc + cw_ref[k:k + 1, cols] * xpad_s[halo - back:halo - back + ts, cols]
        xcb = xc.astype(BF16)
        r = _sigmoid(jnp.dot(xcb, wa_ref[nb], preferred_element_type=F32) + ba_ref[:, cols])
        gi = _sigmoid(jnp.dot(xcb, wi_ref[nb], preferred_element_type=F32) + bi_ref[:, cols])
        log_a = decay[:, cols] * r
        a = jnp.exp(log_a)
        a_s[:, cols] = a
        u_s[:, cols] = jnp.sqrt(-jnp.tanh(log_a) * (a * a + 1.0)) * (gi * xc)

    xpad_s[0:halo, :] = xpad_s[ts:ts + halo, :]

    rows16 = V7X_BF16_ROWS
    row_id = lax.broadcasted_iota(jnp.int32, (V7X_SUBLANES, lane_chunk), 0)

    def group(g, carry):
        r0 = pl.multiple_of(g * rows16, rows16)
        for lc in range(width // lane_chunk):
            cols = slice(lc * lane_chunk, (lc + 1) * lane_chunk)
            hprev = h_s[:, cols]
            halves = []
            for half in range(rows16 // V7X_SUBLANES):
                rr = pl.ds(r0 + half * V7X_SUBLANES, V7X_SUBLANES)
                a = a_s[rr, cols]
                u = u_s[rr, cols]
                for sh in (1, 2, 4):
                    a_sh = pltpu.roll(a, sh, axis=0)
                    u_sh = pltpu.roll(u, sh, axis=0)
                    live = row_id >= sh
                    u = jnp.where(live, a * u_sh + u, u)
                    a = jnp.where(live, a * a_sh, a)
                hcur = a * hprev + u
                hprev = jnp.broadcast_to(hcur[V7X_SUBLANES - 1:V7X_SUBLANES, :], hcur.shape)
                halves.append(hcur)
            h_s[:, cols] = hprev
            h16 = jnp.concatenate(halves, axis=0)
            o_ref[pl.ds(r0, rows16), cols] = (h16 * gy_ref[pl.ds(r0, rows16), cols]).astype(o_ref.dtype)
        return carry

    lax.fori_loop(0, ts // rows16, group, 0)


def _lru_scan(xg, conv_w, conv_b, w_a, b_a, w_i, b_i, lam, n_batch, seq):
    n, two_w = xg.shape
    width = two_w // 2
    ts = _tile(seq, 256)
    tpb = seq // ts
    n_blocks, bd, _ = w_a.shape
    lane_chunk = _tile(width, 512)
    row = lambda b, t: (b * tpb + t, 0)
    vec = lambda a: a.reshape(1, width)
    small = lambda shape: pl.BlockSpec(shape, lambda b, t: (0,) * len(shape))
    nbytes = (4 * ts * width * 4 + 2 * ts * width * 2 + (3 * ts + 16) * width * 4
              + 4 * n_blocks * bd * bd * 2)
    return pl.pallas_call(
        functools.partial(_lru_scan_kernel, lane_chunk=lane_chunk),
        grid=(n_batch, tpb),
        in_specs=[
            pl.BlockSpec((ts, width), row),
            pl.BlockSpec((ts, width), lambda b, t: (b * tpb + t, 1)),
            small(conv_w.shape), small((1, width)),
            small(w_a.shape), small((1, width)),
            small(w_i.shape), small((1, width)),
            small((1, width)),
        ],
        out_specs=pl.BlockSpec((ts, width), row),
        out_shape=jax.ShapeDtypeStruct((n, width), BF16),
        scratch_shapes=[
            pltpu.VMEM((ts + 2 * V7X_SUBLANES, width), F32),
            pltpu.VMEM((ts, width), F32),
            pltpu.VMEM((ts, width), F32),
            pltpu.VMEM((V7X_SUBLANES, width), F32),
        ],
        compiler_params=_params(2, nbytes),
        name="lru_scan",
    )(xg, xg, conv_w, vec(conv_b), w_a, vec(b_a), w_i, vec(b_i), vec(lam))


def _proj_res_kernel(a_ref, w_ref, x_ref, g1_ref, ng_ref, sc_ref, sh_ref, xo_ref, h_ref, *, rc):
    tm = a_ref.shape[0]
    for r in range(tm // rc):
        rows = slice(r * rc, (r + 1) * rc)
        acc = jnp.dot(a_ref[rows, :], w_ref[...], preferred_element_type=F32)
        xn = x_ref[rows, :] + g1_ref[0] * acc
        xo_ref[rows, :] = xn
        h_ref[rows, :] = _rms_mod(xn, ng_ref[...], sc_ref[0], sh_ref[0]).astype(h_ref.dtype)


def _proj_res(a, w, x2, gate, norm_g, scale, shift, seq):
    n, k = a.shape
    d = w.shape[1]
    tm = _tile(seq, 512)
    rc = _tile(tm, 256)
    tpb = seq // tm
    row = lambda i: (i, 0)
    per_batch = lambda i: (i // tpb, 0, 0)
    nbytes = 2 * tm * k * 2 + k * d * 2 + 4 * tm * d * 4 + 2 * tm * d * 2 + 3 * rc * d * 4
    return pl.pallas_call(
        functools.partial(_proj_res_kernel, rc=rc),
        grid=(n // tm,),
        in_specs=[
            pl.BlockSpec((tm, k), row), _resident((k, d)), pl.BlockSpec((tm, d), row),
            pl.BlockSpec((1, 1, d), per_batch), pl.BlockSpec((1, d), lambda i: (0, 0)),
            pl.BlockSpec((1, 1, d), per_batch), pl.BlockSpec((1, 1, d), per_batch),
        ],
        out_specs=[pl.BlockSpec((tm, d), row), pl.BlockSpec((tm, d), row)],
        out_shape=[jax.ShapeDtypeStruct((n, d), F32), jax.ShapeDtypeStruct((n, d), BF16)],
        compiler_params=_params(1, nbytes),
        name="mix_out_proj",
    )(a, w, x2, gate, norm_g.reshape(1, d), scale, shift)


def _mlp_kernel(*refs, n_norm, rc, fc):
    h_ref, win_ref, wout_ref, x_ref, g2_ref = refs[:5]
    norm_refs = refs[5:5 + 3 * n_norm]
    xo_ref = refs[5 + 3 * n_norm]
    ho_refs = refs[6 + 3 * n_norm:6 + 4 * n_norm]
    acc_s = refs[6 + 4 * n_norm]
    k = pl.program_id(1)

    h = h_ref[...]
    tf = win_ref.shape[1]
    pieces = []
    for c in range(tf // fc):
        a = jnp.dot(h, win_ref[:, c * fc:(c + 1) * fc], preferred_element_type=F32)
        pieces.append(jnp.square(jnp.maximum(a, 0.0)).astype(BF16))
    a = pieces[0] if len(pieces) == 1 else jnp.concatenate(pieces, axis=1)

    prev = jnp.where(k > 0, acc_s[...], 0.0)
    acc_s[...] = prev + jnp.dot(a, wout_ref[...], preferred_element_type=F32)

    @pl.when(k == pl.num_programs(1) - 1)
    def _():
        tm = x_ref.shape[0]

        def rows_body(r, carry):
            rows = pl.ds(pl.multiple_of(r * rc, rc), rc)
            xn = x_ref[rows, :] + g2_ref[0] * acc_s[rows, :]
            xo_ref[rows, :] = xn
            for j in range(n_norm):
                ng, sc, sh = norm_refs[3 * j:3 * j + 3]
                ho_refs[j][rows, :] = _rms_mod(xn, ng[...], sc[0], sh[0]).astype(BF16)
            return carry

        lax.fori_loop(0, tm // rc, rows_body, 0)


def _mlp(h2, x2, w_in, w_out, gate, norms, seq):
    n, d = x2.shape
    f = w_in.shape[1]
    tm = _tile(seq, 512)
    tf = _tile(f, 1024)
    fc = _tile(tf, 512)
    rc = _tile(tm, 64)
    tpb = seq // tm
    n_norm = len(norms)
    row = lambda i, k: (i, 0)
    per_batch = lambda i, k: (i // tpb, 0, 0)
    in_specs = [
        pl.BlockSpec((tm, d), row),
        pl.BlockSpec((d, tf), lambda i, k: (0, k)),
        pl.BlockSpec((tf, d), lambda i, k: (k, 0)),
        pl.BlockSpec((tm, d), row),
        pl.BlockSpec((1, 1, d), per_batch),
    ]
    args = [h2, w_in, w_out, x2, gate]
    for ng, sc, sh in norms:
        in_specs += [pl.BlockSpec((1, d), lambda i, k: (0, 0)),
                     pl.BlockSpec((1, 1, d), per_batch), pl.BlockSpec((1, 1, d), per_batch)]
        args += [ng.reshape(1, d), sc, sh]
    out_specs = [pl.BlockSpec((tm, d), row)] * (1 + n_norm)
    out_shape = [jax.ShapeDtypeStruct((n, d), F32)] + [jax.ShapeDtypeStruct((n, d), BF16)] * n_norm
    nbytes = (2 * tm * d * 2 + 4 * d * tf * 2 + 4 * tm * d * 4 + tm * d * 4
              + 2 * n_norm * tm * d * 2 + 2 * tm * tf * 4 + tm * d * 4)
    outs = pl.pallas_call(
        functools.partial(_mlp_kernel, n_norm=n_norm, rc=rc, fc=fc),
        grid=(n // tm, f // tf),
        in_specs=in_specs,
        out_specs=out_specs,
        out_shape=out_shape,
        scratch_shapes=[pltpu.VMEM((tm, d), F32)],
        compiler_params=_params(2, nbytes),
        name="relu2_mlp",
    )(*args)
    return outs[0], list(outs[1:])


def _head_norm(a, g):
    ms = jnp.mean(a * a, axis=-1, keepdims=True)
    return a * lax.rsqrt(ms + NORM_EPS) * g


def _q_kernel(h_ref, w_ref, g_ref, o_ref, *, hc, scale):
    tm = h_ref.shape[0]
    n_heads, dh = o_ref.shape[1], g_ref.shape[1]
    h = h_ref[...]
    row = lax.broadcasted_iota(jnp.int32, (dh, tm), 0)
    ones_rows = jnp.where(row < N_F_PARTS, 1.0, 0.0).astype(o_ref.dtype)
    gs = g_ref[...] * scale
    for c in range(n_heads // hc):
        acc = jnp.dot(h, w_ref[:, c * hc * dh:(c + 1) * hc * dh], preferred_element_type=F32)
        for hh in range(hc):
            head = c * hc + hh
            qn = _head_norm(acc[:, hh * dh:(hh + 1) * dh], gs)
            o_ref[0, head, 0:dh, :] = qn.T.astype(o_ref.dtype)
            o_ref[0, head, dh:2 * dh, :] = ones_rows


def _q_proj(h, w, g, n_batch, seq, n_heads):
    n, d = h.shape
    dh = d // n_heads
    tm = _tile(seq, 512)
    tpb = seq // tm
    hc = _tile(n_heads, 4)
    nbytes = 2 * tm * d * 2 + d * d * 2 + 2 * n_heads * tm * 2 * dh * 2 + 2 * tm * hc * dh * 4
    return pl.pallas_call(
        functools.partial(_q_kernel, hc=hc, scale=dh ** -0.5 * LOG2_E),
        grid=(n // tm,),
        in_specs=[pl.BlockSpec((tm, d), lambda i: (i, 0)), _resident((d, d)),
                  pl.BlockSpec((1, dh), lambda i: (0, 0))],
        out_specs=pl.BlockSpec((1, n_heads, 2 * dh, tm), lambda i: (i // tpb, 0, 0, i % tpb)),
        out_shape=jax.ShapeDtypeStruct((n_batch, n_heads, 2 * dh, seq), BF16),
        compiler_params=_params(1, nbytes),
        name="q_proj",
    )(h, w, g.reshape(1, dh))


def _kv_kernel(h_ref, w_ref, wf_ref, bf_ref, g_ref, ko_ref, vo_ref, fe_ref, carry_s, *, hc, tpb):
    tm, d = h_ref.shape
    n_heads, dh = ko_ref.shape[1], g_ref.shape[1]

    @pl.when(pl.program_id(0) % tpb == 0)
    def _():
        carry_s[...] = jnp.zeros(carry_s.shape, F32)

    h = h_ref[...]

    z = jnp.dot(h, wf_ref[...], preferred_element_type=F32) + bf_ref[...]
    log_f = _log_sigmoid(z)
    ri = lax.broadcasted_iota(jnp.int32, (tm, tm), 0)
    ci = lax.broadcasted_iota(jnp.int32, (tm, tm), 1)
    tri = jnp.where(ri >= ci, 1.0, 0.0).astype(F32)
    cum = jnp.dot(tri, log_f, preferred_element_type=F32, precision=lax.Precision.HIGHEST)
    cum = cum + carry_s[0:1, :]
    carry_s[...] = jnp.broadcast_to(cum[tm - 1:tm, :], carry_s.shape)
    fe_ref[0] = cum[tm - 1:tm, :]

    parts = []
    rest = -LOG2_E * cum
    for _ in range(N_F_PARTS):
        piece = rest.astype(BF16).astype(F32)
        parts.append(piece)
        rest = rest - piece

    lane = lax.broadcasted_iota(jnp.int32, (tm, dh), 1)
    g = g_ref[...]
    for c in range(n_heads // hc):
        acc = jnp.dot(h, w_ref[:, c * hc * dh:(c + 1) * hc * dh], preferred_element_type=F32)
        for hh in range(hc):
            head = c * hc + hh
            kn = _head_norm(acc[:, hh * dh:(hh + 1) * dh], g)
            ko_ref[0, head, :, 0:dh] = kn.astype(ko_ref.dtype)
            aug = jnp.zeros((tm, dh), F32)
            for p in range(N_F_PARTS):
                col = jnp.broadcast_to(parts[p][:, head:head + 1], (tm, dh))
                aug = jnp.where(lane == p, col, aug)
            ko_ref[0, head, :, dh:2 * dh] = aug.astype(ko_ref.dtype)
    for c in range(n_heads // hc):
        acc = jnp.dot(h, w_ref[:, d + c * hc * dh:d + (c + 1) * hc * dh], preferred_element_type=F32)
        for hh in range(hc):
            head = c * hc + hh
            vo_ref[0, head, 0] = acc[:, hh * dh:(hh + 1) * dh].T.astype(vo_ref.dtype)


def _kv_proj(h, w, w_forget, b_forget, g, n_batch, seq, n_heads, tk):
    n, d = h.shape
    dh = d // n_heads
    tm = tk
    tpb = seq // tm
    hc = _tile(n_heads, 4)
    assert n_heads <= V7X_LANES
    wf = jnp.zeros((d, V7X_LANES), BF16).at[:, :n_heads].set(w_forget.astype(BF16))
    bf = jnp.zeros((1, V7X_LANES), F32).at[0, :n_heads].set(b_forget)
    nbytes = (2 * tm * d * 2 + 2 * d * d * 2 + 2 * n_heads * tm * 3 * dh * 2
              + 2 * tm * hc * dh * 4 + 3 * tm * tm * 4)
    return pl.pallas_call(
        functools.partial(_kv_kernel, hc=hc, tpb=tpb),
        grid=(n // tm,),
        in_specs=[pl.BlockSpec((tm, d), lambda i: (i, 0)), _resident((d, 2 * d)),
                  _resident((d, V7X_LANES)), pl.BlockSpec((1, V7X_LANES), lambda i: (0, 0)),
                  pl.BlockSpec((1, dh), lambda i: (0, 0))],
        out_specs=[
            pl.BlockSpec((1, n_heads, tm, 2 * dh), lambda i: (i // tpb, 0, i % tpb, 0)),
            pl.BlockSpec((1, n_heads, 1, dh, tm), lambda i: (i // tpb, 0, i % tpb, 0, 0)),
            pl.BlockSpec((1, 1, V7X_LANES), lambda i: (i, 0, 0)),
        ],
        out_shape=[jax.ShapeDtypeStruct((n_batch, n_heads, seq, 2 * dh), BF16),
                   jax.ShapeDtypeStruct((n_batch, n_heads, seq // tk, dh, tk), BF16),
                   jax.ShapeDtypeStruct((n // tm, 1, V7X_LANES), F32)],
        scratch_shapes=[pltpu.VMEM((V7X_SUBLANES, V7X_LANES), F32)],
        compiler_params=_params(1, nbytes),
        name="kv_proj",
    )(h, w, wf, bf, g.reshape(1, dh))


def _first_key_pairs(f_end, q_gain, k_gain, dh):
    n_batch, n_kv, n_heads = f_end.shape
    nq = n_kv // 2
    qk_max = (QK_BOUND_MARGIN * LOG2_E * math.sqrt(dh)
              * jnp.max(jnp.abs(q_gain)) * jnp.max(jnp.abs(k_gain)))
    first_own = 2 * jnp.arange(nq) - 1
    ref = f_end[:, jnp.maximum(first_own, 0), :]
    gap = LOG2_E * (ref[:, :, None, :] - f_end[:, None, :, :]) + 2.0 * qk_max + 1.0
    below = jnp.arange(n_kv)[None, :] < first_own[:, None]
    needed = jnp.logical_not((gap < -F32_UNDERFLOW_LOG2) & below[None, :, :, None])
    first_block = jnp.argmax(needed, axis=2)
    return jnp.transpose(first_block // 2, (0, 2, 1)).reshape(-1).astype(jnp.int32)


def _attn_kernel(first_ref, q_ref, k_ref, vt_ref, o_ref, s_s, p_s, acc_s, *, qc):
    tq = q_ref.shape[3]
    n_kv, dh, tk = vt_ref.shape[2:]
    assert tq == 2 * tk
    qi = pl.program_id(2)
    u0 = first_ref[(pl.program_id(0) * pl.num_programs(1) + pl.program_id(1)) * pl.num_programs(2) + qi]
    whole = [slice(0, tq)]
    groups = [slice(g * qc, (g + 1) * qc) for g in range(tq // qc)]
    key_minus_query = (lax.broadcasted_iota(jnp.int32, (tk, qc), 0)
                       - lax.broadcasted_iota(jnp.int32, (tk, qc), 1))

    def scores(j, cols):
        kblk = k_ref[0, 0, pl.ds(pl.multiple_of(j * tk, tk), tk), :]
        return jnp.dot(kblk, q_ref[0, 0, :, cols], preferred_element_type=F32)

    def softmax(s, state, j, cols, masked):
        m_old, l_old, _ = state
        if masked:
            s = jnp.where(key_minus_query <= qi * tq + cols.start - j * tk, s, NEG_INF)
        m_new = jnp.maximum(m_old, jnp.max(s, axis=0, keepdims=True))
        p = jnp.exp2(s - m_new)
        alpha = jnp.exp2(m_old - m_new)
        l_new = alpha * l_old + jnp.sum(p, axis=0, keepdims=True)
        return p.astype(p_s.dtype), (m_new, l_new, alpha)

    def add_values(p, alpha, j, cols):
        acc_s[:, cols] = alpha * acc_s[:, cols] + jnp.dot(vt_ref[0, 0, j], p, preferred_element_type=F32)

    def pair(u, carry, last):
        a = 2 * u
        new_carry = []
        for cols, state in zip(groups if last else whole, carry):
            mask_a = last and cols.start < tk
            skip_b = last and cols.stop <= tk
            alpha_prev = state[2]
            s_a = s_s[:, cols]
            if not skip_b:
                s_b = scores(a + 1, cols)
            p_a, state = softmax(s_a, state, a, cols, mask_a)
            add_values(p_s[:, cols], alpha_prev, jnp.maximum(a - 1, 0), cols)
            if not last:
                s_s[:, cols] = scores(a + 2, cols)
            alpha_a = state[2]
            if not skip_b:
                p_b, state = softmax(s_b, state, a + 1, cols, last)
            add_values(p_a, alpha_a, a, cols)
            if not last:
                p_s[:, cols] = p_b
            elif not skip_b:
                add_values(p_b, state[2], a + 1, cols)
            new_carry.append(state)
        return tuple(new_carry)

    def two_pairs(v, carry):
        return pair(u0 + 2 * v + 1, pair(u0 + 2 * v, carry, False), False)

    n_unmasked = qi - u0
    s_s[...] = scores(2 * u0, whole[0])
    p_s[...] = jnp.zeros(p_s.shape, p_s.dtype)
    acc_s[...] = jnp.zeros(acc_s.shape, F32)
    carry = ((jnp.full((1, tq), NEG_INF, F32), jnp.zeros((1, tq), F32), jnp.ones((1, tq), F32)),)
    carry = lax.fori_loop(0, n_unmasked // 2, two_pairs, carry)
    carry = lax.fori_loop(0, n_unmasked % 2, lambda _, cr: pair(qi - 1, cr, False), carry)
    carry = tuple(tuple(v[:, cols] for v in carry[0]) for cols in groups)
    carry = pair(qi, carry, True)

    for cols, state in zip(groups, carry):
        out = acc_s[:, cols] * (1.0 / state[1])
        o_ref[0, cols, :] = out.T.astype(o_ref.dtype)


def _attention(first_pairs, qp, kp, vt):
    n_batch, n_heads, dh2, seq = qp.shape
    n_kv, dh, tk = vt.shape[2:]
    tq = 2 * tk
    qc = _tile(tq, 512)
    nbytes = (2 * tq * dh2 * 2 + 2 * seq * dh2 * 2 + 2 * seq * dh * 2 + 2 * tq * dh * 2
              + dh * tq * 4 + tk * tq * 6 + 4 * tk * tq * 4)
    return pl.pallas_call(
        functools.partial(_attn_kernel, qc=qc),
        grid_spec=pltpu.PrefetchScalarGridSpec(
            num_scalar_prefetch=1,
            grid=(n_batch, n_heads, seq // tq),
            in_specs=[
                pl.BlockSpec((1, 1, dh2, tq), lambda b, h, i, first: (b, h, 0, i)),
                pl.BlockSpec((1, 1, seq, dh2), lambda b, h, i, first: (b, h, 0, 0)),
                pl.BlockSpec((1, 1, n_kv, dh, tk), lambda b, h, i, first: (b, h, 0, 0, 0)),
            ],
            out_specs=pl.BlockSpec((1, tq, dh), lambda b, h, i, first: (b, i, h)),
            scratch_shapes=[pltpu.VMEM((tk, tq), F32), pltpu.VMEM((tk, tq), BF16),
                            pltpu.VMEM((dh, tq), F32)],
        ),
        out_shape=jax.ShapeDtypeStruct((n_batch, seq, n_heads * dh), BF16),
        compiler_params=_params(3, nbytes),
        name="forgetting_attention",
    )(first_pairs, qp, kp, vt)


def kernel(x, c, mix_norm_g, mlp_norm_g, w_mod, b_mod, w_mlp_in, w_mlp_out, lru_w_in, lru_conv_w, lru_conv_b, lru_w_a, lru_b_a, lru_w_i, lru_b_i, lru_lambda, lru_w_out, kv_norm_g, kv_w_mod, kv_b_mod, w_kv, k_norm_g, w_forget, b_forget, attn_w_q, q_norm_g, attn_w_o):
    n_batch, seq, d = x.shape
    depth = w_mod.shape[0]
    n_a = lru_w_in.shape[0]
    n_heads = w_forget.shape[1]
    assert 1 <= n_a < depth and w_mod.shape[2] == N_MOD * d
    n = n_batch * seq
    tk = _tile(seq, 512)

    mod = _modulation(c, w_mod, b_mod)
    kv_mod = _modulation(c, kv_w_mod[None], kv_b_mod[None])[0]
    per_batch = lambda a: a.reshape(n_batch, 1, d)
    sh1, sc1, g1, sh2, sc2, g2 = [[per_batch(mod[l, :, j * d:(j + 1) * d]) for l in range(depth)]
                                  for j in range(N_MOD)]
    kv_shift, kv_scale = per_batch(kv_mod[:, :d]), per_batch(kv_mod[:, d:])

    bf = lambda w: w.astype(BF16)
    x2 = x.reshape(n, d)
    h = _norm_mod(x2, mix_norm_g[0], sc1[0], sh1[0], seq)
    h_kv = kp = vt = f_end = None
    for layer in range(depth):
        if layer < n_a:
            a = layer
            xg = _lru_in(h, bf(lru_w_in[a]))
            mix_in = _lru_scan(xg, lru_conv_w[a], lru_conv_b[a], bf(lru_w_a[a]), lru_b_a[a],
                               bf(lru_w_i[a]), lru_b_i[a], lru_lambda[a], n_batch, seq)
            w_o = lru_w_out[a]
        else:
            bl = layer - n_a
            if layer == n_a:
                kp, vt, f_end = _kv_proj(h_kv, bf(w_kv), w_forget, b_forget, k_norm_g, n_batch, seq, n_heads, tk)
                f_end = f_end.reshape(n_batch, seq // tk, V7X_LANES)[:, :, :n_heads]
            qp = _q_proj(h, bf(attn_w_q[bl]), q_norm_g[bl], n_batch, seq, n_heads)
            first_pairs = _first_key_pairs(f_end, q_norm_g[bl], k_norm_g, d // n_heads)
            mix_in = _attention(first_pairs, qp, kp, vt).reshape(n, d)
            w_o = attn_w_o[bl]
        x2, h2 = _proj_res(mix_in, bf(w_o), x2, g1[layer], mlp_norm_g[layer], sc2[layer], sh2[layer], seq)
        norms = []
        if layer + 1 < depth:
            norms.append((mix_norm_g[layer + 1], sc1[layer + 1], sh1[layer + 1]))
        if layer + 1 == n_a:
            norms.append((kv_norm_g, kv_scale, kv_shift))
        x2, hs = _mlp(h2, x2, bf(w_mlp_in[layer]), bf(w_mlp_out[layer]), g2[layer], norms, seq)
        if hs:
            h = hs[0]
        if layer + 1 == n_a:
            h_kv = hs[1]
    return x2.reshape(n_batch, seq, d)
```

```python
import functools
import math

import jax
import jax.numpy as jnp
from jax import lax
from jax.experimental import pallas as pl
from jax.experimental.pallas import tpu as pltpu

F32 = jnp.float32
BF16 = jnp.bfloat16

NORM_EPS = 1e-6
LRU_C = 8.0
NEG_INF = -1e30
N_MOD = 6

V7X_LANES = 128
V7X_SUBLANES = 8
V7X_BF16_ROWS = 16
V7X_VMEM_BYTES = 64 * 1024 * 1024
V7X_VMEM_RESERVE = 6 * 1024 * 1024

N_F_PARTS = 3
LOG2_E = 1.4426950408889634
F32_UNDERFLOW_LOG2 = 150.0
QK_BOUND_MARGIN = 1.02


def _vmem_limit(nbytes):
    return int(min(V7X_VMEM_BYTES - V7X_VMEM_RESERVE, max(nbytes * 5 // 4 + (8 << 20), 32 << 20)))


def _params(n_grid, nbytes, flags=None):
    return pltpu.CompilerParams(dimension_semantics=("arbitrary",) * n_grid,
                                vmem_limit_bytes=_vmem_limit(nbytes), flags=flags)


def _tile(n, want):
    t = min(n, want)
    while n % t:
        t -= 1
    return t


def _resident(shape):
    nd = len(shape)
    return pl.BlockSpec(shape, lambda *_: (0,) * nd, pipeline_mode=pl.Buffered(1))


def _rms_mod(x, g, scale, shift):
    ms = jnp.mean(x * x, axis=-1, keepdims=True)
    return (x * lax.rsqrt(ms + NORM_EPS)) * (g * (1.0 + scale)) + shift


def _gelu_tanh(x):
    c = math.sqrt(2.0 / math.pi)
    return x * (0.5 * (1.0 + jnp.tanh(c * (x + 0.044715 * (x * x * x)))))


def _sigmoid(x):
    return 0.5 * jnp.tanh(0.5 * x) + 0.5


def _sqrt_nonneg(y):
    return jnp.where(y > 0.0, y * lax.rsqrt(y), 0.0)


def _log_sigmoid(x):
    return jnp.minimum(x, 0.0) - jnp.log1p(jnp.exp(-jnp.abs(x)))


def _softplus(x):
    return jnp.maximum(x, 0.0) + jnp.log1p(jnp.exp(-jnp.abs(x)))


def _mod_kernel(cb_ref, w_ref, b_ref, o_ref):
    n_batch = cb_ref.shape[0]
    tn = w_ref.shape[2]
    for b in range(n_batch):
        cv = cb_ref[b]
        cs = cv * _sigmoid(cv)
        for ch in range(tn // V7X_LANES):
            cols = slice(ch * V7X_LANES, (ch + 1) * V7X_LANES)
            s = jnp.sum(w_ref[0, :, cols] * cs, axis=0, keepdims=True)
            o_ref[0, b:b + 1, cols] = s + b_ref[0, :, cols]


def _modulation(c, w, b):
    n_l, d, m = w.shape
    n_batch = c.shape[0]
    tn = _tile(m, 1024)
    cb = jnp.broadcast_to(c[:, :, None], (n_batch, d, V7X_LANES))
    nbytes = cb.size * 4 + 2 * d * tn * 4 + 4 * n_batch * tn * 4
    return pl.pallas_call(
        _mod_kernel,
        grid=(n_l, m // tn),
        in_specs=[
            pl.BlockSpec((n_batch, d, V7X_LANES), lambda l, j: (0, 0, 0)),
            pl.BlockSpec((1, d, tn), lambda l, j: (l, 0, j)),
            pl.BlockSpec((1, 1, tn), lambda l, j: (l, 0, j)),
        ],
        out_specs=pl.BlockSpec((1, n_batch, tn), lambda l, j: (l, 0, j)),
        out_shape=jax.ShapeDtypeStruct((n_l, n_batch, m), F32),
        compiler_params=_params(2, nbytes),
        name="modulation",
    )(cb, w, b.reshape(n_l, 1, m))


def _norm_kernel(x_ref, g_ref, sc_ref, sh_ref, o_ref):
    o_ref[...] = _rms_mod(x_ref[...], g_ref[...], sc_ref[0], sh_ref[0]).astype(o_ref.dtype)


def _norm_mod(x2, g, scale, shift, seq):
    n, d = x2.shape
    tm = _tile(seq, 512)
    tpb = seq // tm
    row = lambda i: (i, 0)
    per_batch = lambda i: (i // tpb, 0, 0)
    nbytes = 2 * tm * d * 4 + 2 * tm * d * 2
    return pl.pallas_call(
        _norm_kernel,
        grid=(n // tm,),
        in_specs=[
            pl.BlockSpec((tm, d), row),
            pl.BlockSpec((1, d), lambda i: (0, 0)),
            pl.BlockSpec((1, 1, d), per_batch),
            pl.BlockSpec((1, 1, d), per_batch),
        ],
        out_specs=pl.BlockSpec((tm, d), row),
        out_shape=jax.ShapeDtypeStruct((n, d), BF16),
        compiler_params=_params(1, nbytes),
        name="norm_mod",
    )(x2, g.reshape(1, d), scale, shift)


def _lru_in_kernel(h_ref, w_ref, o_ref, *, tc):
    h = h_ref[...]
    n_cols = w_ref.shape[1]
    for c in range(n_cols // tc):
        cols = slice(c * tc, (c + 1) * tc)
        acc = jnp.dot(h, w_ref[:, cols], preferred_element_type=F32)
        if c * tc >= n_cols // 2:
            acc = _gelu_tanh(acc)
        o_ref[:, cols] = acc


def _lru_in(h, w):
    n, d = h.shape
    m = w.shape[1]
    tm = _tile(n, 512)
    tc = _tile(m // 2, 512)
    nbytes = 2 * tm * d * 2 + d * m * 2 + 2 * tm * m * 4 + 2 * tm * tc * 4
    return pl.pallas_call(
        functools.partial(_lru_in_kernel, tc=tc),
        grid=(n // tm,),
        in_specs=[pl.BlockSpec((tm, d), lambda i: (i, 0)), _resident((d, m))],
        out_specs=pl.BlockSpec((tm, m), lambda i: (i, 0)),
        out_shape=jax.ShapeDtypeStruct((n, m), F32),
        compiler_params=_params(1, nbytes),
        name="lru_in_proj",
    )(h, w)


def _lru_scan_kernel(xb_ref, gy_ref, cw_ref, cb_ref, wa_ref, ba_ref, wi_ref, bi_ref, lam_ref,
                     o_ref, xpad_s, a_s, u_s, h_s, *, lane_chunk):
    ts, width = xb_ref.shape
    conv_width = cw_ref.shape[0]
    n_blocks, bd, _ = wa_ref.shape
    halo = V7X_SUBLANES
    assert conv_width - 1 <= halo

    @pl.when(pl.program_id(1) == 0)
    def _():
        xpad_s[0:halo, :] = jnp.zeros((halo, width), F32)
        h_s[...] = jnp.zeros(h_s.shape, F32)

    xpad_s[halo:halo + ts, :] = xb_ref[...]

    decay = -LRU_C * _softplus(-lam_ref[...])

    for nb in range(n_blocks):
        cols = slice(nb * bd, (nb + 1) * bd)
        xc = cb_ref[:, cols] + cw_ref[conv_width - 1:conv_width, cols] * xpad_s[halo:halo + ts, cols]
        for k in range(conv_width - 1):
            back = conv_width - 1 - k
            xc = xc + cw_ref[k:k + 1, cols] * xpad_s[halo - back:halo - back + ts, cols]
        xcb = xc.astype(BF16)
        r = _sigmoid(jnp.dot(xcb, wa_ref[nb], preferred_element_type=F32) + ba_ref[:, cols])
        gi = _sigmoid(jnp.dot(xcb, wi_ref[nb], preferred_element_type=F32) + bi_ref[:, cols])
        log_a = decay[:, cols] * r
        a = jnp.exp(log_a)
        a_s[:, cols] = a
        u_s[:, cols] = _sqrt_nonneg(-jnp.tanh(log_a) * (a * a + 1.0)) * (gi * xc)

    xpad_s[0:halo, :] = xpad_s[ts:ts + halo, :]

    rows16 = V7X_BF16_ROWS
    row_id = lax.broadcasted_iota(jnp.int32, (V7X_SUBLANES, lane_chunk), 0)

    def group(g, carry):
        r0 = pl.multiple_of(g * rows16, rows16)
        for lc in range(width // lane_chunk):
            cols = slice(lc * lane_chunk, (lc + 1) * lane_chunk)
            hprev = h_s[:, cols]
            halves = []
            for half in range(rows16 // V7X_SUBLANES):
                rr = pl.ds(r0 + half * V7X_SUBLANES, V7X_SUBLANES)
                a = a_s[rr, cols]
                u = u_s[rr, cols]
                for sh in (1, 2, 4):
                    a_sh = pltpu.roll(a, sh, axis=0)
                    u_sh = pltpu.roll(u, sh, axis=0)
                    live = row_id >= sh
                    u = jnp.where(live, a * u_sh + u, u)
                    a = jnp.where(live, a * a_sh, a)
                hcur = a * hprev + u
                hprev = jnp.broadcast_to(hcur[V7X_SUBLANES - 1:V7X_SUBLANES, :], hcur.shape)
                halves.append(hcur)
            h_s[:, cols] = hprev
            h16 = jnp.concatenate(halves, axis=0)
            o_ref[pl.ds(r0, rows16), cols] = (h16 * gy_ref[pl.ds(r0, rows16), cols]).astype(o_ref.dtype)
        return carry

    lax.fori_loop(0, ts // rows16, group, 0)


def _lru_scan(xg, conv_w, conv_b, w_a, b_a, w_i, b_i, lam, n_batch, seq):
    n, two_w = xg.shape
    width = two_w // 2
    ts = _tile(seq, 256)
    tpb = seq // ts
    n_blocks, bd, _ = w_a.shape
    lane_chunk = _tile(width, 512)
    row = lambda b, t: (b * tpb + t, 0)
    vec = lambda a: a.reshape(1, width)
    small = lambda shape: pl.BlockSpec(shape, lambda b, t: (0,) * len(shape))
    nbytes = (4 * ts * width * 4 + 2 * ts * width * 2 + (3 * ts + 16) * width * 4
              + 4 * n_blocks * bd * bd * 2)
    return pl.pallas_call(
        functools.partial(_lru_scan_kernel, lane_chunk=lane_chunk),
        grid=(n_batch, tpb),
        in_specs=[
            pl.BlockSpec((ts, width), row),
            pl.BlockSpec((ts, width), lambda b, t: (b * tpb + t, 1)),
            small(conv_w.shape), small((1, width)),
            small(w_a.shape), small((1, width)),
            small(w_i.shape), small((1, width)),
            small((1, width)),
        ],
        out_specs=pl.BlockSpec((ts, width), row),
        out_shape=jax.ShapeDtypeStruct((n, width), BF16),
        scratch_shapes=[
            pltpu.VMEM((ts + 2 * V7X_SUBLANES, width), F32),
            pltpu.VMEM((ts, width), F32),
            pltpu.VMEM((ts, width), F32),
            pltpu.VMEM((V7X_SUBLANES, width), F32),
        ],
        compiler_params=_params(2, nbytes),
        name="lru_scan",
    )(xg, xg, conv_w, vec(conv_b), w_a, vec(b_a), w_i, vec(b_i), vec(lam))


def _proj_res_kernel(a_ref, w_ref, x_ref, g1_ref, ng_ref, sc_ref, sh_ref, xo_ref, h_ref, *, rc):
    tm = a_ref.shape[0]
    for r in range(tm // rc):
        rows = slice(r * rc, (r + 1) * rc)
        acc = jnp.dot(a_ref[rows, :], w_ref[...], preferred_element_type=F32)
        xn = x_ref[rows, :] + g1_ref[0] * acc
        xo_ref[rows, :] = xn
        h_ref[rows, :] = _rms_mod(xn, ng_ref[...], sc_ref[0], sh_ref[0]).astype(h_ref.dtype)


def _proj_res(a, w, x2, gate, norm_g, scale, shift, seq):
    n, k = a.shape
    d = w.shape[1]
    tm = _tile(seq, 512)
    rc = _tile(tm, 256)
    tpb = seq // tm
    row = lambda i: (i, 0)
    per_batch = lambda i: (i // tpb, 0, 0)
    nbytes = 2 * tm * k * 2 + k * d * 2 + 4 * tm * d * 4 + 2 * tm * d * 2 + 3 * rc * d * 4
    return pl.pallas_call(
        functools.partial(_proj_res_kernel, rc=rc),
        grid=(n // tm,),
        in_specs=[
            pl.BlockSpec((tm, k), row), _resident((k, d)), pl.BlockSpec((tm, d), row),
            pl.BlockSpec((1, 1, d), per_batch), pl.BlockSpec((1, d), lambda i: (0, 0)),
            pl.BlockSpec((1, 1, d), per_batch), pl.BlockSpec((1, 1, d), per_batch),
        ],
        out_specs=[pl.BlockSpec((tm, d), row), pl.BlockSpec((tm, d), row)],
        out_shape=[jax.ShapeDtypeStruct((n, d), F32), jax.ShapeDtypeStruct((n, d), BF16)],
        compiler_params=_params(1, nbytes),
        name="mix_out_proj",
    )(a, w, x2, gate, norm_g.reshape(1, d), scale, shift)


def _mlp_kernel(*refs, n_norm, rc, fc):
    h_ref, win_ref, wout_ref, x_ref, g2_ref = refs[:5]
    norm_refs = refs[5:5 + 3 * n_norm]
    xo_ref = refs[5 + 3 * n_norm]
    ho_refs = refs[6 + 3 * n_norm:6 + 4 * n_norm]
    acc_s = refs[6 + 4 * n_norm]
    k = pl.program_id(1)

    h = h_ref[...]
    tf = win_ref.shape[1]
    pieces = []
    for c in range(tf // fc):
        a = jnp.dot(h, win_ref[:, c * fc:(c + 1) * fc], preferred_element_type=F32)
        pieces.append(jnp.square(jnp.maximum(a, 0.0)).astype(BF16))
    a = pieces[0] if len(pieces) == 1 else jnp.concatenate(pieces, axis=1)

    prev = jnp.where(k > 0, acc_s[...], 0.0)
    acc_s[...] = prev + jnp.dot(a, wout_ref[...], preferred_element_type=F32)

    @pl.when(k == pl.num_programs(1) - 1)
    def _():
        tm = x_ref.shape[0]
        for r in range(tm // rc):
            rows = slice(r * rc, (r + 1) * rc)
            xn = x_ref[rows, :] + g2_ref[0] * acc_s[rows, :]
            xo_ref[rows, :] = xn
            for j in range(n_norm):
                ng, sc, sh = norm_refs[3 * j:3 * j + 3]
                ho_refs[j][rows, :] = _rms_mod(xn, ng[...], sc[0], sh[0]).astype(BF16)


def _mlp(h2, x2, w_in, w_out, gate, norms, seq):
    n, d = x2.shape
    f = w_in.shape[1]
    tm = _tile(seq, 512)
    tf = _tile(f, 1024)
    fc = _tile(tf, 512)
    rc = _tile(tm, 128)
    tpb = seq // tm
    n_norm = len(norms)
    row = lambda i, k: (i, 0)
    per_batch = lambda i, k: (i // tpb, 0, 0)
    in_specs = [
        pl.BlockSpec((tm, d), row),
        pl.BlockSpec((d, tf), lambda i, k: (0, k)),
        pl.BlockSpec((tf, d), lambda i, k: (k, 0)),
        pl.BlockSpec((tm, d), row),
        pl.BlockSpec((1, 1, d), per_batch),
    ]
    args = [h2, w_in, w_out, x2, gate]
    for ng, sc, sh in norms:
        in_specs += [pl.BlockSpec((1, d), lambda i, k: (0, 0)),
                     pl.BlockSpec((1, 1, d), per_batch), pl.BlockSpec((1, 1, d), per_batch)]
        args += [ng.reshape(1, d), sc, sh]
    out_specs = [pl.BlockSpec((tm, d), row)] * (1 + n_norm)
    out_shape = [jax.ShapeDtypeStruct((n, d), F32)] + [jax.ShapeDtypeStruct((n, d), BF16)] * n_norm
    nbytes = (2 * tm * d * 2 + 4 * d * tf * 2 + 4 * tm * d * 4 + tm * d * 4
              + 2 * n_norm * tm * d * 2 + 2 * tm * tf * 4 + tm * d * 4)
    outs = pl.pallas_call(
        functools.partial(_mlp_kernel, n_norm=n_norm, rc=rc, fc=fc),
        grid=(n // tm, f // tf),
        in_specs=in_specs,
        out_specs=out_specs,
        out_shape=out_shape,
        scratch_shapes=[pltpu.VMEM((tm, d), F32)],
        compiler_params=_params(2, nbytes),
        name="relu2_mlp",
    )(*args)
    return outs[0], list(outs[1:])


def _head_norm(a, g):
    ms = jnp.mean(a * a, axis=-1, keepdims=True)
    return a * lax.rsqrt(ms + NORM_EPS) * g


def _q_kernel(h_ref, w_ref, g_ref, o_ref, *, hc, scale):
    tm = h_ref.shape[0]
    n_heads, dh = o_ref.shape[1], g_ref.shape[1]
    h = h_ref[...]
    row = lax.broadcasted_iota(jnp.int32, (dh, tm), 0)
    ones_rows = jnp.where(row < N_F_PARTS, 1.0, 0.0).astype(o_ref.dtype)
    gs = g_ref[...] * scale
    for c in range(n_heads // hc):
        acc = jnp.dot(h, w_ref[:, c * hc * dh:(c + 1) * hc * dh], preferred_element_type=F32)
        for hh in range(hc):
            head = c * hc + hh
            qn = _head_norm(acc[:, hh * dh:(hh + 1) * dh], gs)
            o_ref[0, head, 0:dh, :] = qn.T.astype(o_ref.dtype)
            o_ref[0, head, dh:2 * dh, :] = ones_rows


def _q_proj(h, w, g, n_batch, seq, n_heads):
    n, d = h.shape
    dh = d // n_heads
    tm = _tile(seq, 512)
    tpb = seq // tm
    hc = _tile(n_heads, 4)
    nbytes = 2 * tm * d * 2 + d * d * 2 + 2 * n_heads * tm * 2 * dh * 2 + 2 * tm * hc * dh * 4
    return pl.pallas_call(
        functools.partial(_q_kernel, hc=hc, scale=dh ** -0.5 * LOG2_E),
        grid=(n // tm,),
        in_specs=[pl.BlockSpec((tm, d), lambda i: (i, 0)), _resident((d, d)),
                  pl.BlockSpec((1, dh), lambda i: (0, 0))],
        out_specs=pl.BlockSpec((1, n_heads, 2 * dh, tm), lambda i: (i // tpb, 0, 0, i % tpb)),
        out_shape=jax.ShapeDtypeStruct((n_batch, n_heads, 2 * dh, seq), BF16),
        compiler_params=_params(1, nbytes),
        name="q_proj",
    )(h, w, g.reshape(1, dh))


def _kv_kernel(h_ref, w_ref, wf_ref, bf_ref, g_ref, ko_ref, vo_ref, fe_ref, carry_s, *, hc, tpb):
    tm, d = h_ref.shape
    n_heads, dh = ko_ref.shape[1], g_ref.shape[1]

    @pl.when(pl.program_id(0) % tpb == 0)
    def _():
        carry_s[...] = jnp.zeros(carry_s.shape, F32)

    h = h_ref[...]

    z = jnp.dot(h, wf_ref[...], preferred_element_type=F32) + bf_ref[...]
    log_f = _log_sigmoid(z)
    ri = lax.broadcasted_iota(jnp.int32, (tm, tm), 0)
    ci = lax.broadcasted_iota(jnp.int32, (tm, tm), 1)
    tri = jnp.where(ri >= ci, 1.0, 0.0).astype(F32)
    cum = jnp.dot(tri, log_f, preferred_element_type=F32, precision=lax.Precision.HIGHEST)
    cum = cum + carry_s[0:1, :]
    carry_s[...] = jnp.broadcast_to(cum[tm - 1:tm, :], carry_s.shape)
    fe_ref[0] = cum[tm - 1:tm, :]

    parts = []
    rest = -LOG2_E * cum
    for _ in range(N_F_PARTS):
        piece = rest.astype(BF16).astype(F32)
        parts.append(piece)
        rest = rest - piece

    lane = lax.broadcasted_iota(jnp.int32, (tm, dh), 1)
    g = g_ref[...]
    for c in range(n_heads // hc):
        acc = jnp.dot(h, w_ref[:, c * hc * dh:(c + 1) * hc * dh], preferred_element_type=F32)
        for hh in range(hc):
            head = c * hc + hh
            kn = _head_norm(acc[:, hh * dh:(hh + 1) * dh], g)
            ko_ref[0, head, :, 0:dh] = kn.astype(ko_ref.dtype)
            aug = jnp.zeros((tm, dh), F32)
            for p in range(N_F_PARTS):
                col = jnp.broadcast_to(parts[p][:, head:head + 1], (tm, dh))
                aug = jnp.where(lane == p, col, aug)
            ko_ref[0, head, :, dh:2 * dh] = aug.astype(ko_ref.dtype)
    for c in range(n_heads // hc):
        acc = jnp.dot(h, w_ref[:, d + c * hc * dh:d + (c + 1) * hc * dh], preferred_element_type=F32)
        for hh in range(hc):
            head = c * hc + hh
            vo_ref[0, head, 0] = acc[:, hh * dh:(hh + 1) * dh].T.astype(vo_ref.dtype)


def _kv_proj(h, w, w_forget, b_forget, g, n_batch, seq, n_heads, tk):
    n, d = h.shape
    dh = d // n_heads
    tm = tk
    tpb = seq // tm
    hc = _tile(n_heads, 4)
    assert n_heads <= V7X_LANES
    wf = jnp.zeros((d, V7X_LANES), BF16).at[:, :n_heads].set(w_forget.astype(BF16))
    bf = jnp.zeros((1, V7X_LANES), F32).at[0, :n_heads].set(b_forget)
    nbytes = (2 * tm * d * 2 + 2 * d * d * 2 + 2 * n_heads * tm * 3 * dh * 2
              + 2 * tm * hc * dh * 4 + 3 * tm * tm * 4)
    return pl.pallas_call(
        functools.partial(_kv_kernel, hc=hc, tpb=tpb),
        grid=(n // tm,),
        in_specs=[pl.BlockSpec((tm, d), lambda i: (i, 0)), _resident((d, 2 * d)),
                  _resident((d, V7X_LANES)), pl.BlockSpec((1, V7X_LANES), lambda i: (0, 0)),
                  pl.BlockSpec((1, dh), lambda i: (0, 0))],
        out_specs=[
            pl.BlockSpec((1, n_heads, tm, 2 * dh), lambda i: (i // tpb, 0, i % tpb, 0)),
            pl.BlockSpec((1, n_heads, 1, dh, tm), lambda i: (i // tpb, 0, i % tpb, 0, 0)),
            pl.BlockSpec((1, 1, V7X_LANES), lambda i: (i, 0, 0)),
        ],
        out_shape=[jax.ShapeDtypeStruct((n_batch, n_heads, seq, 2 * dh), BF16),
                   jax.ShapeDtypeStruct((n_batch, n_heads, seq // tk, dh, tk), BF16),
                   jax.ShapeDtypeStruct((n // tm, 1, V7X_LANES), F32)],
        scratch_shapes=[pltpu.VMEM((V7X_SUBLANES, V7X_LANES), F32)],
        compiler_params=_params(1, nbytes),
        name="kv_proj",
    )(h, w, wf, bf, g.reshape(1, dh))


def _first_key_pairs(f_end, q_gain, k_gain, dh):
    n_batch, n_kv, n_heads = f_end.shape
    nq = n_kv // 2
    qk_max = (QK_BOUND_MARGIN * LOG2_E * math.sqrt(dh)
              * jnp.max(jnp.abs(q_gain)) * jnp.max(jnp.abs(k_gain)))
    first_own = 2 * jnp.arange(nq) - 1
    ref = f_end[:, jnp.maximum(first_own, 0), :]
    gap = LOG2_E * (ref[:, :, None, :] - f_end[:, None, :, :]) + 2.0 * qk_max + 1.0
    below = jnp.arange(n_kv)[None, :] < first_own[:, None]
    needed = jnp.logical_not((gap < -F32_UNDERFLOW_LOG2) & below[None, :, :, None])
    first_block = jnp.argmax(needed, axis=2)
    return jnp.transpose(first_block // 2, (0, 2, 1)).reshape(-1).astype(jnp.int32)


def _attn_kernel(first_ref, q_ref, qn_ref, k_ref, vt_ref, o_ref, s_s, p_s, acc_s, *, qc):
    tq = q_ref.shape[3]
    n_kv, dh, tk = vt_ref.shape[2:]
    assert tq == 2 * tk
    qi = pl.program_id(2)
    n_q = pl.num_programs(2)
    tile = (pl.program_id(0) * pl.num_programs(1) + pl.program_id(1)) * n_q + qi
    u0 = first_ref[tile]
    whole = [slice(0, tq)]
    groups = [slice(g * qc, (g + 1) * qc) for g in range(tq // qc)]
    key_minus_query = (lax.broadcasted_iota(jnp.int32, (tk, qc), 0)
                       - lax.broadcasted_iota(jnp.int32, (tk, qc), 1))

    def scores(j, cols, query_ref=q_ref):
        kblk = k_ref[0, 0, pl.ds(pl.multiple_of(j * tk, tk), tk), :]
        return jnp.dot(kblk, query_ref[0, 0, :, cols], preferred_element_type=F32)

    def softmax(s, state, j, cols, masked):
        m_old, l_old, _ = state
        if masked:
            s = jnp.where(key_minus_query <= qi * tq + cols.start - j * tk, s, NEG_INF)
        m_new = jnp.maximum(m_old, jnp.max(s, axis=0, keepdims=True))
        p = jnp.exp2(s - m_new)
        alpha = jnp.exp2(m_old - m_new)
        l_new = alpha * l_old + jnp.sum(p, axis=0, keepdims=True)
        return p.astype(p_s.dtype), (m_new, l_new, alpha)

    def add_values(p, alpha, j, cols):
        acc_s[:, cols] = alpha * acc_s[:, cols] + jnp.dot(vt_ref[0, 0, j], p, preferred_element_type=F32)

    def pair(u, carry, last):
        a = 2 * u
        new_carry = []
        for cols, state in zip(groups if last else whole, carry):
            mask_a = last and cols.start < tk
            skip_b = last and cols.stop <= tk
            alpha_prev = state[2]
            s_a = s_s[:, cols]
            if not skip_b:
                s_b = scores(a + 1, cols)
            p_a, state = softmax(s_a, state, a, cols, mask_a)
            add_values(p_s[:, cols], alpha_prev, jnp.maximum(a - 1, 0), cols)
            if not last:
                s_s[:, cols] = scores(a + 2, cols)
            alpha_a = state[2]
            if not skip_b:
                p_b, state = softmax(s_b, state, a + 1, cols, last)
            add_values(p_a, alpha_a, a, cols)
            if not last:
                p_s[:, cols] = p_b
            elif not skip_b:
                add_values(p_b, state[2], a + 1, cols)
            new_carry.append(state)
        return tuple(new_carry)

    def two_pairs(v, carry):
        return pair(u0 + 2 * v + 1, pair(u0 + 2 * v, carry, False), False)

    n_unmasked = qi - u0

    @pl.when(qi == 0)
    def _():
        s_s[...] = scores(2 * u0, whole[0])

    p_s[...] = jnp.zeros(p_s.shape, p_s.dtype)
    acc_s[...] = jnp.zeros(acc_s.shape, F32)
    carry = ((jnp.full((1, tq), NEG_INF, F32), jnp.zeros((1, tq), F32), jnp.ones((1, tq), F32)),)
    carry = lax.fori_loop(0, n_unmasked // 2, two_pairs, carry)
    carry = lax.fori_loop(0, n_unmasked % 2, lambda _, cr: pair(qi - 1, cr, False), carry)
    carry = tuple(tuple(v[:, cols] for v in carry[0]) for cols in groups)
    carry = pair(qi, carry, True)

    next_tile = jnp.minimum(tile + 1, pl.num_programs(0) * pl.num_programs(1) * n_q - 1)
    s_s[...] = scores(2 * first_ref[next_tile], whole[0], qn_ref)

    for cols, state in zip(groups, carry):
        out = acc_s[:, cols] * (1.0 / state[1])
        o_ref[0, cols, :] = out.T.astype(o_ref.dtype)


def _attention(first_pairs, qp, kp, vt):
    n_batch, n_heads, dh2, seq = qp.shape
    n_kv, dh, tk = vt.shape[2:]
    tq = 2 * tk
    qc = _tile(tq, 512)
    nbytes = (2 * tq * dh2 * 2 + 2 * seq * dh2 * 2 + 2 * seq * dh * 2 + 2 * tq * dh * 2
              + dh * tq * 4 + tk * tq * 6 + 4 * tk * tq * 4)
    return pl.pallas_call(
        functools.partial(_attn_kernel, qc=qc),
        grid_spec=pltpu.PrefetchScalarGridSpec(
            num_scalar_prefetch=1,
            grid=(n_batch, n_heads, seq // tq),
            in_specs=[
                pl.BlockSpec((1, 1, dh2, tq), lambda b, h, i, first: (b, h, 0, i)),
                pl.BlockSpec((1, 1, dh2, tq),
                             lambda b, h, i, first: (b, h, 0, jnp.minimum(i + 1, seq // tq - 1))),
                pl.BlockSpec((1, 1, seq, dh2), lambda b, h, i, first: (b, h, 0, 0)),
                pl.BlockSpec((1, 1, n_kv, dh, tk), lambda b, h, i, first: (b, h, 0, 0, 0)),
            ],
            out_specs=pl.BlockSpec((1, tq, dh), lambda b, h, i, first: (b, i, h)),
            scratch_shapes=[pltpu.VMEM((tk, tq), F32), pltpu.VMEM((tk, tq), BF16),
                            pltpu.VMEM((dh, tq), F32)],
        ),
        out_shape=jax.ShapeDtypeStruct((n_batch, seq, n_heads * dh), BF16),
        compiler_params=_params(3, nbytes),
        name="forgetting_attention",
    )(first_pairs, qp, qp, kp, vt)


def kernel(x, c, mix_norm_g, mlp_norm_g, w_mod, b_mod, w_mlp_in, w_mlp_out, lru_w_in, lru_conv_w, lru_conv_b, lru_w_a, lru_b_a, lru_w_i, lru_b_i, lru_lambda, lru_w_out, kv_norm_g, kv_w_mod, kv_b_mod, w_kv, k_norm_g, w_forget, b_forget, attn_w_q, q_norm_g, attn_w_o):
    n_batch, seq, d = x.shape
    depth = w_mod.shape[0]
    n_a = lru_w_in.shape[0]
    n_heads = w_forget.shape[1]
    assert 1 <= n_a < depth and w_mod.shape[2] == N_MOD * d
    n = n_batch * seq
    tk = _tile(seq, 512)

    mod = _modulation(c, w_mod, b_mod)
    kv_mod = _modulation(c, kv_w_mod[None], kv_b_mod[None])[0]
    per_batch = lambda a: a.reshape(n_batch, 1, d)
    sh1, sc1, g1, sh2, sc2, g2 = [[per_batch(mod[l, :, j * d:(j + 1) * d]) for l in range(depth)]
                                  for j in range(N_MOD)]
    kv_shift, kv_scale = per_batch(kv_mod[:, :d]), per_batch(kv_mod[:, d:])

    bf = lambda w: w.astype(BF16)
    x2 = x.reshape(n, d)
    h = _norm_mod(x2, mix_norm_g[0], sc1[0], sh1[0], seq)
    h_kv = kp = vt = f_end = None
    for layer in range(depth):
        if layer < n_a:
            a = layer
            xg = _lru_in(h, bf(lru_w_in[a]))
            mix_in = _lru_scan(xg, lru_conv_w[a], lru_conv_b[a], bf(lru_w_a[a]), lru_b_a[a],
                               bf(lru_w_i[a]), lru_b_i[a], lru_lambda[a], n_batch, seq)
            w_o = lru_w_out[a]
        else:
            bl = layer - n_a
            if layer == n_a:
                kp, vt, f_end = _kv_proj(h_kv, bf(w_kv), w_forget, b_forget, k_norm_g, n_batch, seq, n_heads, tk)
                f_end = f_end.reshape(n_batch, seq // tk, V7X_LANES)[:, :, :n_heads]
            qp = _q_proj(h, bf(attn_w_q[bl]), q_norm_g[bl], n_batch, seq, n_heads)
            first_pairs = _first_key_pairs(f_end, q_norm_g[bl], k_norm_g, d // n_heads)
            mix_in = _attention(first_pairs, qp, kp, vt).reshape(n, d)
            w_o = attn_w_o[bl]
        x2, h2 = _proj_res(mix_in, bf(w_o), x2, g1[layer], mlp_norm_g[layer], sc2[layer], sh2[layer], seq)
        norms = []
        if layer + 1 < depth:
            norms.append((mix_norm_g[layer + 1], sc1[layer + 1], sh1[layer + 1]))
        if layer + 1 == n_a:
            norms.append((kv_norm_g, kv_scale, kv_shift))
        x2, hs = _mlp(h2, x2, bf(w_mlp_in[layer]), bf(w_mlp_out[layer]), g2[layer], norms, seq)
        if hs:
            h = hs[0]
        if layer + 1 == n_a:
            h_kv = hs[1]
    return x2.reshape(n_batch, seq, d)
```

```python
import functools
import math

import jax
import jax.numpy as jnp
from jax import lax
from jax.experimental import pallas as pl
from jax.experimental.pallas import tpu as pltpu

F32 = jnp.float32
BF16 = jnp.bfloat16

NORM_EPS = 1e-6
LRU_C = 8.0
NEG_INF = -1e30
N_MOD = 6

V7X_LANES = 128
V7X_SUBLANES = 8
V7X_BF16_ROWS = 16
V7X_VMEM_BYTES = 64 * 1024 * 1024
V7X_VMEM_RESERVE = 6 * 1024 * 1024

N_F_PARTS = 3
LOG2_E = 1.4426950408889634
F32_UNDERFLOW_LOG2 = 150.0
QK_BOUND_MARGIN = 1.02


def _vmem_limit(nbytes):
    return int(min(V7X_VMEM_BYTES - V7X_VMEM_RESERVE, max(nbytes * 5 // 4 + (8 << 20), 32 << 20)))


def _params(n_grid, nbytes, flags=None):
    return pltpu.CompilerParams(dimension_semantics=("arbitrary",) * n_grid,
                                vmem_limit_bytes=_vmem_limit(nbytes), flags=flags)


def _tile(n, want):
    t = min(n, want)
    while n % t:
        t -= 1
    return t


def _resident(shape):
    nd = len(shape)
    return pl.BlockSpec(shape, lambda *_: (0,) * nd, pipeline_mode=pl.Buffered(1))


def _rms_mod(x, g, scale, shift):
    ms = jnp.mean(x * x, axis=-1, keepdims=True)
    return (x * lax.rsqrt(ms + NORM_EPS)) * (g * (1.0 + scale)) + shift


def _gelu_tanh(x):
    c = math.sqrt(2.0 / math.pi)
    return x * (0.5 * (1.0 + jnp.tanh(c * (x + 0.044715 * (x * x * x)))))


def _sigmoid(x):
    return 0.5 * jnp.tanh(0.5 * x) + 0.5


def _sqrt_nonneg(y):
    return jnp.where(y > 0.0, y * lax.rsqrt(y), 0.0)


def _log_sigmoid(x):
    return jnp.minimum(x, 0.0) - jnp.log1p(jnp.exp(-jnp.abs(x)))


def _softplus(x):
    return jnp.maximum(x, 0.0) + jnp.log1p(jnp.exp(-jnp.abs(x)))


def _mod_kernel(cb_ref, w_ref, b_ref, o_ref):
    n_batch = cb_ref.shape[0]
    tn = w_ref.shape[2]
    for b in range(n_batch):
        cv = cb_ref[b]
        cs = cv * _sigmoid(cv)
        for ch in range(tn // V7X_LANES):
            cols = slice(ch * V7X_LANES, (ch + 1) * V7X_LANES)
            s = jnp.sum(w_ref[0, :, cols] * cs, axis=0, keepdims=True)
            o_ref[0, b:b + 1, cols] = s + b_ref[0, :, cols]


def _modulation(c, w, b):
    n_l, d, m = w.shape
    n_batch = c.shape[0]
    tn = _tile(m, 1024)
    cb = jnp.broadcast_to(c[:, :, None], (n_batch, d, V7X_LANES))
    nbytes = cb.size * 4 + 2 * d * tn * 4 + 4 * n_batch * tn * 4
    return pl.pallas_call(
        _mod_kernel,
        grid=(n_l, m // tn),
        in_specs=[
            pl.BlockSpec((n_batch, d, V7X_LANES), lambda l, j: (0, 0, 0)),
            pl.BlockSpec((1, d, tn), lambda l, j: (l, 0, j)),
            pl.BlockSpec((1, 1, tn), lambda l, j: (l, 0, j)),
        ],
        out_specs=pl.BlockSpec((1, n_batch, tn), lambda l, j: (l, 0, j)),
        out_shape=jax.ShapeDtypeStruct((n_l, n_batch, m), F32),
        compiler_params=_params(2, nbytes),
        name="modulation",
    )(cb, w, b.reshape(n_l, 1, m))


def _norm_kernel(x_ref, g_ref, sc_ref, sh_ref, o_ref):
    o_ref[...] = _rms_mod(x_ref[...], g_ref[...], sc_ref[0], sh_ref[0]).astype(o_ref.dtype)


def _norm_mod(x2, g, scale, shift, seq):
    n, d = x2.shape
    tm = _tile(seq, 512)
    tpb = seq // tm
    row = lambda i: (i, 0)
    per_batch = lambda i: (i // tpb, 0, 0)
    nbytes = 2 * tm * d * 4 + 2 * tm * d * 2
    return pl.pallas_call(
        _norm_kernel,
        grid=(n // tm,),
        in_specs=[
            pl.BlockSpec((tm, d), row),
            pl.BlockSpec((1, d), lambda i: (0, 0)),
            pl.BlockSpec((1, 1, d), per_batch),
            pl.BlockSpec((1, 1, d), per_batch),
        ],
        out_specs=pl.BlockSpec((tm, d), row),
        out_shape=jax.ShapeDtypeStruct((n, d), BF16),
        compiler_params=_params(1, nbytes),
        name="norm_mod",
    )(x2, g.reshape(1, d), scale, shift)


def _lru_in_kernel(h_ref, w_ref, o_ref, *, tc):
    h = h_ref[...]
    n_cols = w_ref.shape[1]
    for c in range(n_cols // tc):
        cols = slice(c * tc, (c + 1) * tc)
        acc = jnp.dot(h, w_ref[:, cols], preferred_element_type=F32)
        if c * tc >= n_cols // 2:
            acc = _gelu_tanh(acc)
        o_ref[:, cols] = acc


def _lru_in(h, w):
    n, d = h.shape
    m = w.shape[1]
    tm = _tile(n, 512)
    tc = _tile(m // 2, 512)
    nbytes = 2 * tm * d * 2 + d * m * 2 + 2 * tm * m * 4 + 2 * tm * tc * 4
    return pl.pallas_call(
        functools.partial(_lru_in_kernel, tc=tc),
        grid=(n // tm,),
        in_specs=[pl.BlockSpec((tm, d), lambda i: (i, 0)), _resident((d, m))],
        out_specs=pl.BlockSpec((tm, m), lambda i: (i, 0)),
        out_shape=jax.ShapeDtypeStruct((n, m), F32),
        compiler_params=_params(1, nbytes),
        name="lru_in_proj",
    )(h, w)


def _lru_scan_kernel(xb_ref, gy_ref, cw_ref, cb_ref, wa_ref, ba_ref, wi_ref, bi_ref, lam_ref,
                     o_ref, xpad_s, a_s, u_s, h_s, *, lane_chunk):
    ts, width = xb_ref.shape
    conv_width = cw_ref.shape[0]
    n_blocks, bd, _ = wa_ref.shape
    halo = V7X_SUBLANES
    assert conv_width - 1 <= halo

    @pl.when(pl.program_id(1) == 0)
    def _():
        xpad_s[0:halo, :] = jnp.zeros((halo, width), F32)
        h_s[...] = jnp.zeros(h_s.shape, F32)

    xpad_s[halo:halo + ts, :] = xb_ref[...]

    decay = -LRU_C * _softplus(-lam_ref[...])

    for nb in range(n_blocks):
        cols = slice(nb * bd, (nb + 1) * bd)
        xc = cb_ref[:, cols] + cw_ref[conv_width - 1:conv_width, cols] * xpad_s[halo:halo + ts, cols]
        for k in range(conv_width - 1):
            back = conv_width - 1 - k
            xc = xc + cw_ref[k:k + 1, cols] * xpad_s[halo - back:halo - back + ts, cols]
        xcb = xc.astype(BF16)
        r = _sigmoid(jnp.dot(xcb, wa_ref[nb], preferred_element_type=F32) + ba_ref[:, cols])
        gi = _sigmoid(jnp.dot(xcb, wi_ref[nb], preferred_element_type=F32) + bi_ref[:, cols])
        log_a = decay[:, cols] * r
        a = jnp.exp(log_a)
        a_s[:, cols] = a
        u_s[:, cols] = _sqrt_nonneg(-jnp.tanh(log_a) * (a * a + 1.0)) * (gi * xc)

    xpad_s[0:halo, :] = xpad_s[ts:ts + halo, :]

    rows16 = V7X_BF16_ROWS
    row_id = lax.broadcasted_iota(jnp.int32, (V7X_SUBLANES, lane_chunk), 0)

    def group(g, carry):
        r0 = pl.multiple_of(g * rows16, rows16)
        for lc in range(width // lane_chunk):
            cols = slice(lc * lane_chunk, (lc + 1) * lane_chunk)
            hprev = h_s[:, cols]
            halves = []
            for half in range(rows16 // V7X_SUBLANES):
                rr = pl.ds(r0 + half * V7X_SUBLANES, V7X_SUBLANES)
                a = a_s[rr, cols]
                u = u_s[rr, cols]
                for sh in (1, 2, 4):
                    a_sh = pltpu.roll(a, sh, axis=0)
                    u_sh = pltpu.roll(u, sh, axis=0)
                    live = row_id >= sh
                    u = jnp.where(live, a * u_sh + u, u)
                    a = jnp.where(live, a * a_sh, a)
                hcur = a * hprev + u
                hprev = jnp.broadcast_to(hcur[V7X_SUBLANES - 1:V7X_SUBLANES, :], hcur.shape)
                halves.append(hcur)
            h_s[:, cols] = hprev
            h16 = jnp.concatenate(halves, axis=0)
            o_ref[pl.ds(r0, rows16), cols] = (h16 * gy_ref[pl.ds(r0, rows16), cols]).astype(o_ref.dtype)
        return carry

    lax.fori_loop(0, ts // rows16, group, 0)


def _lru_scan(xg, conv_w, conv_b, w_a, b_a, w_i, b_i, lam, n_batch, seq):
    n, two_w = xg.shape
    width = two_w // 2
    ts = _tile(seq, 256)
    tpb = seq // ts
    n_blocks, bd, _ = w_a.shape
    lane_chunk = _tile(width, 512)
    row = lambda b, t: (b * tpb + t, 0)
    vec = lambda a: a.reshape(1, width)
    small = lambda shape: pl.BlockSpec(shape, lambda b, t: (0,) * len(shape))
    nbytes = (4 * ts * width * 4 + 2 * ts * width * 2 + (3 * ts + 16) * width * 4
              + 4 * n_blocks * bd * bd * 2)
    return pl.pallas_call(
        functools.partial(_lru_scan_kernel, lane_chunk=lane_chunk),
        grid=(n_batch, tpb),
        in_specs=[
            pl.BlockSpec((ts, width), row),
            pl.BlockSpec((ts, width), lambda b, t: (b * tpb + t, 1)),
            small(conv_w.shape), small((1, width)),
            small(w_a.shape), small((1, width)),
            small(w_i.shape), small((1, width)),
            small((1, width)),
        ],
        out_specs=pl.BlockSpec((ts, width), row),
        out_shape=jax.ShapeDtypeStruct((n, width), BF16),
        scratch_shapes=[
            pltpu.VMEM((ts + 2 * V7X_SUBLANES, width), F32),
            pltpu.VMEM((ts, width), F32),
            pltpu.VMEM((ts, width), F32),
            pltpu.VMEM((V7X_SUBLANES, width), F32),
        ],
        compiler_params=_params(2, nbytes),
        name="lru_scan",
    )(xg, xg, conv_w, vec(conv_b), w_a, vec(b_a), w_i, vec(b_i), vec(lam))


def _proj_res_kernel(a_ref, w_ref, x_ref, g1_ref, ng_ref, sc_ref, sh_ref, xo_ref, h_ref, *, rc):
    tm = a_ref.shape[0]
    for r in range(tm // rc):
        rows = slice(r * rc, (r + 1) * rc)
        acc = jnp.dot(a_ref[rows, :], w_ref[...], preferred_element_type=F32)
        xn = x_ref[rows, :] + g1_ref[0] * acc
        xo_ref[rows, :] = xn
        h_ref[rows, :] = _rms_mod(xn, ng_ref[...], sc_ref[0], sh_ref[0]).astype(h_ref.dtype)


def _proj_res(a, w, x2, gate, norm_g, scale, shift, seq):
    n, k = a.shape
    d = w.shape[1]
    tm = _tile(seq, 512)
    rc = _tile(tm, 256)
    tpb = seq // tm
    row = lambda i: (i, 0)
    per_batch = lambda i: (i // tpb, 0, 0)
    nbytes = 2 * tm * k * 2 + k * d * 2 + 4 * tm * d * 4 + 2 * tm * d * 2 + 3 * rc * d * 4
    return pl.pallas_call(
        functools.partial(_proj_res_kernel, rc=rc),
        grid=(n // tm,),
        in_specs=[
            pl.BlockSpec((tm, k), row), _resident((k, d)), pl.BlockSpec((tm, d), row),
            pl.BlockSpec((1, 1, d), per_batch), pl.BlockSpec((1, d), lambda i: (0, 0)),
            pl.BlockSpec((1, 1, d), per_batch), pl.BlockSpec((1, 1, d), per_batch),
        ],
        out_specs=[pl.BlockSpec((tm, d), row), pl.BlockSpec((tm, d), row)],
        out_shape=[jax.ShapeDtypeStruct((n, d), F32), jax.ShapeDtypeStruct((n, d), BF16)],
        compiler_params=_params(1, nbytes),
        name="mix_out_proj",
    )(a, w, x2, gate, norm_g.reshape(1, d), scale, shift)


def _mlp_kernel(*refs, n_norm, rc, fc):
    h_ref, win_ref, wout_ref, x_ref, g2_ref = refs[:5]
    norm_refs = refs[5:5 + 3 * n_norm]
    xo_ref = refs[5 + 3 * n_norm]
    ho_refs = refs[6 + 3 * n_norm:6 + 4 * n_norm]
    acc_s = refs[6 + 4 * n_norm]
    k = pl.program_id(1)

    h = h_ref[...]
    tf = win_ref.shape[1]
    pieces = []
    for c in range(tf // fc):
        a = jnp.dot(h, win_ref[:, c * fc:(c + 1) * fc], preferred_element_type=F32)
        pieces.append(jnp.square(jnp.maximum(a, 0.0)).astype(BF16))
    a = pieces[0] if len(pieces) == 1 else jnp.concatenate(pieces, axis=1)

    prev = jnp.where(k > 0, acc_s[...], 0.0)
    acc_s[...] = prev + jnp.dot(a, wout_ref[...], preferred_element_type=F32)

    @pl.when(k == pl.num_programs(1) - 1)
    def _():
        tm = x_ref.shape[0]

        def rows_body(r, carry):
            rows = pl.ds(pl.multiple_of(r * rc, rc), rc)
            xn = x_ref[rows, :] + g2_ref[0] * acc_s[rows, :]
            xo_ref[rows, :] = xn
            for j in range(n_norm):
                ng, sc, sh = norm_refs[3 * j:3 * j + 3]
                ho_refs[j][rows, :] = _rms_mod(xn, ng[...], sc[0], sh[0]).astype(BF16)
            return carry

        lax.fori_loop(0, tm // rc, rows_body, 0)


def _mlp(h2, x2, w_in, w_out, gate, norms, seq):
    n, d = x2.shape
    f = w_in.shape[1]
    tm = _tile(seq, 512)
    tf = _tile(f, 1024)
    fc = _tile(tf, 512)
    rc = _tile(tm, 64)
    tpb = seq // tm
    n_norm = len(norms)
    row = lambda i, k: (i, 0)
    per_batch = lambda i, k: (i // tpb, 0, 0)
    in_specs = [
        pl.BlockSpec((tm, d), row),
        pl.BlockSpec((d, tf), lambda i, k: (0, k)),
        pl.BlockSpec((tf, d), lambda i, k: (k, 0)),
        pl.BlockSpec((tm, d), row),
        pl.BlockSpec((1, 1, d), per_batch),
    ]
    args = [h2, w_in, w_out, x2, gate]
    for ng, sc, sh in norms:
        in_specs += [pl.BlockSpec((1, d), lambda i, k: (0, 0)),
                     pl.BlockSpec((1, 1, d), per_batch), pl.BlockSpec((1, 1, d), per_batch)]
        args += [ng.reshape(1, d), sc, sh]
    out_specs = [pl.BlockSpec((tm, d), row)] * (1 + n_norm)
    out_shape = [jax.ShapeDtypeStruct((n, d), F32)] + [jax.ShapeDtypeStruct((n, d), BF16)] * n_norm
    nbytes = (2 * tm * d * 2 + 4 * d * tf * 2 + 4 * tm * d * 4 + tm * d * 4
              + 2 * n_norm * tm * d * 2 + 2 * tm * tf * 4 + tm * d * 4)
    outs = pl.pallas_call(
        functools.partial(_mlp_kernel, n_norm=n_norm, rc=rc, fc=fc),
        grid=(n // tm, f // tf),
        in_specs=in_specs,
        out_specs=out_specs,
        out_shape=out_shape,
        scratch_shapes=[pltpu.VMEM((tm, d), F32)],
        compiler_params=_params(2, nbytes),
        name="relu2_mlp",
    )(*args)
    return outs[0], list(outs[1:])


def _head_norm(a, g):
    ms = jnp.mean(a * a, axis=-1, keepdims=True)
    return a * lax.rsqrt(ms + NORM_EPS) * g


def _q_kernel(h_ref, w_ref, g_ref, o_ref, *, hc, scale):
    tm = h_ref.shape[0]
    n_heads, dh = o_ref.shape[1], g_ref.shape[1]
    h = h_ref[...]
    row = lax.broadcasted_iota(jnp.int32, (dh, tm), 0)
    ones_rows = jnp.where(row < N_F_PARTS, 1.0, 0.0).astype(o_ref.dtype)
    gs = g_ref[...] * scale
    for c in range(n_heads // hc):
        acc = jnp.dot(h, w_ref[:, c * hc * dh:(c + 1) * hc * dh], preferred_element_type=F32)
        for hh in range(hc):
            head = c * hc + hh
            qn = _head_norm(acc[:, hh * dh:(hh + 1) * dh], gs)
            o_ref[0, head, 0:dh, :] = qn.T.astype(o_ref.dtype)
            o_ref[0, head, dh:2 * dh, :] = ones_rows


def _q_proj(h, w, g, n_batch, seq, n_heads):
    n, d = h.shape
    dh = d // n_heads
    tm = _tile(seq, 512)
    tpb = seq // tm
    hc = _tile(n_heads, 4)
    nbytes = 2 * tm * d * 2 + d * d * 2 + 2 * n_heads * tm * 2 * dh * 2 + 2 * tm * hc * dh * 4
    return pl.pallas_call(
        functools.partial(_q_kernel, hc=hc, scale=dh ** -0.5 * LOG2_E),
        grid=(n // tm,),
        in_specs=[pl.BlockSpec((tm, d), lambda i: (i, 0)), _resident((d, d)),
                  pl.BlockSpec((1, dh), lambda i: (0, 0))],
        out_specs=pl.BlockSpec((1, n_heads, 2 * dh, tm), lambda i: (i // tpb, 0, 0, i % tpb)),
        out_shape=jax.ShapeDtypeStruct((n_batch, n_heads, 2 * dh, seq), BF16),
        compiler_params=_params(1, nbytes),
        name="q_proj",
    )(h, w, g.reshape(1, dh))


def _kv_kernel(h_ref, w_ref, wf_ref, bf_ref, g_ref, ko_ref, vo_ref, fe_ref, carry_s, *, hc, tpb):
    tm, d = h_ref.shape
    n_heads, dh = ko_ref.shape[1], g_ref.shape[1]

    @pl.when(pl.program_id(0) % tpb == 0)
    def _():
        carry_s[...] = jnp.zeros(carry_s.shape, F32)

    h = h_ref[...]

    z = jnp.dot(h, wf_ref[...], preferred_element_type=F32) + bf_ref[...]
    log_f = _log_sigmoid(z)
    ri = lax.broadcasted_iota(jnp.int32, (tm, tm), 0)
    ci = lax.broadcasted_iota(jnp.int32, (tm, tm), 1)
    tri = jnp.where(ri >= ci, 1.0, 0.0).astype(F32)
    cum = jnp.dot(tri, log_f, preferred_element_type=F32, precision=lax.Precision.HIGHEST)
    cum = cum + carry_s[0:1, :]
    carry_s[...] = jnp.broadcast_to(cum[tm - 1:tm, :], carry_s.shape)
    fe_ref[0] = cum[tm - 1:tm, :]

    parts = []
    rest = -LOG2_E * cum
    for _ in range(N_F_PARTS):
        piece = rest.astype(BF16).astype(F32)
        parts.append(piece)
        rest = rest - piece

    lane = lax.broadcasted_iota(jnp.int32, (tm, dh), 1)
    g = g_ref[...]
    for c in range(n_heads // hc):
        acc = jnp.dot(h, w_ref[:, c * hc * dh:(c + 1) * hc * dh], preferred_element_type=F32)
        for hh in range(hc):
            head = c * hc + hh
            kn = _head_norm(acc[:, hh * dh:(hh + 1) * dh], g)
            ko_ref[0, head, :, 0:dh] = kn.astype(ko_ref.dtype)
            aug = jnp.zeros((tm, dh), F32)
            for p in range(N_F_PARTS):
                col = jnp.broadcast_to(parts[p][:, head:head + 1], (tm, dh))
                aug = jnp.where(lane == p, col, aug)
            ko_ref[0, head, :, dh:2 * dh] = aug.astype(ko_ref.dtype)
    for c in range(n_heads // hc):
        acc = jnp.dot(h, w_ref[:, d + c * hc * dh:d + (c + 1) * hc * dh], preferred_element_type=F32)
        for hh in range(hc):
            head = c * hc + hh
            vo_ref[0, head, 0] = acc[:, hh * dh:(hh + 1) * dh].T.astype(vo_ref.dtype)


def _kv_proj(h, w, w_forget, b_forget, g, n_batch, seq, n_heads, tk):
    n, d = h.shape
    dh = d // n_heads
    tm = tk
    tpb = seq // tm
    hc = _tile(n_heads, 4)
    assert n_heads <= V7X_LANES
    wf = jnp.zeros((d, V7X_LANES), BF16).at[:, :n_heads].set(w_forget.astype(BF16))
    bf = jnp.zeros((1, V7X_LANES), F32).at[0, :n_heads].set(b_forget)
    nbytes = (2 * tm * d * 2 + 2 * d * d * 2 + 2 * n_heads * tm * 3 * dh * 2
              + 2 * tm * hc * dh * 4 + 3 * tm * tm * 4)
    return pl.pallas_call(
        functools.partial(_kv_kernel, hc=hc, tpb=tpb),
        grid=(n // tm,),
        in_specs=[pl.BlockSpec((tm, d), lambda i: (i, 0)), _resident((d, 2 * d)),
                  _resident((d, V7X_LANES)), pl.BlockSpec((1, V7X_LANES), lambda i: (0, 0)),
                  pl.BlockSpec((1, dh), lambda i: (0, 0))],
        out_specs=[
            pl.BlockSpec((1, n_heads, tm, 2 * dh), lambda i: (i // tpb, 0, i % tpb, 0)),
            pl.BlockSpec((1, n_heads, 1, dh, tm), lambda i: (i // tpb, 0, i % tpb, 0, 0)),
            pl.BlockSpec((1, 1, V7X_LANES), lambda i: (i, 0, 0)),
        ],
        out_shape=[jax.ShapeDtypeStruct((n_batch, n_heads, seq, 2 * dh), BF16),
                   jax.ShapeDtypeStruct((n_batch, n_heads, seq // tk, dh, tk), BF16),
                   jax.ShapeDtypeStruct((n // tm, 1, V7X_LANES), F32)],
        scratch_shapes=[pltpu.VMEM((V7X_SUBLANES, V7X_LANES), F32)],
        compiler_params=_params(1, nbytes),
        name="kv_proj",
    )(h, w, wf, bf, g.reshape(1, dh))


def _first_key_blocks(f_end, q_gain, k_gain, dh):
    n_batch, n_kv, n_heads = f_end.shape
    nq = n_kv // 2
    qk_max = (QK_BOUND_MARGIN * LOG2_E * math.sqrt(dh)
              * jnp.max(jnp.abs(q_gain)) * jnp.max(jnp.abs(k_gain)))
    first_own = 2 * jnp.arange(nq) - 1
    ref = f_end[:, jnp.maximum(first_own, 0), :]
    gap = LOG2_E * (ref[:, :, None, :] - f_end[:, None, :, :]) + 2.0 * qk_max + 1.0
    below = jnp.arange(n_kv)[None, :] < first_own[:, None]
    needed = jnp.logical_not((gap < -F32_UNDERFLOW_LOG2) & below[None, :, :, None])
    first_block = jnp.argmax(needed, axis=2)
    return jnp.transpose(first_block, (0, 2, 1)).reshape(-1).astype(jnp.int32)


def _attn_kernel(first_ref, q_ref, qn_ref, k_ref, vt_ref, o_ref, s_s, p_s, acc_s, *, qc):
    tq = q_ref.shape[3]
    n_kv, dh, tk = vt_ref.shape[2:]
    assert tq == 2 * tk
    qi = pl.program_id(2)
    n_q = pl.num_programs(2)
    tile = (pl.program_id(0) * pl.num_programs(1) + pl.program_id(1)) * n_q + qi
    j0 = first_ref[tile]
    whole = [slice(0, tq)]
    groups = [slice(g * qc, (g + 1) * qc) for g in range(tq // qc)]
    key_minus_query = (lax.broadcasted_iota(jnp.int32, (tk, qc), 0)
                       - lax.broadcasted_iota(jnp.int32, (tk, qc), 1))

    def scores(j, cols, query_ref=q_ref):
        kblk = k_ref[0, 0, pl.ds(pl.multiple_of(j * tk, tk), tk), :]
        return jnp.dot(kblk, query_ref[0, 0, :, cols], preferred_element_type=F32)

    def softmax(s, state, j, cols, masked):
        m_old, l_old, _ = state
        if masked:
            s = jnp.where(key_minus_query <= qi * tq + cols.start - j * tk, s, NEG_INF)
        m_new = jnp.maximum(m_old, jnp.max(s, axis=0, keepdims=True))
        p = jnp.exp2(s - m_new)
        alpha = jnp.exp2(m_old - m_new)
        l_new = alpha * l_old + jnp.sum(p, axis=0, keepdims=True)
        return p.astype(p_s.dtype), (m_new, l_new, alpha)

    def add_values(p, alpha, j, cols):
        acc_s[:, cols] = alpha * acc_s[:, cols] + jnp.dot(vt_ref[0, 0, j], p, preferred_element_type=F32)

    def pair(a, carry, last):
        new_carry = []
        for cols, state in zip(groups if last else whole, carry):
            mask_a = last and cols.start < tk
            skip_b = last and cols.stop <= tk
            alpha_prev = state[2]
            s_a = s_s[:, cols]
            if not skip_b:
                s_b = scores(a + 1, cols)
            p_a, state = softmax(s_a, state, a, cols, mask_a)
            add_values(p_s[:, cols], alpha_prev, jnp.maximum(a - 1, 0), cols)
            if not last:
                s_s[:, cols] = scores(a + 2, cols)
            alpha_a = state[2]
            if not skip_b:
                p_b, state = softmax(s_b, state, a + 1, cols, last)
            add_values(p_a, alpha_a, a, cols)
            if not last:
                p_s[:, cols] = p_b
            elif not skip_b:
                add_values(p_b, state[2], a + 1, cols)
            new_carry.append(state)
        return tuple(new_carry)

    def single(_, carry):
        p, state = softmax(s_s[...], carry[0], j0, whole[0], False)
        s_s[...] = scores(j0 + 1, whole[0])
        p_s[...] = p
        return (state,)

    odd = j0 % 2
    a0 = j0 + odd
    n_unmasked = qi - a0 // 2

    def two_pairs(v, carry):
        return pair(a0 + 4 * v + 2, pair(a0 + 4 * v, carry, False), False)

    @pl.when(qi == 0)
    def _():
        s_s[...] = scores(j0, whole[0])

    p_s[...] = jnp.zeros(p_s.shape, p_s.dtype)
    acc_s[...] = jnp.zeros(acc_s.shape, F32)
    carry = ((jnp.full((1, tq), NEG_INF, F32), jnp.zeros((1, tq), F32), jnp.ones((1, tq), F32)),)
    carry = lax.fori_loop(0, odd, single, carry)
    carry = lax.fori_loop(0, n_unmasked // 2, two_pairs, carry)
    carry = lax.fori_loop(0, n_unmasked % 2, lambda _, cr: pair(2 * qi - 2, cr, False), carry)
    carry = tuple(tuple(v[:, cols] for v in carry[0]) for cols in groups)
    carry = pair(2 * qi, carry, True)

    next_tile = jnp.minimum(tile + 1, pl.num_programs(0) * pl.num_programs(1) * n_q - 1)
    s_s[...] = scores(first_ref[next_tile], whole[0], qn_ref)

    for cols, state in zip(groups, carry):
        out = acc_s[:, cols] * (1.0 / state[1])
        o_ref[0, cols, :] = out.T.astype(o_ref.dtype)


def _attention(first_pairs, qp, kp, vt):
    n_batch, n_heads, dh2, seq = qp.shape
    n_kv, dh, tk = vt.shape[2:]
    tq = 2 * tk
    qc = _tile(tq, 512)
    nbytes = (2 * tq * dh2 * 2 + 2 * seq * dh2 * 2 + 2 * seq * dh * 2 + 2 * tq * dh * 2
              + dh * tq * 4 + tk * tq * 6 + 4 * tk * tq * 4)
    return pl.pallas_call(
        functools.partial(_attn_kernel, qc=qc),
        grid_spec=pltpu.PrefetchScalarGridSpec(
            num_scalar_prefetch=1,
            grid=(n_batch, n_heads, seq // tq),
            in_specs=[
                pl.BlockSpec((1, 1, dh2, tq), lambda b, h, i, first: (b, h, 0, i)),
                pl.BlockSpec((1, 1, dh2, tq),
                             lambda b, h, i, first: (b, h, 0, jnp.minimum(i + 1, seq // tq - 1))),
                pl.BlockSpec((1, 1, seq, dh2), lambda b, h, i, first: (b, h, 0, 0)),
                pl.BlockSpec((1, 1, n_kv, dh, tk), lambda b, h, i, first: (b, h, 0, 0, 0)),
            ],
            out_specs=pl.BlockSpec((1, tq, dh), lambda b, h, i, first: (b, i, h)),
            scratch_shapes=[pltpu.VMEM((tk, tq), F32), pltpu.VMEM((tk, tq), BF16),
                            pltpu.VMEM((dh, tq), F32)],
        ),
        out_shape=jax.ShapeDtypeStruct((n_batch, seq, n_heads * dh), BF16),
        compiler_params=_params(3, nbytes),
        name="forgetting_attention",
    )(first_pairs, qp, qp, kp, vt)


def kernel(x, c, mix_norm_g, mlp_norm_g, w_mod, b_mod, w_mlp_in, w_mlp_out, lru_w_in, lru_conv_w, lru_conv_b, lru_w_a, lru_b_a, lru_w_i, lru_b_i, lru_lambda, lru_w_out, kv_norm_g, kv_w_mod, kv_b_mod, w_kv, k_norm_g, w_forget, b_forget, attn_w_q, q_norm_g, attn_w_o):
    n_batch, seq, d = x.shape
    depth = w_mod.shape[0]
    n_a = lru_w_in.shape[0]
    n_heads = w_forget.shape[1]
    assert 1 <= n_a < depth and w_mod.shape[2] == N_MOD * d
    n = n_batch * seq
    tk = _tile(seq, 512)

    mod = _modulation(c, w_mod, b_mod)
    kv_mod = _modulation(c, kv_w_mod[None], kv_b_mod[None])[0]
    per_batch = lambda a: a.reshape(n_batch, 1, d)
    sh1, sc1, g1, sh2, sc2, g2 = [[per_batch(mod[l, :, j * d:(j + 1) * d]) for l in range(depth)]
                                  for j in range(N_MOD)]
    kv_shift, kv_scale = per_batch(kv_mod[:, :d]), per_batch(kv_mod[:, d:])

    bf = lambda w: w.astype(BF16)
    x2 = x.reshape(n, d)
    h = _norm_mod(x2, mix_norm_g[0], sc1[0], sh1[0], seq)
    h_kv = kp = vt = f_end = None
    for layer in range(depth):
        if layer < n_a:
            a = layer
            xg = _lru_in(h, bf(lru_w_in[a]))
            mix_in = _lru_scan(xg, lru_conv_w[a], lru_conv_b[a], bf(lru_w_a[a]), lru_b_a[a],
                               bf(lru_w_i[a]), lru_b_i[a], lru_lambda[a], n_batch, seq)
            w_o = lru_w_out[a]
        else:
            bl = layer - n_a
            if layer == n_a:
                kp, vt, f_end = _kv_proj(h_kv, bf(w_kv), w_forget, b_forget, k_norm_g, n_batch, seq, n_heads, tk)
                f_end = f_end.reshape(n_batch, seq // tk, V7X_LANES)[:, :, :n_heads]
            qp = _q_proj(h, bf(attn_w_q[bl]), q_norm_g[bl], n_batch, seq, n_heads)
            first_blocks = _first_key_blocks(f_end, q_norm_g[bl], k_norm_g, d // n_heads)
            mix_in = _attention(first_blocks, qp, kp, vt).reshape(n, d)
            w_o = attn_w_o[bl]
        x2, h2 = _proj_res(mix_in, bf(w_o), x2, g1[layer], mlp_norm_g[layer], sc2[layer], sh2[layer], seq)
        norms = []
        if layer + 1 < depth:
            norms.append((mix_norm_g[layer + 1], sc1[layer + 1], sh1[layer + 1]))
        if layer + 1 == n_a:
            norms.append((kv_norm_g, kv_scale, kv_shift))
        x2, hs = _mlp(h2, x2, bf(w_mlp_in[layer]), bf(w_mlp_out[layer]), g2[layer], norms, seq)
        if hs:
            h = hs[0]
        if layer + 1 == n_a:
            h_kv = hs[1]
    return x2.reshape(n_batch, seq, d)
```

```python
import functools
import math

import jax
import jax.numpy as jnp
from jax import lax
from jax.experimental import pallas as pl
from jax.experimental.pallas import tpu as pltpu

F32 = jnp.float32
BF16 = jnp.bfloat16

NORM_EPS = 1e-6
LRU_C = 8.0
NEG_INF = -1e30
N_MOD = 6

V7X_LANES = 128
V7X_SUBLANES = 8
V7X_BF16_ROWS = 16
V7X_VMEM_BYTES = 64 * 1024 * 1024
V7X_VMEM_RESERVE = 6 * 1024 * 1024

N_F_PARTS = 3
LOG2_E = 1.4426950408889634
F32_UNDERFLOW_LOG2 = 150.0
QK_BOUND_MARGIN = 1.02


def _vmem_limit(nbytes):
    return int(min(V7X_VMEM_BYTES - V7X_VMEM_RESERVE, max(nbytes * 5 // 4 + (8 << 20), 32 << 20)))


def _params(n_grid, nbytes, flags=None):
    return pltpu.CompilerParams(dimension_semantics=("arbitrary",) * n_grid,
                                vmem_limit_bytes=_vmem_limit(nbytes), flags=flags)


def _tile(n, want):
    t = min(n, want)
    while n % t:
        t -= 1
    return t


def _resident(shape):
    nd = len(shape)
    return pl.BlockSpec(shape, lambda *_: (0,) * nd, pipeline_mode=pl.Buffered(1))


def _rms_mod(x, g, scale, shift):
    ms = jnp.mean(x * x, axis=-1, keepdims=True)
    return (x * lax.rsqrt(ms + NORM_EPS)) * (g * (1.0 + scale)) + shift


def _gelu_tanh(x):
    c = math.sqrt(2.0 / math.pi)
    return x * (0.5 * (1.0 + jnp.tanh(c * (x + 0.044715 * (x * x * x)))))


def _sigmoid(x):
    return 0.5 * jnp.tanh(0.5 * x) + 0.5


def _sqrt_nonneg(y):
    return jnp.where(y > 0.0, y * lax.rsqrt(y), 0.0)


def _log_sigmoid(x):
    return jnp.minimum(x, 0.0) - jnp.log1p(jnp.exp(-jnp.abs(x)))


def _softplus(x):
    return jnp.maximum(x, 0.0) + jnp.log1p(jnp.exp(-jnp.abs(x)))


def _mod_kernel(cb_ref, w_ref, b_ref, o_ref):
    n_batch = cb_ref.shape[0]
    tn = w_ref.shape[2]
    for b in range(n_batch):
        cv = cb_ref[b]
        cs = cv * _sigmoid(cv)
        for ch in range(tn // V7X_LANES):
            cols = slice(ch * V7X_LANES, (ch + 1) * V7X_LANES)
            s = jnp.sum(w_ref[0, :, cols] * cs, axis=0, keepdims=True)
            o_ref[0, b:b + 1, cols] = s + b_ref[0, :, cols]


def _modulation(c, w, b):
    n_l, d, m = w.shape
    n_batch = c.shape[0]
    tn = _tile(m, 1024)
    cb = jnp.broadcast_to(c[:, :, None], (n_batch, d, V7X_LANES))
    nbytes = cb.size * 4 + 2 * d * tn * 4 + 4 * n_batch * tn * 4
    return pl.pallas_call(
        _mod_kernel,
        grid=(n_l, m // tn),
        in_specs=[
            pl.BlockSpec((n_batch, d, V7X_LANES), lambda l, j: (0, 0, 0)),
            pl.BlockSpec((1, d, tn), lambda l, j: (l, 0, j)),
            pl.BlockSpec((1, 1, tn), lambda l, j: (l, 0, j)),
        ],
        out_specs=pl.BlockSpec((1, n_batch, tn), lambda l, j: (l, 0, j)),
        out_shape=jax.ShapeDtypeStruct((n_l, n_batch, m), F32),
        compiler_params=_params(2, nbytes),
        name="modulation",
    )(cb, w, b.reshape(n_l, 1, m))


def _norm_kernel(x_ref, g_ref, sc_ref, sh_ref, o_ref):
    o_ref[...] = _rms_mod(x_ref[...], g_ref[...], sc_ref[0], sh_ref[0]).astype(o_ref.dtype)


def _norm_mod(x2, g, scale, shift, seq):
    n, d = x2.shape
    tm = _tile(seq, 512)
    tpb = seq // tm
    row = lambda i: (i, 0)
    per_batch = lambda i: (i // tpb, 0, 0)
    nbytes = 2 * tm * d * 4 + 2 * tm * d * 2
    return pl.pallas_call(
        _norm_kernel,
        grid=(n // tm,),
        in_specs=[
            pl.BlockSpec((tm, d), row),
            pl.BlockSpec((1, d), lambda i: (0, 0)),
            pl.BlockSpec((1, 1, d), per_batch),
            pl.BlockSpec((1, 1, d), per_batch),
        ],
        out_specs=pl.BlockSpec((tm, d), row),
        out_shape=jax.ShapeDtypeStruct((n, d), BF16),
        compiler_params=_params(1, nbytes),
        name="norm_mod",
    )(x2, g.reshape(1, d), scale, shift)


def _lru_in_kernel(h_ref, w_ref, o_ref, *, tc):
    h = h_ref[...]
    n_cols = w_ref.shape[1]
    for c in range(n_cols // tc):
        cols = slice(c * tc, (c + 1) * tc)
        acc = jnp.dot(h, w_ref[:, cols], preferred_element_type=F32)
        if c * tc >= n_cols // 2:
            acc = _gelu_tanh(acc)
        o_ref[:, cols] = acc


def _lru_in(h, w):
    n, d = h.shape
    m = w.shape[1]
    tm = _tile(n, 512)
    tc = _tile(m // 2, 512)
    nbytes = 2 * tm * d * 2 + d * m * 2 + 2 * tm * m * 4 + 2 * tm * tc * 4
    return pl.pallas_call(
        functools.partial(_lru_in_kernel, tc=tc),
        grid=(n // tm,),
        in_specs=[pl.BlockSpec((tm, d), lambda i: (i, 0)), _resident((d, m))],
        out_specs=pl.BlockSpec((tm, m), lambda i: (i, 0)),
        out_shape=jax.ShapeDtypeStruct((n, m), F32),
        compiler_params=_params(1, nbytes),
        name="lru_in_proj",
    )(h, w)


def _lru_scan_kernel(xb_ref, gy_ref, cw_ref, cb_ref, wa_ref, ba_ref, wi_ref, bi_ref, lam_ref,
                     o_ref, xpad_s, a_s, u_s, h_s, *, lane_chunk):
    ts, width = xb_ref.shape
    conv_width = cw_ref.shape[0]
    n_blocks, bd, _ = wa_ref.shape
    halo = V7X_SUBLANES
    assert conv_width - 1 <= halo

    @pl.when(pl.program_id(1) == 0)
    def _():
        xpad_s[0:halo, :] = jnp.zeros((halo, width), F32)
        h_s[...] = jnp.zeros(h_s.shape, F32)

    xpad_s[halo:halo + ts, :] = xb_ref[...]

    decay = -LRU_C * _softplus(-lam_ref[...])

    for nb in range(n_blocks):
        cols = slice(nb * bd, (nb + 1) * bd)
        xc = cb_ref[:, cols] + cw_ref[conv_width - 1:conv_width, cols] * xpad_s[halo:halo + ts, cols]
        for k in range(conv_width - 1):
            back = conv_width - 1 - k
            xc = xc + cw_ref[k:k + 1, cols] * xpad_s[halo - back:halo - back + ts, cols]
        xcb = xc.astype(BF16)
        r = _sigmoid(jnp.dot(xcb, wa_ref[nb], preferred_element_type=F32) + ba_ref[:, cols])
        gi = _sigmoid(jnp.dot(xcb, wi_ref[nb], preferred_element_type=F32) + bi_ref[:, cols])
        log_a = decay[:, cols] * r
        a = jnp.exp(log_a)
        a_s[:, cols] = a
        u_s[:, cols] = _sqrt_nonneg(-jnp.tanh(log_a) * (a * a + 1.0)) * (gi * xc)

    xpad_s[0:halo, :] = xpad_s[ts:ts + halo, :]

    rows16 = V7X_BF16_ROWS
    row_id = lax.broadcasted_iota(jnp.int32, (V7X_SUBLANES, lane_chunk), 0)

    def group(g, carry):
        r0 = pl.multiple_of(g * rows16, rows16)
        for lc in range(width // lane_chunk):
            cols = slice(lc * lane_chunk, (lc + 1) * lane_chunk)
            hprev = h_s[:, cols]
            halves = []
            for half in range(rows16 // V7X_SUBLANES):
                rr = pl.ds(r0 + half * V7X_SUBLANES, V7X_SUBLANES)
                a = a_s[rr, cols]
                u = u_s[rr, cols]
                for sh in (1, 2, 4):
                    a_sh = pltpu.roll(a, sh, axis=0)
                    u_sh = pltpu.roll(u, sh, axis=0)
                    live = row_id >= sh
                    u = jnp.where(live, a * u_sh + u, u)
                    a = jnp.where(live, a * a_sh, a)
                hcur = a * hprev + u
                hprev = jnp.broadcast_to(hcur[V7X_SUBLANES - 1:V7X_SUBLANES, :], hcur.shape)
                halves.append(hcur)
            h_s[:, cols] = hprev
            h16 = jnp.concatenate(halves, axis=0)
            o_ref[pl.ds(r0, rows16), cols] = (h16 * gy_ref[pl.ds(r0, rows16), cols]).astype(o_ref.dtype)
        return carry

    lax.fori_loop(0, ts // rows16, group, 0)


def _lru_scan(xg, conv_w, conv_b, w_a, b_a, w_i, b_i, lam, n_batch, seq):
    n, two_w = xg.shape
    width = two_w // 2
    ts = _tile(seq, 256)
    tpb = seq // ts
    n_blocks, bd, _ = w_a.shape
    lane_chunk = _tile(width, 512)
    row = lambda b, t: (b * tpb + t, 0)
    vec = lambda a: a.reshape(1, width)
    small = lambda shape: pl.BlockSpec(shape, lambda b, t: (0,) * len(shape))
    nbytes = (4 * ts * width * 4 + 2 * ts * width * 2 + (3 * ts + 16) * width * 4
              + 4 * n_blocks * bd * bd * 2)
    return pl.pallas_call(
        functools.partial(_lru_scan_kernel, lane_chunk=lane_chunk),
        grid=(n_batch, tpb),
        in_specs=[
            pl.BlockSpec((ts, width), row),
            pl.BlockSpec((ts, width), lambda b, t: (b * tpb + t, 1)),
            small(conv_w.shape), small((1, width)),
            small(w_a.shape), small((1, width)),
            small(w_i.shape), small((1, width)),
            small((1, width)),
        ],
        out_specs=pl.BlockSpec((ts, width), row),
        out_shape=jax.ShapeDtypeStruct((n, width), BF16),
        scratch_shapes=[
            pltpu.VMEM((ts + 2 * V7X_SUBLANES, width), F32),
            pltpu.VMEM((ts, width), F32),
            pltpu.VMEM((ts, width), F32),
            pltpu.VMEM((V7X_SUBLANES, width), F32),
        ],
        compiler_params=_params(2, nbytes),
        name="lru_scan",
    )(xg, xg, conv_w, vec(conv_b), w_a, vec(b_a), w_i, vec(b_i), vec(lam))


def _proj_res_kernel(a_ref, w_ref, x_ref, g1_ref, ng_ref, sc_ref, sh_ref, xo_ref, h_ref, *, rc):
    tm = a_ref.shape[0]
    for r in range(tm // rc):
        rows = slice(r * rc, (r + 1) * rc)
        acc = jnp.dot(a_ref[rows, :], w_ref[...], preferred_element_type=F32)
        xn = x_ref[rows, :] + g1_ref[0] * acc
        xo_ref[rows, :] = xn
        h_ref[rows, :] = _rms_mod(xn, ng_ref[...], sc_ref[0], sh_ref[0]).astype(h_ref.dtype)


def _proj_res(a, w, x2, gate, norm_g, scale, shift, seq):
    n, k = a.shape
    d = w.shape[1]
    tm = _tile(seq, 512)
    rc = _tile(tm, 256)
    tpb = seq // tm
    row = lambda i: (i, 0)
    per_batch = lambda i: (i // tpb, 0, 0)
    nbytes = 2 * tm * k * 2 + k * d * 2 + 4 * tm * d * 4 + 2 * tm * d * 2 + 3 * rc * d * 4
    return pl.pallas_call(
        functools.partial(_proj_res_kernel, rc=rc),
        grid=(n // tm,),
        in_specs=[
            pl.BlockSpec((tm, k), row), _resident((k, d)), pl.BlockSpec((tm, d), row),
            pl.BlockSpec((1, 1, d), per_batch), pl.BlockSpec((1, d), lambda i: (0, 0)),
            pl.BlockSpec((1, 1, d), per_batch), pl.BlockSpec((1, 1, d), per_batch),
        ],
        out_specs=[pl.BlockSpec((tm, d), row), pl.BlockSpec((tm, d), row)],
        out_shape=[jax.ShapeDtypeStruct((n, d), F32), jax.ShapeDtypeStruct((n, d), BF16)],
        compiler_params=_params(1, nbytes),
        name="mix_out_proj",
    )(a, w, x2, gate, norm_g.reshape(1, d), scale, shift)


def _mlp_kernel(*refs, n_norm, rc, fc):
    h_ref, win_ref, wout_ref, x_ref, g2_ref = refs[:5]
    norm_refs = refs[5:5 + 3 * n_norm]
    xo_ref = refs[5 + 3 * n_norm]
    ho_refs = refs[6 + 3 * n_norm:6 + 4 * n_norm]
    acc_s = refs[6 + 4 * n_norm]
    k = pl.program_id(1)

    h = h_ref[...]
    tf = win_ref.shape[1]
    pieces = []
    for c in range(tf // fc):
        a = jnp.dot(h, win_ref[:, c * fc:(c + 1) * fc], preferred_element_type=F32)
        pieces.append(jnp.square(jnp.maximum(a, 0.0)).astype(BF16))
    a = pieces[0] if len(pieces) == 1 else jnp.concatenate(pieces, axis=1)

    prev = jnp.where(k > 0, acc_s[...], 0.0)
    acc_s[...] = prev + jnp.dot(a, wout_ref[...], preferred_element_type=F32)

    @pl.when(k == pl.num_programs(1) - 1)
    def _():
        tm = x_ref.shape[0]

        def rows_body(r, carry):
            rows = pl.ds(pl.multiple_of(r * rc, rc), rc)
            xn = x_ref[rows, :] + g2_ref[0] * acc_s[rows, :]
            xo_ref[rows, :] = xn
            for j in range(n_norm):
                ng, sc, sh = norm_refs[3 * j:3 * j + 3]
                ho_refs[j][rows, :] = _rms_mod(xn, ng[...], sc[0], sh[0]).astype(BF16)
            return carry

        lax.fori_loop(0, tm // rc, rows_body, 0)


def _mlp(h2, x2, w_in, w_out, gate, norms, seq):
    n, d = x2.shape
    f = w_in.shape[1]
    tm = _tile(seq, 512)
    tf = _tile(f, 1024)
    fc = _tile(tf, 512)
    rc = _tile(tm, 64)
    tpb = seq // tm
    n_norm = len(norms)
    row = lambda i, k: (i, 0)
    per_batch = lambda i, k: (i // tpb, 0, 0)
    in_specs = [
        pl.BlockSpec((tm, d), row),
        pl.BlockSpec((d, tf), lambda i, k: (0, k)),
        pl.BlockSpec((tf, d), lambda i, k: (k, 0)),
        pl.BlockSpec((tm, d), row),
        pl.BlockSpec((1, 1, d), per_batch),
    ]
    args = [h2, w_in, w_out, x2, gate]
    for ng, sc, sh in norms:
        in_specs += [pl.BlockSpec((1, d), lambda i, k: (0, 0)),
                     pl.BlockSpec((1, 1, d), per_batch), pl.BlockSpec((1, 1, d), per_batch)]
        args += [ng.reshape(1, d), sc, sh]
    out_specs = [pl.BlockSpec((tm, d), row)] * (1 + n_norm)
    out_shape = [jax.ShapeDtypeStruct((n, d), F32)] + [jax.ShapeDtypeStruct((n, d), BF16)] * n_norm
    nbytes = (2 * tm * d * 2 + 4 * d * tf * 2 + 4 * tm * d * 4 + tm * d * 4
              + 2 * n_norm * tm * d * 2 + 2 * tm * tf * 4 + tm * d * 4)
    outs = pl.pallas_call(
        functools.partial(_mlp_kernel, n_norm=n_norm, rc=rc, fc=fc),
        grid=(n // tm, f // tf),
        in_specs=in_specs,
        out_specs=out_specs,
        out_shape=out_shape,
        scratch_shapes=[pltpu.VMEM((tm, d), F32)],
        compiler_params=_params(2, nbytes),
        name="relu2_mlp",
    )(*args)
    return outs[0], list(outs[1:])


def _head_norm(a, g):
    ms = jnp.mean(a * a, axis=-1, keepdims=True)
    return a * lax.rsqrt(ms + NORM_EPS) * g


def _q_kernel(h_ref, w_ref, g_ref, o_ref, *, hc, scale):
    tm = h_ref.shape[0]
    n_heads, dh = o_ref.shape[1], g_ref.shape[1]
    h = h_ref[...]
    row = lax.broadcasted_iota(jnp.int32, (dh, tm), 0)
    ones_rows = jnp.where(row < N_F_PARTS, 1.0, 0.0).astype(o_ref.dtype)
    gs = g_ref[...] * scale
    for c in range(n_heads // hc):
        acc = jnp.dot(h, w_ref[:, c * hc * dh:(c + 1) * hc * dh], preferred_element_type=F32)
        for hh in range(hc):
            head = c * hc + hh
            qn = _head_norm(acc[:, hh * dh:(hh + 1) * dh], gs)
            o_ref[0, head, 0:dh, :] = qn.T.astype(o_ref.dtype)
            o_ref[0, head, dh:2 * dh, :] = ones_rows


def _q_proj(h, w, g, n_batch, seq, n_heads):
    n, d = h.shape
    dh = d // n_heads
    tm = _tile(seq, 512)
    tpb = seq // tm
    hc = _tile(n_heads, 4)
    nbytes = 2 * tm * d * 2 + d * d * 2 + 2 * n_heads * tm * 2 * dh * 2 + 2 * tm * hc * dh * 4
    return pl.pallas_call(
        functools.partial(_q_kernel, hc=hc, scale=dh ** -0.5 * LOG2_E),
        grid=(n // tm,),
        in_specs=[pl.BlockSpec((tm, d), lambda i: (i, 0)), _resident((d, d)),
                  pl.BlockSpec((1, dh), lambda i: (0, 0))],
        out_specs=pl.BlockSpec((1, n_heads, 2 * dh, tm), lambda i: (i // tpb, 0, 0, i % tpb)),
        out_shape=jax.ShapeDtypeStruct((n_batch, n_heads, 2 * dh, seq), BF16),
        compiler_params=_params(1, nbytes),
        name="q_proj",
    )(h, w, g.reshape(1, dh))


def _kv_kernel(h_ref, w_ref, wf_ref, bf_ref, g_ref, ko_ref, vo_ref, fe_ref, carry_s, *, hc, tpb):
    tm, d = h_ref.shape
    n_heads, dh = ko_ref.shape[1], g_ref.shape[1]

    @pl.when(pl.program_id(0) % tpb == 0)
    def _():
        carry_s[...] = jnp.zeros(carry_s.shape, F32)

    h = h_ref[...]

    z = jnp.dot(h, wf_ref[...], preferred_element_type=F32) + bf_ref[...]
    log_f = _log_sigmoid(z)
    ri = lax.broadcasted_iota(jnp.int32, (tm, tm), 0)
    ci = lax.broadcasted_iota(jnp.int32, (tm, tm), 1)
    tri = jnp.where(ri >= ci, 1.0, 0.0).astype(F32)
    cum = jnp.dot(tri, log_f, preferred_element_type=F32, precision=lax.Precision.HIGHEST)
    cum = cum + carry_s[0:1, :]
    carry_s[...] = jnp.broadcast_to(cum[tm - 1:tm, :], carry_s.shape)
    fe_ref[0] = cum[tm - 1:tm, :]

    parts = []
    rest = -LOG2_E * cum
    for _ in range(N_F_PARTS):
        piece = rest.astype(BF16).astype(F32)
        parts.append(piece)
        rest = rest - piece

    lane = lax.broadcasted_iota(jnp.int32, (tm, dh), 1)
    g = g_ref[...]
    for c in range(n_heads // hc):
        acc = jnp.dot(h, w_ref[:, c * hc * dh:(c + 1) * hc * dh], preferred_element_type=F32)
        for hh in range(hc):
            head = c * hc + hh
            kn = _head_norm(acc[:, hh * dh:(hh + 1) * dh], g)
            ko_ref[0, head, :, 0:dh] = kn.astype(ko_ref.dtype)
            aug = jnp.zeros((tm, dh), F32)
            for p in range(N_F_PARTS):
                col = jnp.broadcast_to(parts[p][:, head:head + 1], (tm, dh))
                aug = jnp.where(lane == p, col, aug)
            ko_ref[0, head, :, dh:2 * dh] = aug.astype(ko_ref.dtype)
    for c in range(n_heads // hc):
        acc = jnp.dot(h, w_ref[:, d + c * hc * dh:d + (c + 1) * hc * dh], preferred_element_type=F32)
        for hh in range(hc):
            head = c * hc + hh
            vo_ref[0, head, 0] = acc[:, hh * dh:(hh + 1) * dh].T.astype(vo_ref.dtype)


def _kv_proj(h, w, w_forget, b_forget, g, n_batch, seq, n_heads, tk):
    n, d = h.shape
    dh = d // n_heads
    tm = tk
    tpb = seq // tm
    hc = _tile(n_heads, 4)
    assert n_heads <= V7X_LANES
    wf = jnp.zeros((d, V7X_LANES), BF16).at[:, :n_heads].set(w_forget.astype(BF16))
    bf = jnp.zeros((1, V7X_LANES), F32).at[0, :n_heads].set(b_forget)
    nbytes = (2 * tm * d * 2 + 2 * d * d * 2 + 2 * n_heads * tm * 3 * dh * 2
              + 2 * tm * hc * dh * 4 + 3 * tm * tm * 4)
    return pl.pallas_call(
        functools.partial(_kv_kernel, hc=hc, tpb=tpb),
        grid=(n // tm,),
        in_specs=[pl.BlockSpec((tm, d), lambda i: (i, 0)), _resident((d, 2 * d)),
                  _resident((d, V7X_LANES)), pl.BlockSpec((1, V7X_LANES), lambda i: (0, 0)),
                  pl.BlockSpec((1, dh), lambda i: (0, 0))],
        out_specs=[
            pl.BlockSpec((1, n_heads, tm, 2 * dh), lambda i: (i // tpb, 0, i % tpb, 0)),
            pl.BlockSpec((1, n_heads, 1, dh, tm), lambda i: (i // tpb, 0, i % tpb, 0, 0)),
            pl.BlockSpec((1, 1, V7X_LANES), lambda i: (i, 0, 0)),
        ],
        out_shape=[jax.ShapeDtypeStruct((n_batch, n_heads, seq, 2 * dh), BF16),
                   jax.ShapeDtypeStruct((n_batch, n_heads, seq // tk, dh, tk), BF16),
                   jax.ShapeDtypeStruct((n // tm, 1, V7X_LANES), F32)],
        scratch_shapes=[pltpu.VMEM((V7X_SUBLANES, V7X_LANES), F32)],
        compiler_params=_params(1, nbytes),
        name="kv_proj",
    )(h, w, wf, bf, g.reshape(1, dh))


def _first_key_blocks(f_end, q_gain, k_gain, dh):
    n_batch, n_kv, n_heads = f_end.shape
    nq = n_kv // 2
    qk_max = (QK_BOUND_MARGIN * LOG2_E * math.sqrt(dh)
              * jnp.max(jnp.abs(q_gain)) * jnp.max(jnp.abs(k_gain)))
    first_own = 2 * jnp.arange(nq) - 1
    ref = f_end[:, jnp.maximum(first_own, 0), :]
    gap = LOG2_E * (ref[:, :, None, :] - f_end[:, None, :, :]) + 2.0 * qk_max + 1.0
    below = jnp.arange(n_kv)[None, :] < first_own[:, None]
    needed = jnp.logical_not((gap < -F32_UNDERFLOW_LOG2) & below[None, :, :, None])
    first_block = jnp.argmax(needed, axis=2)
    return jnp.transpose(first_block, (0, 2, 1)).reshape(-1).astype(jnp.int32)


def _attn_kernel(first_ref, q_ref, qn_ref, k_ref, vt_ref, o_ref, s_s, p_s, acc_s, *, qc):
    tq = q_ref.shape[3]
    n_kv, dh, tk = vt_ref.shape[2:]
    assert tq == 2 * tk
    qi = pl.program_id(2)
    n_q = pl.num_programs(2)
    tile = (pl.program_id(0) * pl.num_programs(1) + pl.program_id(1)) * n_q + qi
    j0 = first_ref[tile]
    whole = [slice(0, tq)]
    groups = [slice(g * qc, (g + 1) * qc) for g in range(tq // qc)]
    key_minus_query = (lax.broadcasted_iota(jnp.int32, (tk, qc), 0)
                       - lax.broadcasted_iota(jnp.int32, (tk, qc), 1))

    def scores(j, cols, query_ref=q_ref):
        kblk = k_ref[0, 0, pl.ds(pl.multiple_of(j * tk, tk), tk), :]
        return jnp.dot(kblk, query_ref[0, 0, :, cols], preferred_element_type=F32)

    def softmax(s, state, j, cols, masked):
        m_old, _ = state
        if masked:
            s = jnp.where(key_minus_query <= qi * tq + cols.start - j * tk, s, NEG_INF)
        m_new = jnp.maximum(m_old, jnp.max(s, axis=0, keepdims=True))
        p = jnp.exp2(s - m_new)
        alpha = jnp.exp2(m_old - m_new)
        return p.astype(p_s.dtype), (m_new, alpha)

    ones_rows = jnp.ones((V7X_BF16_ROWS, tk), p_s.dtype)

    def add_values(p, alpha, j, cols):
        vt1 = jnp.concatenate([vt_ref[0, 0, j], ones_rows], axis=0)
        acc_s[:, cols] = alpha * acc_s[:, cols] + jnp.dot(vt1, p, preferred_element_type=F32)

    def pair(a, carry, last):
        new_carry = []
        for cols, state in zip(groups if last else whole, carry):
            mask_a = last and cols.start < tk
            skip_b = last and cols.stop <= tk
            alpha_prev = state[1]
            s_a = s_s[:, cols]
            if not skip_b:
                s_b = scores(a + 1, cols)
            p_a, state = softmax(s_a, state, a, cols, mask_a)
            add_values(p_s[:, cols], alpha_prev, jnp.maximum(a - 1, 0), cols)
            if not last:
                s_s[:, cols] = scores(a + 2, cols)
            alpha_a = state[1]
            if not skip_b:
                p_b, state = softmax(s_b, state, a + 1, cols, last)
            add_values(p_a, alpha_a, a, cols)
            if not last:
                p_s[:, cols] = p_b
            elif not skip_b:
                add_values(p_b, state[1], a + 1, cols)
            new_carry.append(state)
        return tuple(new_carry)

    def single(_, carry):
        p, state = softmax(s_s[...], carry[0], j0, whole[0], False)
        s_s[...] = scores(j0 + 1, whole[0])
        p_s[...] = p
        return (state,)

    odd = j0 % 2
    a0 = j0 + odd
    n_unmasked = qi - a0 // 2

    def two_pairs(v, carry):
        return pair(a0 + 4 * v + 2, pair(a0 + 4 * v, carry, False), False)

    @pl.when(qi == 0)
    def _():
        s_s[...] = scores(j0, whole[0])

    p_s[...] = jnp.zeros(p_s.shape, p_s.dtype)
    acc_s[...] = jnp.zeros(acc_s.shape, F32)
    carry = ((jnp.full((1, tq), NEG_INF, F32), jnp.ones((1, tq), F32)),)
    carry = lax.fori_loop(0, odd, single, carry)
    carry = lax.fori_loop(0, n_unmasked // 2, two_pairs, carry)
    carry = lax.fori_loop(0, n_unmasked % 2, lambda _, cr: pair(2 * qi - 2, cr, False), carry)
    carry = tuple(tuple(v[:, cols] for v in carry[0]) for cols in groups)
    carry = pair(2 * qi, carry, True)

    next_tile = jnp.minimum(tile + 1, pl.num_programs(0) * pl.num_programs(1) * n_q - 1)
    s_s[...] = scores(first_ref[next_tile], whole[0], qn_ref)

    for cols in groups:
        out = acc_s[0:dh, cols] * (1.0 / acc_s[dh:dh + 1, cols])
        o_ref[0, cols, :] = out.T.astype(o_ref.dtype)


def _attention(first_pairs, qp, kp, vt):
    n_batch, n_heads, dh2, seq = qp.shape
    n_kv, dh, tk = vt.shape[2:]
    tq = 2 * tk
    qc = _tile(tq, 512)
    nbytes = (2 * tq * dh2 * 2 + 2 * seq * dh2 * 2 + 2 * seq * dh * 2 + 2 * tq * dh * 2
              + dh * tq * 4 + tk * tq * 6 + 4 * tk * tq * 4)
    return pl.pallas_call(
        functools.partial(_attn_kernel, qc=qc),
        grid_spec=pltpu.PrefetchScalarGridSpec(
            num_scalar_prefetch=1,
            grid=(n_batch, n_heads, seq // tq),
            in_specs=[
                pl.BlockSpec((1, 1, dh2, tq), lambda b, h, i, first: (b, h, 0, i)),
                pl.BlockSpec((1, 1, dh2, tq),
                             lambda b, h, i, first: (b, h, 0, jnp.minimum(i + 1, seq // tq - 1))),
                pl.BlockSpec((1, 1, seq, dh2), lambda b, h, i, first: (b, h, 0, 0)),
                pl.BlockSpec((1, 1, n_kv, dh, tk), lambda b, h, i, first: (b, h, 0, 0, 0)),
            ],
            out_specs=pl.BlockSpec((1, tq, dh), lambda b, h, i, first: (b, i, h)),
            scratch_shapes=[pltpu.VMEM((tk, tq), F32), pltpu.VMEM((tk, tq), BF16),
                            pltpu.VMEM((dh + V7X_BF16_ROWS, tq), F32)],
        ),
        out_shape=jax.ShapeDtypeStruct((n_batch, seq, n_heads * dh), BF16),
        compiler_params=_params(3, nbytes),
        name="forgetting_attention",
    )(first_pairs, qp, qp, kp, vt)


def kernel(x, c, mix_norm_g, mlp_norm_g, w_mod, b_mod, w_mlp_in, w_mlp_out, lru_w_in, lru_conv_w, lru_conv_b, lru_w_a, lru_b_a, lru_w_i, lru_b_i, lru_lambda, lru_w_out, kv_norm_g, kv_w_mod, kv_b_mod, w_kv, k_norm_g, w_forget, b_forget, attn_w_q, q_norm_g, attn_w_o):
    n_batch, seq, d = x.shape
    depth = w_mod.shape[0]
    n_a = lru_w_in.shape[0]
    n_heads = w_forget.shape[1]
    assert 1 <= n_a < depth and w_mod.shape[2] == N_MOD * d
    n = n_batch * seq
    tk = _tile(seq, 512)

    mod = _modulation(c, w_mod, b_mod)
    kv_mod = _modulation(c, kv_w_mod[None], kv_b_mod[None])[0]
    per_batch = lambda a: a.reshape(n_batch, 1, d)
    sh1, sc1, g1, sh2, sc2, g2 = [[per_batch(mod[l, :, j * d:(j + 1) * d]) for l in range(depth)]
                                  for j in range(N_MOD)]
    kv_shift, kv_scale = per_batch(kv_mod[:, :d]), per_batch(kv_mod[:, d:])

    bf = lambda w: w.astype(BF16)
    x2 = x.reshape(n, d)
    h = _norm_mod(x2, mix_norm_g[0], sc1[0], sh1[0], seq)
    h_kv = kp = vt = f_end = None
    for layer in range(depth):
        if layer < n_a:
            a = layer
            xg = _lru_in(h, bf(lru_w_in[a]))
            mix_in = _lru_scan(xg, lru_conv_w[a], lru_conv_b[a], bf(lru_w_a[a]), lru_b_a[a],
                               bf(lru_w_i[a]), lru_b_i[a], lru_lambda[a], n_batch, seq)
            w_o = lru_w_out[a]
        else:
            bl = layer - n_a
            if layer == n_a:
                kp, vt, f_end = _kv_proj(h_kv, bf(w_kv), w_forget, b_forget, k_norm_g, n_batch, seq, n_heads, tk)
                f_end = f_end.reshape(n_batch, seq // tk, V7X_LANES)[:, :, :n_heads]
            qp = _q_proj(h, bf(attn_w_q[bl]), q_norm_g[bl], n_batch, seq, n_heads)
            first_blocks = _first_key_blocks(f_end, q_norm_g[bl], k_norm_g, d // n_heads)
            mix_in = _attention(first_blocks, qp, kp, vt).reshape(n, d)
            w_o = attn_w_o[bl]
        x2, h2 = _proj_res(mix_in, bf(w_o), x2, g1[layer], mlp_norm_g[layer], sc2[layer], sh2[layer], seq)
        norms = []
        if layer + 1 < depth:
            norms.append((mix_norm_g[layer + 1], sc1[layer + 1], sh1[layer + 1]))
        if layer + 1 == n_a:
            norms.append((kv_norm_g, kv_scale, kv_shift))
        x2, hs = _mlp(h2, x2, bf(w_mlp_in[layer]), bf(w_mlp_out[layer]), g2[layer], norms, seq)
        if hs:
            h = hs[0]
        if layer + 1 == n_a:
            h_kv = hs[1]
    return x2.reshape(n_batch, seq, d)
```

```python
import functools
import math

import jax
import jax.numpy as jnp
from jax import lax
from jax.experimental import pallas as pl
from jax.experimental.pallas import tpu as pltpu

F32 = jnp.float32
BF16 = jnp.bfloat16

NORM_EPS = 1e-6
LRU_C = 8.0
NEG_INF = -1e30
N_MOD = 6

V7X_LANES = 128
V7X_SUBLANES = 8
V7X_BF16_ROWS = 16
V7X_VMEM_BYTES = 64 * 1024 * 1024
V7X_VMEM_RESERVE = 6 * 1024 * 1024

N_F_PARTS = 3
LOG2_E = 1.4426950408889634
F32_UNDERFLOW_LOG2 = 150.0
QK_BOUND_MARGIN = 1.02


def _vmem_limit(nbytes):
    return int(min(V7X_VMEM_BYTES - V7X_VMEM_RESERVE, max(nbytes * 5 // 4 + (8 << 20), 32 << 20)))


def _params(n_grid, nbytes, flags=None):
    return pltpu.CompilerParams(dimension_semantics=("arbitrary",) * n_grid,
                                vmem_limit_bytes=_vmem_limit(nbytes), flags=flags)


def _tile(n, want):
    t = min(n, want)
    while n % t:
        t -= 1
    return t


def _resident(shape, layer=None):
    nd = len(shape)
    if layer is None:
        return pl.BlockSpec(shape, lambda *_: (0,) * nd, pipeline_mode=pl.Buffered(1))
    return pl.BlockSpec((pl.Squeezed(),) + tuple(shape), lambda *_: (layer,) + (0,) * nd,
                        pipeline_mode=pl.Buffered(1))


def _rms_mod(x, g, scale, shift):
    ms = jnp.mean(x * x, axis=-1, keepdims=True)
    return (x * lax.rsqrt(ms + NORM_EPS)) * (g * (1.0 + scale)) + shift


def _gelu_tanh(x):
    c = math.sqrt(2.0 / math.pi)
    return x * (0.5 * (1.0 + jnp.tanh(c * (x + 0.044715 * (x * x * x)))))


def _sigmoid(x):
    return 0.5 * jnp.tanh(0.5 * x) + 0.5


def _sqrt_nonneg(y):
    return jnp.where(y > 0.0, y * lax.rsqrt(y), 0.0)


def _log_sigmoid(x):
    return jnp.minimum(x, 0.0) - jnp.log1p(jnp.exp(-jnp.abs(x)))


def _softplus(x):
    return jnp.maximum(x, 0.0) + jnp.log1p(jnp.exp(-jnp.abs(x)))


def _mod_kernel(cb_ref, w_ref, b_ref, o_ref):
    n_batch = cb_ref.shape[0]
    tn = w_ref.shape[2]
    for b in range(n_batch):
        cv = cb_ref[b]
        cs = cv * _sigmoid(cv)
        for ch in range(tn // V7X_LANES):
            cols = slice(ch * V7X_LANES, (ch + 1) * V7X_LANES)
            s = jnp.sum(w_ref[0, :, cols] * cs, axis=0, keepdims=True)
            o_ref[0, b:b + 1, cols] = s + b_ref[0, :, cols]


def _modulation(c, w, b):
    n_l, d, m = w.shape
    n_batch = c.shape[0]
    tn = _tile(m, 1024)
    cb = jnp.broadcast_to(c[:, :, None], (n_batch, d, V7X_LANES))
    nbytes = cb.size * 4 + 2 * d * tn * 4 + 4 * n_batch * tn * 4
    return pl.pallas_call(
        _mod_kernel,
        grid=(n_l, m // tn),
        in_specs=[
            pl.BlockSpec((n_batch, d, V7X_LANES), lambda l, j: (0, 0, 0)),
            pl.BlockSpec((1, d, tn), lambda l, j: (l, 0, j)),
            pl.BlockSpec((1, 1, tn), lambda l, j: (l, 0, j)),
        ],
        out_specs=pl.BlockSpec((1, n_batch, tn), lambda l, j: (l, 0, j)),
        out_shape=jax.ShapeDtypeStruct((n_l, n_batch, m), F32),
        compiler_params=_params(2, nbytes),
        name="modulation",
    )(cb, w, b.reshape(n_l, 1, m))


def _norm_kernel(x_ref, g_ref, sc_ref, sh_ref, o_ref):
    o_ref[...] = _rms_mod(x_ref[...], g_ref[...], sc_ref[0], sh_ref[0]).astype(o_ref.dtype)


def _norm_mod(x2, g, scale, shift, seq):
    n, d = x2.shape
    tm = _tile(seq, 512)
    tpb = seq // tm
    row = lambda i: (i, 0)
    per_batch = lambda i: (i // tpb, 0, 0)
    nbytes = 2 * tm * d * 4 + 2 * tm * d * 2
    return pl.pallas_call(
        _norm_kernel,
        grid=(n // tm,),
        in_specs=[
            pl.BlockSpec((tm, d), row),
            pl.BlockSpec((1, d), lambda i: (0, 0)),
            pl.BlockSpec((1, 1, d), per_batch),
            pl.BlockSpec((1, 1, d), per_batch),
        ],
        out_specs=pl.BlockSpec((tm, d), row),
        out_shape=jax.ShapeDtypeStruct((n, d), BF16),
        compiler_params=_params(1, nbytes),
        name="norm_mod",
    )(x2, g.reshape(1, d), scale, shift)


def _lru_in_kernel(h_ref, w_ref, o_ref, *, tc):
    h = h_ref[...]
    n_cols = w_ref.shape[1]
    for c in range(n_cols // tc):
        cols = slice(c * tc, (c + 1) * tc)
        acc = jnp.dot(h, w_ref[:, cols], preferred_element_type=F32)
        if c * tc >= n_cols // 2:
            acc = _gelu_tanh(acc)
        o_ref[:, cols] = acc


def _lru_in(h, w, layer):
    n, d = h.shape
    m = w.shape[2]
    tm = _tile(n, 512)
    tc = _tile(m // 2, 512)
    nbytes = 2 * tm * d * 2 + d * m * 2 + 2 * tm * m * 4 + 2 * tm * tc * 4
    return pl.pallas_call(
        functools.partial(_lru_in_kernel, tc=tc),
        grid=(n // tm,),
        in_specs=[pl.BlockSpec((tm, d), lambda i: (i, 0)), _resident((d, m), layer)],
        out_specs=pl.BlockSpec((tm, m), lambda i: (i, 0)),
        out_shape=jax.ShapeDtypeStruct((n, m), F32),
        compiler_params=_params(1, nbytes),
        name="lru_in_proj",
    )(h, w)


def _lru_scan_kernel(xb_ref, gy_ref, cw_ref, cb_ref, wa_ref, ba_ref, wi_ref, bi_ref, lam_ref,
                     o_ref, xpad_s, a_s, u_s, h_s, *, lane_chunk):
    ts, width = xb_ref.shape
    conv_width = cw_ref.shape[0]
    n_blocks, bd, _ = wa_ref.shape
    halo = V7X_SUBLANES
    assert conv_width - 1 <= halo

    @pl.when(pl.program_id(1) == 0)
    def _():
        xpad_s[0:halo, :] = jnp.zeros((halo, width), F32)
        h_s[...] = jnp.zeros(h_s.shape, F32)

    xpad_s[halo:halo + ts, :] = xb_ref[...]

    decay = -LRU_C * _softplus(-lam_ref[...])

    for nb in range(n_blocks):
        cols = slice(nb * bd, (nb + 1) * bd)
        xc = cb_ref[:, cols] + cw_ref[conv_width - 1:conv_width, cols] * xpad_s[halo:halo + ts, cols]
        for k in range(conv_width - 1):
            back = conv_width - 1 - k
            xc = xc + cw_ref[k:k + 1, cols] * xpad_s[halo - back:halo - back + ts, cols]
        xcb = xc.astype(BF16)
        r = _sigmoid(jnp.dot(xcb, wa_ref[nb], preferred_element_type=F32) + ba_ref[:, cols])
        gi = _sigmoid(jnp.dot(xcb, wi_ref[nb], preferred_element_type=F32) + bi_ref[:, cols])
        log_a = decay[:, cols] * r
        a = jnp.exp(log_a)
        a_s[:, cols] = a
        u_s[:, cols] = _sqrt_nonneg(-jnp.tanh(log_a) * (a * a + 1.0)) * (gi * xc)

    xpad_s[0:halo, :] = xpad_s[ts:ts + halo, :]

    rows16 = V7X_BF16_ROWS
    row_id = lax.broadcasted_iota(jnp.int32, (V7X_SUBLANES, lane_chunk), 0)

    def group(g, carry):
        r0 = pl.multiple_of(g * rows16, rows16)
        for lc in range(width // lane_chunk):
            cols = slice(lc * lane_chunk, (lc + 1) * lane_chunk)
            hprev = h_s[:, cols]
            halves = []
            for half in range(rows16 // V7X_SUBLANES):
                rr = pl.ds(r0 + half * V7X_SUBLANES, V7X_SUBLANES)
                a = a_s[rr, cols]
                u = u_s[rr, cols]
                for sh in (1, 2, 4):
                    a_sh = pltpu.roll(a, sh, axis=0)
                    u_sh = pltpu.roll(u, sh, axis=0)
                    live = row_id >= sh
                    u = jnp.where(live, a * u_sh + u, u)
                    a = jnp.where(live, a * a_sh, a)
                hcur = a * hprev + u
                hprev = jnp.broadcast_to(hcur[V7X_SUBLANES - 1:V7X_SUBLANES, :], hcur.shape)
                halves.append(hcur)
            h_s[:, cols] = hprev
            h16 = jnp.concatenate(halves, axis=0)
            o_ref[pl.ds(r0, rows16), cols] = (h16 * gy_ref[pl.ds(r0, rows16), cols]).astype(o_ref.dtype)
        return carry

    lax.fori_loop(0, ts // rows16, group, 0)


def _lru_scan(xg, conv_w, conv_b, w_a, b_a, w_i, b_i, lam, n_batch, seq):
    n, two_w = xg.shape
    width = two_w // 2
    ts = _tile(seq, 256)
    tpb = seq // ts
    n_blocks, bd, _ = w_a.shape
    lane_chunk = _tile(width, 512)
    row = lambda b, t: (b * tpb + t, 0)
    vec = lambda a: a.reshape(1, width)
    small = lambda shape: pl.BlockSpec(shape, lambda b, t: (0,) * len(shape))
    nbytes = (4 * ts * width * 4 + 2 * ts * width * 2 + (3 * ts + 16) * width * 4
              + 4 * n_blocks * bd * bd * 2)
    return pl.pallas_call(
        functools.partial(_lru_scan_kernel, lane_chunk=lane_chunk),
        grid=(n_batch, tpb),
        in_specs=[
            pl.BlockSpec((ts, width), row),
            pl.BlockSpec((ts, width), lambda b, t: (b * tpb + t, 1)),
            small(conv_w.shape), small((1, width)),
            small(w_a.shape), small((1, width)),
            small(w_i.shape), small((1, width)),
            small((1, width)),
        ],
        out_specs=pl.BlockSpec((ts, width), row),
        out_shape=jax.ShapeDtypeStruct((n, width), BF16),
        scratch_shapes=[
            pltpu.VMEM((ts + 2 * V7X_SUBLANES, width), F32),
            pltpu.VMEM((ts, width), F32),
            pltpu.VMEM((ts, width), F32),
            pltpu.VMEM((V7X_SUBLANES, width), F32),
        ],
        compiler_params=_params(2, nbytes),
        name="lru_scan",
    )(xg, xg, conv_w, vec(conv_b), w_a, vec(b_a), w_i, vec(b_i), vec(lam))


def _proj_res_kernel(a_ref, w_ref, x_ref, g1_ref, ng_ref, sc_ref, sh_ref, xo_ref, h_ref, *, rc):
    tm = a_ref.shape[0]
    for r in range(tm // rc):
        rows = slice(r * rc, (r + 1) * rc)
        acc = jnp.dot(a_ref[rows, :], w_ref[...], preferred_element_type=F32)
        xn = x_ref[rows, :] + g1_ref[0] * acc
        xo_ref[rows, :] = xn
        h_ref[rows, :] = _rms_mod(xn, ng_ref[...], sc_ref[0], sh_ref[0]).astype(h_ref.dtype)


def _proj_res(a, w, layer, x2, gate, norm_g, scale, shift, seq):
    n, k = a.shape
    d = w.shape[2]
    tm = _tile(seq, 512)
    rc = _tile(tm, 256)
    tpb = seq // tm
    row = lambda i: (i, 0)
    per_batch = lambda i: (i // tpb, 0, 0)
    nbytes = 2 * tm * k * 2 + k * d * 2 + 4 * tm * d * 4 + 2 * tm * d * 2 + 3 * rc * d * 4
    return pl.pallas_call(
        functools.partial(_proj_res_kernel, rc=rc),
        grid=(n // tm,),
        in_specs=[
            pl.BlockSpec((tm, k), row), _resident((k, d), layer), pl.BlockSpec((tm, d), row),
            pl.BlockSpec((1, 1, d), per_batch), pl.BlockSpec((1, d), lambda i: (0, 0)),
            pl.BlockSpec((1, 1, d), per_batch), pl.BlockSpec((1, 1, d), per_batch),
        ],
        out_specs=[pl.BlockSpec((tm, d), row), pl.BlockSpec((tm, d), row)],
        out_shape=[jax.ShapeDtypeStruct((n, d), F32), jax.ShapeDtypeStruct((n, d), BF16)],
        compiler_params=_params(1, nbytes),
        name="mix_out_proj",
    )(a, w, x2, gate, norm_g.reshape(1, d), scale, shift)


def _mlp_kernel(*refs, n_norm, rc, fc):
    h_ref, win_ref, wout_ref, x_ref, g2_ref = refs[:5]
    norm_refs = refs[5:5 + 3 * n_norm]
    xo_ref = refs[5 + 3 * n_norm]
    ho_refs = refs[6 + 3 * n_norm:6 + 4 * n_norm]
    acc_s = refs[6 + 4 * n_norm]
    k = pl.program_id(1)

    h = h_ref[...]
    tf = win_ref.shape[1]
    pieces = []
    for c in range(tf // fc):
        a = jnp.dot(h, win_ref[:, c * fc:(c + 1) * fc], preferred_element_type=F32)
        pieces.append(jnp.square(jnp.maximum(a, 0.0)).astype(BF16))
    a = pieces[0] if len(pieces) == 1 else jnp.concatenate(pieces, axis=1)

    prev = jnp.where(k > 0, acc_s[...], 0.0)
    acc_s[...] = prev + jnp.dot(a, wout_ref[...], preferred_element_type=F32)

    @pl.when(k == pl.num_programs(1) - 1)
    def _():
        tm = x_ref.shape[0]

        def rows_body(r, carry):
            rows = pl.ds(pl.multiple_of(r * rc, rc), rc)
            xn = x_ref[rows, :] + g2_ref[0] * acc_s[rows, :]
            xo_ref[rows, :] = xn
            for j in range(n_norm):
                ng, sc, sh = norm_refs[3 * j:3 * j + 3]
                ho_refs[j][rows, :] = _rms_mod(xn, ng[...], sc[0], sh[0]).astype(BF16)
            return carry

        lax.fori_loop(0, tm // rc, rows_body, 0)


def _mlp(h2, x2, w_in, w_out, layer, gate, norms, seq):
    n, d = x2.shape
    f = w_in.shape[2]
    tm = _tile(seq, 512)
    tf = _tile(f, 1024)
    fc = _tile(tf, 512)
    rc = _tile(tm, 64)
    tpb = seq // tm
    n_norm = len(norms)
    row = lambda i, k: (i, 0)
    per_batch = lambda i, k: (i // tpb, 0, 0)
    in_specs = [
        pl.BlockSpec((tm, d), row),
        pl.BlockSpec((pl.Squeezed(), d, tf), lambda i, k: (layer, 0, k)),
        pl.BlockSpec((pl.Squeezed(), tf, d), lambda i, k: (layer, k, 0)),
        pl.BlockSpec((tm, d), row),
        pl.BlockSpec((1, 1, d), per_batch),
    ]
    args = [h2, w_in, w_out, x2, gate]
    for ng, sc, sh in norms:
        in_specs += [pl.BlockSpec((1, d), lambda i, k: (0, 0)),
                     pl.BlockSpec((1, 1, d), per_batch), pl.BlockSpec((1, 1, d), per_batch)]
        args += [ng.reshape(1, d), sc, sh]
    out_specs = [pl.BlockSpec((tm, d), row)] * (1 + n_norm)
    out_shape = [jax.ShapeDtypeStruct((n, d), F32)] + [jax.ShapeDtypeStruct((n, d), BF16)] * n_norm
    nbytes = (2 * tm * d * 2 + 4 * d * tf * 2 + 4 * tm * d * 4 + tm * d * 4
              + 2 * n_norm * tm * d * 2 + 2 * tm * tf * 4 + tm * d * 4)
    outs = pl.pallas_call(
        functools.partial(_mlp_kernel, n_norm=n_norm, rc=rc, fc=fc),
        grid=(n // tm, f // tf),
        in_specs=in_specs,
        out_specs=out_specs,
        out_shape=out_shape,
        scratch_shapes=[pltpu.VMEM((tm, d), F32)],
        compiler_params=_params(2, nbytes),
        name="relu2_mlp",
    )(*args)
    return outs[0], list(outs[1:])


def _head_norm(a, g):
    ms = jnp.mean(a * a, axis=-1, keepdims=True)
    return a * lax.rsqrt(ms + NORM_EPS) * g


def _q_kernel(h_ref, w_ref, g_ref, o_ref, *, hc, scale):
    tm = h_ref.shape[0]
    n_heads, dh = o_ref.shape[1], g_ref.shape[1]
    h = h_ref[...]
    row = lax.broadcasted_iota(jnp.int32, (dh, tm), 0)
    ones_rows = jnp.where(row < N_F_PARTS, 1.0, 0.0).astype(o_ref.dtype)
    gs = g_ref[...] * scale
    for c in range(n_heads // hc):
        acc = jnp.dot(h, w_ref[:, c * hc * dh:(c + 1) * hc * dh], preferred_element_type=F32)
        for hh in range(hc):
            head = c * hc + hh
            qn = _head_norm(acc[:, hh * dh:(hh + 1) * dh], gs)
            o_ref[0, head, 0:dh, :] = qn.T.astype(o_ref.dtype)
            o_ref[0, head, dh:2 * dh, :] = ones_rows


def _q_proj(h, w, layer, g, n_batch, seq, n_heads):
    n, d = h.shape
    dh = d // n_heads
    tm = _tile(seq, 512)
    tpb = seq // tm
    hc = _tile(n_heads, 4)
    nbytes = 2 * tm * d * 2 + d * d * 2 + 2 * n_heads * tm * 2 * dh * 2 + 2 * tm * hc * dh * 4
    return pl.pallas_call(
        functools.partial(_q_kernel, hc=hc, scale=dh ** -0.5 * LOG2_E),
        grid=(n // tm,),
        in_specs=[pl.BlockSpec((tm, d), lambda i: (i, 0)), _resident((d, d), layer),
                  pl.BlockSpec((1, dh), lambda i: (0, 0))],
        out_specs=pl.BlockSpec((1, n_heads, 2 * dh, tm), lambda i: (i // tpb, 0, 0, i % tpb)),
        out_shape=jax.ShapeDtypeStruct((n_batch, n_heads, 2 * dh, seq), BF16),
        compiler_params=_params(1, nbytes),
        name="q_proj",
    )(h, w, g.reshape(1, dh))


def _kv_kernel(h_ref, w_ref, wf_ref, bf_ref, g_ref, ko_ref, vo_ref, fe_ref, carry_s, *, hc, tpb):
    tm, d = h_ref.shape
    n_heads, dh = ko_ref.shape[1], g_ref.shape[1]

    @pl.when(pl.program_id(0) % tpb == 0)
    def _():
        carry_s[...] = jnp.zeros(carry_s.shape, F32)

    h = h_ref[...]

    z = jnp.dot(h, wf_ref[...], preferred_element_type=F32) + bf_ref[...]
    log_f = _log_sigmoid(z)
    ri = lax.broadcasted_iota(jnp.int32, (tm, tm), 0)
    ci = lax.broadcasted_iota(jnp.int32, (tm, tm), 1)
    tri = jnp.where(ri >= ci, 1.0, 0.0).astype(F32)
    cum = jnp.dot(tri, log_f, preferred_element_type=F32, precision=lax.Precision.HIGHEST)
    cum = cum + carry_s[0:1, :]
    carry_s[...] = jnp.broadcast_to(cum[tm - 1:tm, :], carry_s.shape)
    fe_ref[0] = cum[tm - 1:tm, :]

    parts = []
    rest = -LOG2_E * cum
    for _ in range(N_F_PARTS):
        piece = rest.astype(BF16).astype(F32)
        parts.append(piece)
        rest = rest - piece

    lane = lax.broadcasted_iota(jnp.int32, (tm, dh), 1)
    g = g_ref[...]
    for c in range(n_heads // hc):
        acc = jnp.dot(h, w_ref[:, c * hc * dh:(c + 1) * hc * dh], preferred_element_type=F32)
        for hh in range(hc):
            head = c * hc + hh
            kn = _head_norm(acc[:, hh * dh:(hh + 1) * dh], g)
            ko_ref[0, head, :, 0:dh] = kn.astype(ko_ref.dtype)
            aug = jnp.zeros((tm, dh), F32)
            for p in range(N_F_PARTS):
                col = jnp.broadcast_to(parts[p][:, head:head + 1], (tm, dh))
                aug = jnp.where(lane == p, col, aug)
            ko_ref[0, head, :, dh:2 * dh] = aug.astype(ko_ref.dtype)
    for c in range(n_heads // hc):
        acc = jnp.dot(h, w_ref[:, d + c * hc * dh:d + (c + 1) * hc * dh], preferred_element_type=F32)
        for hh in range(hc):
            head = c * hc + hh
            vo_ref[0, head, 0] = acc[:, hh * dh:(hh + 1) * dh].T.astype(vo_ref.dtype)


def _kv_proj(h, w, w_forget, b_forget, g, n_batch, seq, n_heads, tk):
    n, d = h.shape
    dh = d // n_heads
    tm = tk
    tpb = seq // tm
    hc = _tile(n_heads, 4)
    assert n_heads <= V7X_LANES
    wf = jnp.zeros((d, V7X_LANES), BF16).at[:, :n_heads].set(w_forget.astype(BF16))
    bf = jnp.zeros((1, V7X_LANES), F32).at[0, :n_heads].set(b_forget)
    nbytes = (2 * tm * d * 2 + 2 * d * d * 2 + 2 * n_heads * tm * 3 * dh * 2
              + 2 * tm * hc * dh * 4 + 3 * tm * tm * 4)
    return pl.pallas_call(
        functools.partial(_kv_kernel, hc=hc, tpb=tpb),
        grid=(n // tm,),
        in_specs=[pl.BlockSpec((tm, d), lambda i: (i, 0)), _resident((d, 2 * d)),
                  _resident((d, V7X_LANES)), pl.BlockSpec((1, V7X_LANES), lambda i: (0, 0)),
                  pl.BlockSpec((1, dh), lambda i: (0, 0))],
        out_specs=[
            pl.BlockSpec((1, n_heads, tm, 2 * dh), lambda i: (i // tpb, 0, i % tpb, 0)),
            pl.BlockSpec((1, n_heads, 1, dh, tm), lambda i: (i // tpb, 0, i % tpb, 0, 0)),
            pl.BlockSpec((1, 1, V7X_LANES), lambda i: (i, 0, 0)),
        ],
        out_shape=[jax.ShapeDtypeStruct((n_batch, n_heads, seq, 2 * dh), BF16),
                   jax.ShapeDtypeStruct((n_batch, n_heads, seq // tk, dh, tk), BF16),
                   jax.ShapeDtypeStruct((n // tm, 1, V7X_LANES), F32)],
        scratch_shapes=[pltpu.VMEM((V7X_SUBLANES, V7X_LANES), F32)],
        compiler_params=_params(1, nbytes),
        name="kv_proj",
    )(h, w, wf, bf, g.reshape(1, dh))


def _first_key_blocks(f_end, q_gain, k_gain, dh):
    n_batch, n_kv, n_heads = f_end.shape
    nq = n_kv // 2
    qk_max = (QK_BOUND_MARGIN * LOG2_E * math.sqrt(dh)
              * jnp.max(jnp.abs(q_gain)) * jnp.max(jnp.abs(k_gain)))
    first_own = 2 * jnp.arange(nq) - 1
    ref = f_end[:, jnp.maximum(first_own, 0), :]
    gap = LOG2_E * (ref[:, :, None, :] - f_end[:, None, :, :]) + 2.0 * qk_max + 1.0
    below = jnp.arange(n_kv)[None, :] < first_own[:, None]
    needed = jnp.logical_not((gap < -F32_UNDERFLOW_LOG2) & below[None, :, :, None])
    first_block = jnp.argmax(needed, axis=2)
    return jnp.transpose(first_block, (0, 2, 1)).reshape(-1).astype(jnp.int32)


def _attn_kernel(first_ref, q_ref, qn_ref, k_ref, vt_ref, o_ref, s_s, p_s, acc_s, *, qc):
    tq = q_ref.shape[3]
    n_kv, dh, tk = vt_ref.shape[2:]
    assert tq == 2 * tk
    qi = pl.program_id(2)
    n_q = pl.num_programs(2)
    tile = (pl.program_id(0) * pl.num_programs(1) + pl.program_id(1)) * n_q + qi
    j0 = first_ref[tile]
    whole = [slice(0, tq)]
    groups = [slice(g * qc, (g + 1) * qc) for g in range(tq // qc)]
    key_minus_query = (lax.broadcasted_iota(jnp.int32, (tk, qc), 0)
                       - lax.broadcasted_iota(jnp.int32, (tk, qc), 1))

    def scores(j, cols, query_ref=q_ref):
        kblk = k_ref[0, 0, pl.ds(pl.multiple_of(j * tk, tk), tk), :]
        return jnp.dot(kblk, query_ref[0, 0, :, cols], preferred_element_type=F32)

    def softmax(s, state, j, cols, masked):
        m_old, _ = state
        if masked:
            s = jnp.where(key_minus_query <= qi * tq + cols.start - j * tk, s, NEG_INF)
        m_new = jnp.maximum(m_old, jnp.max(s, axis=0, keepdims=True))
        p = jnp.exp2(s - m_new)
        alpha = jnp.exp2(m_old - m_new)
        return p.astype(p_s.dtype), (m_new, alpha)

    ones_rows = jnp.ones((V7X_BF16_ROWS, tk), p_s.dtype)

    def add_values(p, alpha, j, cols):
        vt1 = jnp.concatenate([vt_ref[0, 0, j], ones_rows], axis=0)
        acc_s[:, cols] = alpha * acc_s[:, cols] + jnp.dot(vt1, p, preferred_element_type=F32)

    def pair(a, carry, last):
        new_carry = []
        for cols, state in zip(groups if last else whole, carry):
            mask_a = last and cols.start < tk
            skip_b = last and cols.stop <= tk
            alpha_prev = state[1]
            s_a = s_s[:, cols]
            if not skip_b:
                s_b = scores(a + 1, cols)
            p_a, state = softmax(s_a, state, a, cols, mask_a)
            add_values(p_s[:, cols], alpha_prev, jnp.maximum(a - 1, 0), cols)
            if not last:
                s_s[:, cols] = scores(a + 2, cols)
            alpha_a = state[1]
            if not skip_b:
                p_b, state = softmax(s_b, state, a + 1, cols, last)
            add_values(p_a, alpha_a, a, cols)
            if not last:
                p_s[:, cols] = p_b
            elif not skip_b:
                add_values(p_b, state[1], a + 1, cols)
            new_carry.append(state)
        return tuple(new_carry)

    def single(_, carry):
        p, state = softmax(s_s[...], carry[0], j0, whole[0], False)
        s_s[...] = scores(j0 + 1, whole[0])
        p_s[...] = p
        return (state,)

    odd = j0 % 2
    a0 = j0 + odd
    n_unmasked = qi - a0 // 2

    def two_pairs(v, carry):
        return pair(a0 + 4 * v + 2, pair(a0 + 4 * v, carry, False), False)

    @pl.when(qi == 0)
    def _():
        s_s[...] = scores(j0, whole[0])

    p_s[...] = jnp.zeros(p_s.shape, p_s.dtype)
    acc_s[...] = jnp.zeros(acc_s.shape, F32)
    carry = ((jnp.full((1, tq), NEG_INF, F32), jnp.ones((1, tq), F32)),)
    carry = lax.fori_loop(0, odd, single, carry)
    carry = lax.fori_loop(0, n_unmasked // 2, two_pairs, carry)
    carry = lax.fori_loop(0, n_unmasked % 2, lambda _, cr: pair(2 * qi - 2, cr, False), carry)
    carry = tuple(tuple(v[:, cols] for v in carry[0]) for cols in groups)
    carry = pair(2 * qi, carry, True)

    next_tile = jnp.minimum(tile + 1, pl.num_programs(0) * pl.num_programs(1) * n_q - 1)
    s_s[...] = scores(first_ref[next_tile], whole[0], qn_ref)

    for cols in groups:
        out = acc_s[0:dh, cols] * (1.0 / acc_s[dh:dh + 1, cols])
        o_ref[0, cols, :] = out.T.astype(o_ref.dtype)


def _attention(first_pairs, qp, kp, vt):
    n_batch, n_heads, dh2, seq = qp.shape
    n_kv, dh, tk = vt.shape[2:]
    tq = 2 * tk
    qc = _tile(tq, 512)
    nbytes = (2 * tq * dh2 * 2 + 2 * seq * dh2 * 2 + 2 * seq * dh * 2 + 2 * tq * dh * 2
              + dh * tq * 4 + tk * tq * 6 + 4 * tk * tq * 4)
    return pl.pallas_call(
        functools.partial(_attn_kernel, qc=qc),
        grid_spec=pltpu.PrefetchScalarGridSpec(
            num_scalar_prefetch=1,
            grid=(n_batch, n_heads, seq // tq),
            in_specs=[
                pl.BlockSpec((1, 1, dh2, tq), lambda b, h, i, first: (b, h, 0, i)),
                pl.BlockSpec((1, 1, dh2, tq),
                             lambda b, h, i, first: (b, h, 0, jnp.minimum(i + 1, seq // tq - 1))),
                pl.BlockSpec((1, 1, seq, dh2), lambda b, h, i, first: (b, h, 0, 0)),
                pl.BlockSpec((1, 1, n_kv, dh, tk), lambda b, h, i, first: (b, h, 0, 0, 0)),
            ],
            out_specs=pl.BlockSpec((1, tq, dh), lambda b, h, i, first: (b, i, h)),
            scratch_shapes=[pltpu.VMEM((tk, tq), F32), pltpu.VMEM((tk, tq), BF16),
                            pltpu.VMEM((dh + V7X_BF16_ROWS, tq), F32)],
        ),
        out_shape=jax.ShapeDtypeStruct((n_batch, seq, n_heads * dh), BF16),
        compiler_params=_params(3, nbytes),
        name="forgetting_attention",
    )(first_pairs, qp, qp, kp, vt)


def kernel(x, c, mix_norm_g, mlp_norm_g, w_mod, b_mod, w_mlp_in, w_mlp_out, lru_w_in, lru_conv_w, lru_conv_b, lru_w_a, lru_b_a, lru_w_i, lru_b_i, lru_lambda, lru_w_out, kv_norm_g, kv_w_mod, kv_b_mod, w_kv, k_norm_g, w_forget, b_forget, attn_w_q, q_norm_g, attn_w_o):
    n_batch, seq, d = x.shape
    depth = w_mod.shape[0]
    n_a = lru_w_in.shape[0]
    n_heads = w_forget.shape[1]
    assert 1 <= n_a < depth and w_mod.shape[2] == N_MOD * d
    n = n_batch * seq
    tk = _tile(seq, 512)

    mod = _modulation(c, w_mod, b_mod)
    kv_mod = _modulation(c, kv_w_mod[None], kv_b_mod[None])[0]
    per_batch = lambda a: a.reshape(n_batch, 1, d)
    sh1, sc1, g1, sh2, sc2, g2 = [[per_batch(mod[l, :, j * d:(j + 1) * d]) for l in range(depth)]
                                  for j in range(N_MOD)]
    kv_shift, kv_scale = per_batch(kv_mod[:, :d]), per_batch(kv_mod[:, d:])

    bf = lambda w: w.astype(BF16)
    w_mlp_in, w_mlp_out, lru_w_in, lru_w_out = bf(w_mlp_in), bf(w_mlp_out), bf(lru_w_in), bf(lru_w_out)
    attn_w_q, attn_w_o = bf(attn_w_q), bf(attn_w_o)
    x2 = x.reshape(n, d)
    h = _norm_mod(x2, mix_norm_g[0], sc1[0], sh1[0], seq)
    h_kv = kp = vt = f_end = None
    for layer in range(depth):
        if layer < n_a:
            a = layer
            xg = _lru_in(h, lru_w_in, a)
            mix_in = _lru_scan(xg, lru_conv_w[a], lru_conv_b[a], bf(lru_w_a[a]), lru_b_a[a],
                               bf(lru_w_i[a]), lru_b_i[a], lru_lambda[a], n_batch, seq)
            w_o, w_o_layer = lru_w_out, a
        else:
            bl = layer - n_a
            if layer == n_a:
                kp, vt, f_end = _kv_proj(h_kv, bf(w_kv), w_forget, b_forget, k_norm_g, n_batch, seq, n_heads, tk)
                f_end = f_end.reshape(n_batch, seq // tk, V7X_LANES)[:, :, :n_heads]
            qp = _q_proj(h, attn_w_q, bl, q_norm_g[bl], n_batch, seq, n_heads)
            first_blocks = _first_key_blocks(f_end, q_norm_g[bl], k_norm_g, d // n_heads)
            mix_in = _attention(first_blocks, qp, kp, vt).reshape(n, d)
            w_o, w_o_layer = attn_w_o, bl
        x2, h2 = _proj_res(mix_in, w_o, w_o_layer, x2, g1[layer], mlp_norm_g[layer], sc2[layer],
                           sh2[layer], seq)
        norms = []
        if layer + 1 < depth:
            norms.append((mix_norm_g[layer + 1], sc1[layer + 1], sh1[layer + 1]))
        if layer + 1 == n_a:
            norms.append((kv_norm_g, kv_scale, kv_shift))
        x2, hs = _mlp(h2, x2, w_mlp_in, w_mlp_out, layer, g2[layer], norms, seq)
        if hs:
            h = hs[0]
        if layer + 1 == n_a:
            h_kv = hs[1]
    return x2.reshape(n_batch, seq, d)
```

```python
import functools
import math

import jax
import jax.numpy as jnp
from jax import lax
from jax.experimental import pallas as pl
from jax.experimental.pallas import tpu as pltpu

F32 = jnp.float32
BF16 = jnp.bfloat16

NORM_EPS = 1e-6
LRU_C = 8.0
NEG_INF = -1e30
N_MOD = 6

V7X_LANES = 128
V7X_SUBLANES = 8
V7X_BF16_ROWS = 16
V7X_VMEM_BYTES = 64 * 1024 * 1024
V7X_VMEM_RESERVE = 6 * 1024 * 1024

N_F_PARTS = 3
LOG2_E = 1.4426950408889634
F32_UNDERFLOW_LOG2 = 150.0
QK_BOUND_MARGIN = 1.02
MAX_SHIFT_OVERSHOOT_LOG2 = 64.0


def _vmem_limit(nbytes):
    return int(min(V7X_VMEM_BYTES - V7X_VMEM_RESERVE, max(nbytes * 5 // 4 + (8 << 20), 32 << 20)))


def _params(n_grid, nbytes, flags=None):
    return pltpu.CompilerParams(dimension_semantics=("arbitrary",) * n_grid,
                                vmem_limit_bytes=_vmem_limit(nbytes), flags=flags)


def _tile(n, want):
    t = min(n, want)
    while n % t:
        t -= 1
    return t


def _resident(shape, layer=None):
    nd = len(shape)
    if layer is None:
        return pl.BlockSpec(shape, lambda *_: (0,) * nd, pipeline_mode=pl.Buffered(1))
    return pl.BlockSpec((pl.Squeezed(),) + tuple(shape), lambda *_: (layer,) + (0,) * nd,
                        pipeline_mode=pl.Buffered(1))


def _rms_mod(x, g, scale, shift):
    ms = jnp.mean(x * x, axis=-1, keepdims=True)
    return (x * lax.rsqrt(ms + NORM_EPS)) * (g * (1.0 + scale)) + shift


def _gelu_tanh(x):
    c = math.sqrt(2.0 / math.pi)
    return x * (0.5 * (1.0 + jnp.tanh(c * (x + 0.044715 * (x * x * x)))))


def _sigmoid(x):
    return 0.5 * jnp.tanh(0.5 * x) + 0.5


def _sqrt_nonneg(y):
    return jnp.where(y > 0.0, y * lax.rsqrt(y), 0.0)


def _log_sigmoid(x):
    return jnp.minimum(x, 0.0) - jnp.log1p(jnp.exp(-jnp.abs(x)))


def _softplus(x):
    return jnp.maximum(x, 0.0) + jnp.log1p(jnp.exp(-jnp.abs(x)))


def _mod_kernel(cb_ref, w_ref, b_ref, o_ref):
    n_batch = cb_ref.shape[0]
    tn = w_ref.shape[2]
    for b in range(n_batch):
        cv = cb_ref[b]
        cs = cv * _sigmoid(cv)
        for ch in range(tn // V7X_LANES):
            cols = slice(ch * V7X_LANES, (ch + 1) * V7X_LANES)
            s = jnp.sum(w_ref[0, :, cols] * cs, axis=0, keepdims=True)
            o_ref[0, b:b + 1, cols] = s + b_ref[0, :, cols]


def _modulation(c, w, b):
    n_l, d, m = w.shape
    n_batch = c.shape[0]
    tn = _tile(m, 1024)
    cb = jnp.broadcast_to(c[:, :, None], (n_batch, d, V7X_LANES))
    nbytes = cb.size * 4 + 2 * d * tn * 4 + 4 * n_batch * tn * 4
    return pl.pallas_call(
        _mod_kernel,
        grid=(n_l, m // tn),
        in_specs=[
            pl.BlockSpec((n_batch, d, V7X_LANES), lambda l, j: (0, 0, 0)),
            pl.BlockSpec((1, d, tn), lambda l, j: (l, 0, j)),
            pl.BlockSpec((1, 1, tn), lambda l, j: (l, 0, j)),
        ],
        out_specs=pl.BlockSpec((1, n_batch, tn), lambda l, j: (l, 0, j)),
        out_shape=jax.ShapeDtypeStruct((n_l, n_batch, m), F32),
        compiler_params=_params(2, nbytes),
        name="modulation",
    )(cb, w, b.reshape(n_l, 1, m))


def _norm_kernel(x_ref, g_ref, sc_ref, sh_ref, o_ref):
    o_ref[...] = _rms_mod(x_ref[...], g_ref[...], sc_ref[0], sh_ref[0]).astype(o_ref.dtype)


def _norm_mod(x2, g, scale, shift, seq):
    n, d = x2.shape
    tm = _tile(seq, 512)
    tpb = seq // tm
    row = lambda i: (i, 0)
    per_batch = lambda i: (i // tpb, 0, 0)
    nbytes = 2 * tm * d * 4 + 2 * tm * d * 2
    return pl.pallas_call(
        _norm_kernel,
        grid=(n // tm,),
        in_specs=[
            pl.BlockSpec((tm, d), row),
            pl.BlockSpec((1, d), lambda i: (0, 0)),
            pl.BlockSpec((1, 1, d), per_batch),
            pl.BlockSpec((1, 1, d), per_batch),
        ],
        out_specs=pl.BlockSpec((tm, d), row),
        out_shape=jax.ShapeDtypeStruct((n, d), BF16),
        compiler_params=_params(1, nbytes),
        name="norm_mod",
    )(x2, g.reshape(1, d), scale, shift)


def _lru_in_kernel(h_ref, w_ref, o_ref, *, tc):
    h = h_ref[...]
    n_cols = w_ref.shape[1]
    for c in range(n_cols // tc):
        cols = slice(c * tc, (c + 1) * tc)
        acc = jnp.dot(h, w_ref[:, cols], preferred_element_type=F32)
        if c * tc >= n_cols // 2:
            acc = _gelu_tanh(acc)
        o_ref[:, cols] = acc


def _lru_in(h, w, layer):
    n, d = h.shape
    m = w.shape[2]
    tm = _tile(n, 512)
    tc = _tile(m // 2, 512)
    nbytes = 2 * tm * d * 2 + d * m * 2 + 2 * tm * m * 4 + 2 * tm * tc * 4
    return pl.pallas_call(
        functools.partial(_lru_in_kernel, tc=tc),
        grid=(n // tm,),
        in_specs=[pl.BlockSpec((tm, d), lambda i: (i, 0)), _resident((d, m), layer)],
        out_specs=pl.BlockSpec((tm, m), lambda i: (i, 0)),
        out_shape=jax.ShapeDtypeStruct((n, m), F32),
        compiler_params=_params(1, nbytes),
        name="lru_in_proj",
    )(h, w)


def _lru_scan_kernel(xb_ref, gy_ref, cw_ref, cb_ref, wa_ref, ba_ref, wi_ref, bi_ref, lam_ref,
                     o_ref, xpad_s, a_s, u_s, h_s, *, lane_chunk):
    ts, width = xb_ref.shape
    conv_width = cw_ref.shape[0]
    n_blocks, bd, _ = wa_ref.shape
    halo = V7X_SUBLANES
    assert conv_width - 1 <= halo

    @pl.when(pl.program_id(1) == 0)
    def _():
        xpad_s[0:halo, :] = jnp.zeros((halo, width), F32)
        h_s[...] = jnp.zeros(h_s.shape, F32)

    xpad_s[halo:halo + ts, :] = xb_ref[...]

    decay = -LRU_C * _softplus(-lam_ref[...])

    for nb in range(n_blocks):
        cols = slice(nb * bd, (nb + 1) * bd)
        xc = cb_ref[:, cols] + cw_ref[conv_width - 1:conv_width, cols] * xpad_s[halo:halo + ts, cols]
        for k in range(conv_width - 1):
            back = conv_width - 1 - k
            xc = xc + cw_ref[k:k + 1, cols] * xpad_s[halo - back:halo - back + ts, cols]
        xcb = xc.astype(BF16)
        r = _sigmoid(jnp.dot(xcb, wa_ref[nb], preferred_element_type=F32) + ba_ref[:, cols])
        gi = _sigmoid(jnp.dot(xcb, wi_ref[nb], preferred_element_type=F32) + bi_ref[:, cols])
        log_a = decay[:, cols] * r
        a = jnp.exp(log_a)
        a_s[:, cols] = a
        u_s[:, cols] = _sqrt_nonneg(-jnp.tanh(log_a) * (a * a + 1.0)) * (gi * xc)

    xpad_s[0:halo, :] = xpad_s[ts:ts + halo, :]

    rows16 = V7X_BF16_ROWS
    row_id = lax.broadcasted_iota(jnp.int32, (V7X_SUBLANES, lane_chunk), 0)

    def group(g, carry):
        r0 = pl.multiple_of(g * rows16, rows16)
        for lc in range(width // lane_chunk):
            cols = slice(lc * lane_chunk, (lc + 1) * lane_chunk)
            hprev = h_s[:, cols]
            halves = []
            for half in range(rows16 // V7X_SUBLANES):
                rr = pl.ds(r0 + half * V7X_SUBLANES, V7X_SUBLANES)
                a = a_s[rr, cols]
                u = u_s[rr, cols]
                for sh in (1, 2, 4):
                    a_sh = pltpu.roll(a, sh, axis=0)
                    u_sh = pltpu.roll(u, sh, axis=0)
                    live = row_id >= sh
                    u = jnp.where(live, a * u_sh + u, u)
                    a = jnp.where(live, a * a_sh, a)
                hcur = a * hprev + u
                hprev = jnp.broadcast_to(hcur[V7X_SUBLANES - 1:V7X_SUBLANES, :], hcur.shape)
                halves.append(hcur)
            h_s[:, cols] = hprev
            h16 = jnp.concatenate(halves, axis=0)
            o_ref[pl.ds(r0, rows16), cols] = (h16 * gy_ref[pl.ds(r0, rows16), cols]).astype(o_ref.dtype)
        return carry

    lax.fori_loop(0, ts // rows16, group, 0)


def _lru_scan(xg, conv_w, conv_b, w_a, b_a, w_i, b_i, lam, n_batch, seq):
    n, two_w = xg.shape
    width = two_w // 2
    ts = _tile(seq, 256)
    tpb = seq // ts
    n_blocks, bd, _ = w_a.shape
    lane_chunk = _tile(width, 512)
    row = lambda b, t: (b * tpb + t, 0)
    vec = lambda a: a.reshape(1, width)
    small = lambda shape: pl.BlockSpec(shape, lambda b, t: (0,) * len(shape))
    nbytes = (4 * ts * width * 4 + 2 * ts * width * 2 + (3 * ts + 16) * width * 4
              + 4 * n_blocks * bd * bd * 2)
    return pl.pallas_call(
        functools.partial(_lru_scan_kernel, lane_chunk=lane_chunk),
        grid=(n_batch, tpb),
        in_specs=[
            pl.BlockSpec((ts, width), row),
            pl.BlockSpec((ts, width), lambda b, t: (b * tpb + t, 1)),
            small(conv_w.shape), small((1, width)),
            small(w_a.shape), small((1, width)),
            small(w_i.shape), small((1, width)),
            small((1, width)),
        ],
        out_specs=pl.BlockSpec((ts, width), row),
        out_shape=jax.ShapeDtypeStruct((n, width), BF16),
        scratch_shapes=[
            pltpu.VMEM((ts + 2 * V7X_SUBLANES, width), F32),
            pltpu.VMEM((ts, width), F32),
            pltpu.VMEM((ts, width), F32),
            pltpu.VMEM((V7X_SUBLANES, width), F32),
        ],
        compiler_params=_params(2, nbytes),
        name="lru_scan",
    )(xg, xg, conv_w, vec(conv_b), w_a, vec(b_a), w_i, vec(b_i), vec(lam))


def _proj_res_kernel(a_ref, w_ref, x_ref, g1_ref, ng_ref, sc_ref, sh_ref, xo_ref, h_ref, *, rc):
    tm = a_ref.shape[0]
    for r in range(tm // rc):
        rows = slice(r * rc, (r + 1) * rc)
        acc = jnp.dot(a_ref[rows, :], w_ref[...], preferred_element_type=F32)
        xn = x_ref[rows, :] + g1_ref[0] * acc
        xo_ref[rows, :] = xn
        h_ref[rows, :] = _rms_mod(xn, ng_ref[...], sc_ref[0], sh_ref[0]).astype(h_ref.dtype)


def _proj_res(a, w, layer, x2, gate, norm_g, scale, shift, seq):
    n, k = a.shape
    d = w.shape[2]
    tm = _tile(seq, 512)
    rc = _tile(tm, 256)
    tpb = seq // tm
    row = lambda i: (i, 0)
    per_batch = lambda i: (i // tpb, 0, 0)
    nbytes = 2 * tm * k * 2 + k * d * 2 + 4 * tm * d * 4 + 2 * tm * d * 2 + 3 * rc * d * 4
    return pl.pallas_call(
        functools.partial(_proj_res_kernel, rc=rc),
        grid=(n // tm,),
        in_specs=[
            pl.BlockSpec((tm, k), row), _resident((k, d), layer), pl.BlockSpec((tm, d), row),
            pl.BlockSpec((1, 1, d), per_batch), pl.BlockSpec((1, d), lambda i: (0, 0)),
            pl.BlockSpec((1, 1, d), per_batch), pl.BlockSpec((1, 1, d), per_batch),
        ],
        out_specs=[pl.BlockSpec((tm, d), row), pl.BlockSpec((tm, d), row)],
        out_shape=[jax.ShapeDtypeStruct((n, d), F32), jax.ShapeDtypeStruct((n, d), BF16)],
        compiler_params=_params(1, nbytes),
        name="mix_out_proj",
    )(a, w, x2, gate, norm_g.reshape(1, d), scale, shift)


def _mlp_kernel(*refs, n_norm, rc, fc):
    h_ref, win_ref, wout_ref, x_ref, g2_ref = refs[:5]
    norm_refs = refs[5:5 + 3 * n_norm]
    xo_ref = refs[5 + 3 * n_norm]
    ho_refs = refs[6 + 3 * n_norm:6 + 4 * n_norm]
    acc_s = refs[6 + 4 * n_norm]
    k = pl.program_id(1)

    h = h_ref[...]
    tf = win_ref.shape[1]
    pieces = []
    for c in range(tf // fc):
        a = jnp.dot(h, win_ref[:, c * fc:(c + 1) * fc], preferred_element_type=F32)
        pieces.append(jnp.square(jnp.maximum(a, 0.0)).astype(BF16))
    a = pieces[0] if len(pieces) == 1 else jnp.concatenate(pieces, axis=1)

    prev = jnp.where(k > 0, acc_s[...], 0.0)
    acc_s[...] = prev + jnp.dot(a, wout_ref[...], preferred_element_type=F32)

    @pl.when(k == pl.num_programs(1) - 1)
    def _():
        tm = x_ref.shape[0]

        def rows_body(r, carry):
            rows = pl.ds(pl.multiple_of(r * rc, rc), rc)
            xn = x_ref[rows, :] + g2_ref[0] * acc_s[rows, :]
            xo_ref[rows, :] = xn
            for j in range(n_norm):
                ng, sc, sh = norm_refs[3 * j:3 * j + 3]
                ho_refs[j][rows, :] = _rms_mod(xn, ng[...], sc[0], sh[0]).astype(BF16)
            return carry

        lax.fori_loop(0, tm // rc, rows_body, 0)


def _mlp(h2, x2, w_in, w_out, layer, gate, norms, seq):
    n, d = x2.shape
    f = w_in.shape[2]
    tm = _tile(seq, 512)
    tf = _tile(f, 1024)
    fc = _tile(tf, 512)
    rc = _tile(tm, 64)
    tpb = seq // tm
    n_norm = len(norms)
    row = lambda i, k: (i, 0)
    per_batch = lambda i, k: (i // tpb, 0, 0)
    in_specs = [
        pl.BlockSpec((tm, d), row),
        pl.BlockSpec((pl.Squeezed(), d, tf), lambda i, k: (layer, 0, k)),
        pl.BlockSpec((pl.Squeezed(), tf, d), lambda i, k: (layer, k, 0)),
        pl.BlockSpec((tm, d), row),
        pl.BlockSpec((1, 1, d), per_batch),
    ]
    args = [h2, w_in, w_out, x2, gate]
    for ng, sc, sh in norms:
        in_specs += [pl.BlockSpec((1, d), lambda i, k: (0, 0)),
                     pl.BlockSpec((1, 1, d), per_batch), pl.BlockSpec((1, 1, d), per_batch)]
        args += [ng.reshape(1, d), sc, sh]
    out_specs = [pl.BlockSpec((tm, d), row)] * (1 + n_norm)
    out_shape = [jax.ShapeDtypeStruct((n, d), F32)] + [jax.ShapeDtypeStruct((n, d), BF16)] * n_norm
    nbytes = (2 * tm * d * 2 + 4 * d * tf * 2 + 4 * tm * d * 4 + tm * d * 4
              + 2 * n_norm * tm * d * 2 + 2 * tm * tf * 4 + tm * d * 4)
    outs = pl.pallas_call(
        functools.partial(_mlp_kernel, n_norm=n_norm, rc=rc, fc=fc),
        grid=(n // tm, f // tf),
        in_specs=in_specs,
        out_specs=out_specs,
        out_shape=out_shape,
        scratch_shapes=[pltpu.VMEM((tm, d), F32)],
        compiler_params=_params(2, nbytes),
        name="relu2_mlp",
    )(*args)
    return outs[0], list(outs[1:])


def _head_norm(a, g):
    ms = jnp.mean(a * a, axis=-1, keepdims=True)
    return a * lax.rsqrt(ms + NORM_EPS) * g


def _q_kernel(h_ref, w_ref, g_ref, o_ref, *, hc, scale):
    tm = h_ref.shape[0]
    n_heads, dh = o_ref.shape[1], g_ref.shape[1]
    h = h_ref[...]
    row = lax.broadcasted_iota(jnp.int32, (dh, tm), 0)
    ones_rows = jnp.where(row < N_F_PARTS, 1.0, 0.0).astype(o_ref.dtype)
    gs = g_ref[...] * scale
    for c in range(n_heads // hc):
        acc = jnp.dot(h, w_ref[:, c * hc * dh:(c + 1) * hc * dh], preferred_element_type=F32)
        for hh in range(hc):
            head = c * hc + hh
            qn = _head_norm(acc[:, hh * dh:(hh + 1) * dh], gs)
            o_ref[0, head, 0:dh, :] = qn.T.astype(o_ref.dtype)
            o_ref[0, head, dh:2 * dh, :] = ones_rows


def _q_proj(h, w, layer, g, n_batch, seq, n_heads):
    n, d = h.shape
    dh = d // n_heads
    tm = _tile(seq, 512)
    tpb = seq // tm
    hc = _tile(n_heads, 4)
    nbytes = 2 * tm * d * 2 + d * d * 2 + 2 * n_heads * tm * 2 * dh * 2 + 2 * tm * hc * dh * 4
    return pl.pallas_call(
        functools.partial(_q_kernel, hc=hc, scale=dh ** -0.5 * LOG2_E),
        grid=(n // tm,),
        in_specs=[pl.BlockSpec((tm, d), lambda i: (i, 0)), _resident((d, d), layer),
                  pl.BlockSpec((1, dh), lambda i: (0, 0))],
        out_specs=pl.BlockSpec((1, n_heads, 2 * dh, tm), lambda i: (i // tpb, 0, 0, i % tpb)),
        out_shape=jax.ShapeDtypeStruct((n_batch, n_heads, 2 * dh, seq), BF16),
        compiler_params=_params(1, nbytes),
        name="q_proj",
    )(h, w, g.reshape(1, dh))


def _kv_kernel(h_ref, w_ref, wf_ref, bf_ref, g_ref, ko_ref, vo_ref, fe_ref, carry_s, *, hc, tpb):
    tm, d = h_ref.shape
    n_heads, dh = ko_ref.shape[1], g_ref.shape[1]

    @pl.when(pl.program_id(0) % tpb == 0)
    def _():
        carry_s[...] = jnp.zeros(carry_s.shape, F32)

    h = h_ref[...]

    z = jnp.dot(h, wf_ref[...], preferred_element_type=F32) + bf_ref[...]
    log_f = _log_sigmoid(z)
    ri = lax.broadcasted_iota(jnp.int32, (tm, tm), 0)
    ci = lax.broadcasted_iota(jnp.int32, (tm, tm), 1)
    tri = jnp.where(ri >= ci, 1.0, 0.0).astype(F32)
    cum = jnp.dot(tri, log_f, preferred_element_type=F32, precision=lax.Precision.HIGHEST)
    cum = cum + carry_s[0:1, :]
    carry_s[...] = jnp.broadcast_to(cum[tm - 1:tm, :], carry_s.shape)
    fe_ref[0] = cum[tm - 1:tm, :]

    parts = []
    rest = -LOG2_E * cum
    for _ in range(N_F_PARTS):
        piece = rest.astype(BF16).astype(F32)
        parts.append(piece)
        rest = rest - piece

    lane = lax.broadcasted_iota(jnp.int32, (tm, dh), 1)
    g = g_ref[...]
    for c in range(n_heads // hc):
        acc = jnp.dot(h, w_ref[:, c * hc * dh:(c + 1) * hc * dh], preferred_element_type=F32)
        for hh in range(hc):
            head = c * hc + hh
            kn = _head_norm(acc[:, hh * dh:(hh + 1) * dh], g)
            ko_ref[0, head, :, 0:dh] = kn.astype(ko_ref.dtype)
            aug = jnp.zeros((tm, dh), F32)
            for p in range(N_F_PARTS):
                col = jnp.broadcast_to(parts[p][:, head:head + 1], (tm, dh))
                aug = jnp.where(lane == p, col, aug)
            ko_ref[0, head, :, dh:2 * dh] = aug.astype(ko_ref.dtype)
    for c in range(n_heads // hc):
        acc = jnp.dot(h, w_ref[:, d + c * hc * dh:d + (c + 1) * hc * dh], preferred_element_type=F32)
        for hh in range(hc):
            head = c * hc + hh
            vo_ref[0, head, 0] = acc[:, hh * dh:(hh + 1) * dh].T.astype(vo_ref.dtype)


def _kv_proj(h, w, w_forget, b_forget, g, n_batch, seq, n_heads, tk):
    n, d = h.shape
    dh = d // n_heads
    tm = tk
    tpb = seq // tm
    hc = _tile(n_heads, 4)
    assert n_heads <= V7X_LANES
    wf = jnp.zeros((d, V7X_LANES), BF16).at[:, :n_heads].set(w_forget.astype(BF16))
    bf = jnp.zeros((1, V7X_LANES), F32).at[0, :n_heads].set(b_forget)
    nbytes = (2 * tm * d * 2 + 2 * d * d * 2 + 2 * n_heads * tm * 3 * dh * 2
              + 2 * tm * hc * dh * 4 + 3 * tm * tm * 4)
    return pl.pallas_call(
        functools.partial(_kv_kernel, hc=hc, tpb=tpb),
        grid=(n // tm,),
        in_specs=[pl.BlockSpec((tm, d), lambda i: (i, 0)), _resident((d, 2 * d)),
                  _resident((d, V7X_LANES)), pl.BlockSpec((1, V7X_LANES), lambda i: (0, 0)),
                  pl.BlockSpec((1, dh), lambda i: (0, 0))],
        out_specs=[
            pl.BlockSpec((1, n_heads, tm, 2 * dh), lambda i: (i // tpb, 0, i % tpb, 0)),
            pl.BlockSpec((1, n_heads, 1, dh, tm), lambda i: (i // tpb, 0, i % tpb, 0, 0)),
            pl.BlockSpec((1, 1, V7X_LANES), lambda i: (i, 0, 0)),
        ],
        out_shape=[jax.ShapeDtypeStruct((n_batch, n_heads, seq, 2 * dh), BF16),
                   jax.ShapeDtypeStruct((n_batch, n_heads, seq // tk, dh, tk), BF16),
                   jax.ShapeDtypeStruct((n // tm, 1, V7X_LANES), F32)],
        scratch_shapes=[pltpu.VMEM((V7X_SUBLANES, V7X_LANES), F32)],
        compiler_params=_params(1, nbytes),
        name="kv_proj",
    )(h, w, wf, bf, g.reshape(1, dh))


def _first_key_blocks(f_end, q_gain, k_gain, dh):
    n_batch, n_kv, n_heads = f_end.shape
    nq = n_kv // 2
    qk_max = (QK_BOUND_MARGIN * LOG2_E * math.sqrt(dh)
              * jnp.max(jnp.abs(q_gain)) * jnp.max(jnp.abs(k_gain)))
    first_own = 2 * jnp.arange(nq) - 1
    ref = f_end[:, jnp.maximum(first_own, 0), :]
    gap = LOG2_E * (ref[:, :, None, :] - f_end[:, None, :, :]) + 2.0 * qk_max + 1.0
    below = jnp.arange(n_kv)[None, :] < first_own[:, None]
    needed = jnp.logical_not((gap < -F32_UNDERFLOW_LOG2) & below[None, :, :, None])
    first_block = jnp.argmax(needed, axis=2)
    first_block = jnp.transpose(first_block, (0, 2, 1)).reshape(-1).astype(jnp.int32)

    k_bound = QK_BOUND_MARGIN * math.sqrt(dh) * jnp.max(jnp.abs(k_gain))
    use_bound = (2.0 * qk_max <= MAX_SHIFT_OVERSHOOT_LOG2).astype(F32)
    block_term = jnp.transpose(-LOG2_E * f_end + 0.05, (0, 2, 1)).reshape(-1)
    bounds = jnp.concatenate([block_term, jnp.stack([k_bound, use_bound])]).astype(F32)
    return first_block, bounds


def _attn_kernel(first_ref, bnd_ref, q_ref, qn_ref, k_ref, vt_ref, o_ref, s_s, p_s, acc_s, *, qc):
    tq = q_ref.shape[3]
    n_kv, dh, tk = vt_ref.shape[2:]
    assert tq == 2 * tk
    qi = pl.program_id(2)
    n_q = pl.num_programs(2)
    tile = (pl.program_id(0) * pl.num_programs(1) + pl.program_id(1)) * n_q + qi
    j0 = first_ref[tile]
    whole = [slice(0, tq)]
    groups = [slice(g * qc, (g + 1) * qc) for g in range(tq // qc)]
    key_minus_query = (lax.broadcasted_iota(jnp.int32, (tk, qc), 0)
                       - lax.broadcasted_iota(jnp.int32, (tk, qc), 1))

    def scores(j, cols, query_ref=q_ref):
        kblk = k_ref[0, 0, pl.ds(pl.multiple_of(j * tk, tk), tk), :]
        return jnp.dot(kblk, query_ref[0, 0, :, cols], preferred_element_type=F32)

    head = pl.program_id(0) * pl.num_programs(1) + pl.program_id(1)
    n_bnd = pl.num_programs(0) * pl.num_programs(1) * n_kv
    use_bound = bnd_ref[n_bnd + 1] > 0.5
    q_sq = jnp.square(q_ref[0, 0, 0:dh, :].astype(F32))
    q_reach = jnp.sqrt(jnp.sum(q_sq, axis=0, keepdims=True)) * bnd_ref[n_bnd]

    def softmax(s, state, j, cols, masked, bounded=False):
        m_old, _ = state
        if masked:
            s = jnp.where(key_minus_query <= qi * tq + cols.start - j * tk, s, NEG_INF)
        if bounded:
            m_blk = q_reach[:, cols] + bnd_ref[head * n_kv + j]
        else:
            m_blk = jnp.max(s, axis=0, keepdims=True)
        m_new = jnp.maximum(m_old, m_blk)
        p = jnp.exp2(s - m_new)
        alpha = jnp.exp2(m_old - m_new)
        return p.astype(p_s.dtype), (m_new, alpha)

    ones_rows = jnp.ones((V7X_BF16_ROWS, tk), p_s.dtype)

    def add_values(p, alpha, j, cols):
        vt1 = jnp.concatenate([vt_ref[0, 0, j], ones_rows], axis=0)
        acc_s[:, cols] = alpha * acc_s[:, cols] + jnp.dot(vt1, p, preferred_element_type=F32)

    def pair(a, carry, last, bounded=False):
        new_carry = []
        for cols, state in zip(groups if last else whole, carry):
            mask_a = last and cols.start < tk
            skip_b = last and cols.stop <= tk
            alpha_prev = state[1]
            s_a = s_s[:, cols]
            if not skip_b:
                s_b = scores(a + 1, cols)
            p_a, state = softmax(s_a, state, a, cols, mask_a, bounded)
            add_values(p_s[:, cols], alpha_prev, jnp.maximum(a - 1, 0), cols)
            if not last:
                s_s[:, cols] = scores(a + 2, cols)
            alpha_a = state[1]
            if not skip_b:
                p_b, state = softmax(s_b, state, a + 1, cols, last, bounded)
            add_values(p_a, alpha_a, a, cols)
            if not last:
                p_s[:, cols] = p_b
            elif not skip_b:
                add_values(p_b, state[1], a + 1, cols)
            new_carry.append(state)
        return tuple(new_carry)

    odd = j0 % 2
    a0 = j0 + odd
    n_unmasked = qi - a0 // 2

    def unmasked_blocks(carry, bounded):
        def single(_, cr):
            p, state = softmax(s_s[...], cr[0], j0, whole[0], False, bounded)
            s_s[...] = scores(j0 + 1, whole[0])
            p_s[...] = p
            return (state,)

        def two_pairs(v, cr):
            return pair(a0 + 4 * v + 2, pair(a0 + 4 * v, cr, False, bounded), False, bounded)

        carry = lax.fori_loop(0, odd, single, carry)
        carry = lax.fori_loop(0, n_unmasked // 2, two_pairs, carry)
        return lax.fori_loop(0, n_unmasked % 2,
                             lambda _, cr: pair(2 * qi - 2, cr, False, bounded), carry)

    @pl.when(qi == 0)
    def _():
        s_s[...] = scores(j0, whole[0])

    p_s[...] = jnp.zeros(p_s.shape, p_s.dtype)
    acc_s[...] = jnp.zeros(acc_s.shape, F32)
    carry = ((jnp.full((1, tq), NEG_INF, F32), jnp.ones((1, tq), F32)),)
    carry = lax.cond(use_bound, lambda cr: unmasked_blocks(cr, True),
                     lambda cr: unmasked_blocks(cr, False), carry)
    carry = tuple(tuple(v[:, cols] for v in carry[0]) for cols in groups)
    carry = pair(2 * qi, carry, True)

    next_tile = jnp.minimum(tile + 1, pl.num_programs(0) * pl.num_programs(1) * n_q - 1)
    s_s[...] = scores(first_ref[next_tile], whole[0], qn_ref)

    for cols in groups:
        out = acc_s[0:dh, cols] * (1.0 / acc_s[dh:dh + 1, cols])
        o_ref[0, cols, :] = out.T.astype(o_ref.dtype)


def _attention(first_blocks, bounds, qp, kp, vt):
    n_batch, n_heads, dh2, seq = qp.shape
    n_kv, dh, tk = vt.shape[2:]
    tq = 2 * tk
    qc = _tile(tq, 512)
    nbytes = (2 * tq * dh2 * 2 + 2 * seq * dh2 * 2 + 2 * seq * dh * 2 + 2 * tq * dh * 2
              + dh * tq * 4 + tk * tq * 6 + 4 * tk * tq * 4)
    return pl.pallas_call(
        functools.partial(_attn_kernel, qc=qc),
        grid_spec=pltpu.PrefetchScalarGridSpec(
            num_scalar_prefetch=1,
            grid=(n_batch, n_heads, seq // tq),
            in_specs=[
                pl.BlockSpec(memory_space=pltpu.SMEM),
                pl.BlockSpec((1, 1, dh2, tq), lambda b, h, i, first: (b, h, 0, i)),
                pl.BlockSpec((1, 1, dh2, tq),
                             lambda b, h, i, first: (b, h, 0, jnp.minimum(i + 1, seq // tq - 1))),
                pl.BlockSpec((1, 1, seq, dh2), lambda b, h, i, first: (b, h, 0, 0)),
                pl.BlockSpec((1, 1, n_kv, dh, tk), lambda b, h, i, first: (b, h, 0, 0, 0)),
            ],
            out_specs=pl.BlockSpec((1, tq, dh), lambda b, h, i, first: (b, i, h)),
            scratch_shapes=[pltpu.VMEM((tk, tq), F32), pltpu.VMEM((tk, tq), BF16),
                            pltpu.VMEM((dh + V7X_BF16_ROWS, tq), F32)],
        ),
        out_shape=jax.ShapeDtypeStruct((n_batch, seq, n_heads * dh), BF16),
        compiler_params=_params(3, nbytes),
        name="forgetting_attention",
    )(first_blocks, bounds, qp, qp, kp, vt)


def kernel(x, c, mix_norm_g, mlp_norm_g, w_mod, b_mod, w_mlp_in, w_mlp_out, lru_w_in, lru_conv_w, lru_conv_b, lru_w_a, lru_b_a, lru_w_i, lru_b_i, lru_lambda, lru_w_out, kv_norm_g, kv_w_mod, kv_b_mod, w_kv, k_norm_g, w_forget, b_forget, attn_w_q, q_norm_g, attn_w_o):
    n_batch, seq, d = x.shape
    depth = w_mod.shape[0]
    n_a = lru_w_in.shape[0]
    n_heads = w_forget.shape[1]
    assert 1 <= n_a < depth and w_mod.shape[2] == N_MOD * d
    n = n_batch * seq
    tk = _tile(seq, 512)

    mod = _modulation(c, w_mod, b_mod)
    kv_mod = _modulation(c, kv_w_mod[None], kv_b_mod[None])[0]
    per_batch = lambda a: a.reshape(n_batch, 1, d)
    sh1, sc1, g1, sh2, sc2, g2 = [[per_batch(mod[l, :, j * d:(j + 1) * d]) for l in range(depth)]
                                  for j in range(N_MOD)]
    kv_shift, kv_scale = per_batch(kv_mod[:, :d]), per_batch(kv_mod[:, d:])

    bf = lambda w: w.astype(BF16)
    w_mlp_in, w_mlp_out, lru_w_in, lru_w_out = bf(w_mlp_in), bf(w_mlp_out), bf(lru_w_in), bf(lru_w_out)
    attn_w_q, attn_w_o = bf(attn_w_q), bf(attn_w_o)
    x2 = x.reshape(n, d)
    h = _norm_mod(x2, mix_norm_g[0], sc1[0], sh1[0], seq)
    h_kv = kp = vt = f_end = None
    for layer in range(depth):
        if layer < n_a:
            a = layer
            xg = _lru_in(h, lru_w_in, a)
            mix_in = _lru_scan(xg, lru_conv_w[a], lru_conv_b[a], bf(lru_w_a[a]), lru_b_a[a],
                               bf(lru_w_i[a]), lru_b_i[a], lru_lambda[a], n_batch, seq)
            w_o, w_o_layer = lru_w_out, a
        else:
            bl = layer - n_a
            if layer == n_a:
                kp, vt, f_end = _kv_proj(h_kv, bf(w_kv), w_forget, b_forget, k_norm_g, n_batch, seq, n_heads, tk)
                f_end = f_end.reshape(n_batch, seq // tk, V7X_LANES)[:, :, :n_heads]
            qp = _q_proj(h, attn_w_q, bl, q_norm_g[bl], n_batch, seq, n_heads)
            first_blocks, bounds = _first_key_blocks(f_end, q_norm_g[bl], k_norm_g, d // n_heads)
            mix_in = _attention(first_blocks, bounds, qp, kp, vt).reshape(n, d)
            w_o, w_o_layer = attn_w_o, bl
        x2, h2 = _proj_res(mix_in, w_o, w_o_layer, x2, g1[layer], mlp_norm_g[layer], sc2[layer],
                           sh2[layer], seq)
        norms = []
        if layer + 1 < depth:
            norms.append((mix_norm_g[layer + 1], sc1[layer + 1], sh1[layer + 1]))
        if layer + 1 == n_a:
            norms.append((kv_norm_g, kv_scale, kv_shift))
        x2, hs = _mlp(h2, x2, w_mlp_in, w_mlp_out, layer, g2[layer], norms, seq)
        if hs:
            h = hs[0]
        if layer + 1 == n_a:
            h_kv = hs[1]
    return x2.reshape(n_batch, seq, d)
```

```python
import functools
import math

import jax
import jax.numpy as jnp
from jax import lax
from jax.experimental import pallas as pl
from jax.experimental.pallas import tpu as pltpu

F32 = jnp.float32
BF16 = jnp.bfloat16

NORM_EPS = 1e-6
LRU_C = 8.0
NEG_INF = -1e30
N_MOD = 6

V7X_LANES = 128
V7X_SUBLANES = 8
V7X_BF16_ROWS = 16
V7X_VMEM_BYTES = 64 * 1024 * 1024
V7X_VMEM_RESERVE = 6 * 1024 * 1024

N_F_PARTS = 3
LOG2_E = 1.4426950408889634
F32_UNDERFLOW_LOG2 = 150.0
QK_BOUND_MARGIN = 1.02
F_BOUND_SLACK = 0.05
MAX_SHIFT_OVERSHOOT_LOG2 = 64.0


def _vmem_limit(nbytes):
    return int(min(V7X_VMEM_BYTES - V7X_VMEM_RESERVE, max(nbytes * 5 // 4 + (8 << 20), 32 << 20)))


def _params(n_grid, nbytes, flags=None):
    return pltpu.CompilerParams(dimension_semantics=("arbitrary",) * n_grid,
                                vmem_limit_bytes=_vmem_limit(nbytes), flags=flags)


def _tile(n, want):
    t = min(n, want)
    while n % t:
        t -= 1
    return t


def _resident(shape, layer=None):
    nd = len(shape)
    if layer is None:
        return pl.BlockSpec(shape, lambda *_: (0,) * nd, pipeline_mode=pl.Buffered(1))
    return pl.BlockSpec((pl.Squeezed(),) + tuple(shape), lambda *_: (layer,) + (0,) * nd,
                        pipeline_mode=pl.Buffered(1))


def _rms_mod(x, g, scale, shift):
    ms = jnp.mean(x * x, axis=-1, keepdims=True)
    return (x * lax.rsqrt(ms + NORM_EPS)) * (g * (1.0 + scale)) + shift


def _gelu_tanh(x):
    c = math.sqrt(2.0 / math.pi)
    return x * (0.5 * (1.0 + jnp.tanh(c * (x + 0.044715 * (x * x * x)))))


def _sigmoid(x):
    return 0.5 * jnp.tanh(0.5 * x) + 0.5


def _sqrt_nonneg(y):
    return jnp.where(y > 0.0, y * lax.rsqrt(y), 0.0)


def _log_sigmoid(x):
    return jnp.minimum(x, 0.0) - jnp.log1p(jnp.exp(-jnp.abs(x)))


def _softplus(x):
    return jnp.maximum(x, 0.0) + jnp.log1p(jnp.exp(-jnp.abs(x)))


def _mod_kernel(cb_ref, w_ref, b_ref, o_ref):
    n_batch = cb_ref.shape[0]
    tn = w_ref.shape[2]
    for b in range(n_batch):
        cv = cb_ref[b]
        cs = cv * _sigmoid(cv)
        for ch in range(tn // V7X_LANES):
            cols = slice(ch * V7X_LANES, (ch + 1) * V7X_LANES)
            s = jnp.sum(w_ref[0, :, cols] * cs, axis=0, keepdims=True)
            o_ref[0, b:b + 1, cols] = s + b_ref[0, :, cols]


def _modulation(c, w, b):
    n_l, d, m = w.shape
    n_batch = c.shape[0]
    tn = _tile(m, 1024)
    cb = jnp.broadcast_to(c[:, :, None], (n_batch, d, V7X_LANES))
    nbytes = cb.size * 4 + 2 * d * tn * 4 + 4 * n_batch * tn * 4
    return pl.pallas_call(
        _mod_kernel,
        grid=(n_l, m // tn),
        in_specs=[
            pl.BlockSpec((n_batch, d, V7X_LANES), lambda l, j: (0, 0, 0)),
            pl.BlockSpec((1, d, tn), lambda l, j: (l, 0, j)),
            pl.BlockSpec((1, 1, tn), lambda l, j: (l, 0, j)),
        ],
        out_specs=pl.BlockSpec((1, n_batch, tn), lambda l, j: (l, 0, j)),
        out_shape=jax.ShapeDtypeStruct((n_l, n_batch, m), F32),
        compiler_params=_params(2, nbytes),
        name="modulation",
    )(cb, w, b.reshape(n_l, 1, m))


def _norm_kernel(x_ref, g_ref, sc_ref, sh_ref, o_ref):
    o_ref[...] = _rms_mod(x_ref[...], g_ref[...], sc_ref[0], sh_ref[0]).astype(o_ref.dtype)


def _norm_mod(x2, g, scale, shift, seq):
    n, d = x2.shape
    tm = _tile(seq, 512)
    tpb = seq // tm
    row = lambda i: (i, 0)
    per_batch = lambda i: (i // tpb, 0, 0)
    nbytes = 2 * tm * d * 4 + 2 * tm * d * 2
    return pl.pallas_call(
        _norm_kernel,
        grid=(n // tm,),
        in_specs=[
            pl.BlockSpec((tm, d), row),
            pl.BlockSpec((1, d), lambda i: (0, 0)),
            pl.BlockSpec((1, 1, d), per_batch),
            pl.BlockSpec((1, 1, d), per_batch),
        ],
        out_specs=pl.BlockSpec((tm, d), row),
        out_shape=jax.ShapeDtypeStruct((n, d), BF16),
        compiler_params=_params(1, nbytes),
        name="norm_mod",
    )(x2, g.reshape(1, d), scale, shift)


def _lru_in_kernel(h_ref, w_ref, o_ref, *, tc):
    h = h_ref[...]
    n_cols = w_ref.shape[1]
    for c in range(n_cols // tc):
        cols = slice(c * tc, (c + 1) * tc)
        acc = jnp.dot(h, w_ref[:, cols], preferred_element_type=F32)
        if c * tc >= n_cols // 2:
            acc = _gelu_tanh(acc)
        o_ref[:, cols] = acc


def _lru_in(h, w, layer):
    n, d = h.shape
    m = w.shape[2]
    tm = _tile(n, 512)
    tc = _tile(m // 2, 512)
    nbytes = 2 * tm * d * 2 + d * m * 2 + 2 * tm * m * 4 + 2 * tm * tc * 4
    return pl.pallas_call(
        functools.partial(_lru_in_kernel, tc=tc),
        grid=(n // tm,),
        in_specs=[pl.BlockSpec((tm, d), lambda i: (i, 0)), _resident((d, m), layer)],
        out_specs=pl.BlockSpec((tm, m), lambda i: (i, 0)),
        out_shape=jax.ShapeDtypeStruct((n, m), F32),
        compiler_params=_params(1, nbytes),
        name="lru_in_proj",
    )(h, w)


def _lru_scan_kernel(xb_ref, gy_ref, cw_ref, cb_ref, wa_ref, ba_ref, wi_ref, bi_ref, lam_ref,
                     o_ref, xpad_s, a_s, u_s, h_s, *, lane_chunk):
    ts, width = xb_ref.shape
    conv_width = cw_ref.shape[0]
    n_blocks, bd, _ = wa_ref.shape
    halo = V7X_SUBLANES
    assert conv_width - 1 <= halo

    @pl.when(pl.program_id(1) == 0)
    def _():
        xpad_s[0:halo, :] = jnp.zeros((halo, width), F32)
        h_s[...] = jnp.zeros(h_s.shape, F32)

    xpad_s[halo:halo + ts, :] = xb_ref[...]

    decay = -LRU_C * _softplus(-lam_ref[...])

    for nb in range(n_blocks):
        cols = slice(nb * bd, (nb + 1) * bd)
        xc = cb_ref[:, cols] + cw_ref[conv_width - 1:conv_width, cols] * xpad_s[halo:halo + ts, cols]
        for k in range(conv_width - 1):
            back = conv_width - 1 - k
            xc = xc + cw_ref[k:k + 1, cols] * xpad_s[halo - back:halo - back + ts, cols]
        xcb = xc.astype(BF16)
        r = _sigmoid(jnp.dot(xcb, wa_ref[nb], preferred_element_type=F32) + ba_ref[:, cols])
        gi = _sigmoid(jnp.dot(xcb, wi_ref[nb], preferred_element_type=F32) + bi_ref[:, cols])
        log_a = decay[:, cols] * r
        a = jnp.exp(log_a)
        a_s[:, cols] = a
        u_s[:, cols] = _sqrt_nonneg(-jnp.tanh(log_a) * (a * a + 1.0)) * (gi * xc)

    xpad_s[0:halo, :] = xpad_s[ts:ts + halo, :]

    rows16 = V7X_BF16_ROWS
    row_id = lax.broadcasted_iota(jnp.int32, (V7X_SUBLANES, lane_chunk), 0)

    def group(g, carry):
        r0 = pl.multiple_of(g * rows16, rows16)
        for lc in range(width // lane_chunk):
            cols = slice(lc * lane_chunk, (lc + 1) * lane_chunk)
            hprev = h_s[:, cols]
            halves = []
            for half in range(rows16 // V7X_SUBLANES):
                rr = pl.ds(r0 + half * V7X_SUBLANES, V7X_SUBLANES)
                a = a_s[rr, cols]
                u = u_s[rr, cols]
                for sh in (1, 2, 4):
                    a_sh = pltpu.roll(a, sh, axis=0)
                    u_sh = pltpu.roll(u, sh, axis=0)
                    live = row_id >= sh
                    u = jnp.where(live, a * u_sh + u, u)
                    a = jnp.where(live, a * a_sh, a)
                hcur = a * hprev + u
                hprev = jnp.broadcast_to(hcur[V7X_SUBLANES - 1:V7X_SUBLANES, :], hcur.shape)
                halves.append(hcur)
            h_s[:, cols] = hprev
            h16 = jnp.concatenate(halves, axis=0)
            o_ref[pl.ds(r0, rows16), cols] = (h16 * gy_ref[pl.ds(r0, rows16), cols]).astype(o_ref.dtype)
        return carry

    lax.fori_loop(0, ts // rows16, group, 0)


def _lru_scan(xg, conv_w, conv_b, w_a, b_a, w_i, b_i, lam, n_batch, seq):
    n, two_w = xg.shape
    width = two_w // 2
    ts = _tile(seq, 256)
    tpb = seq // ts
    n_blocks, bd, _ = w_a.shape
    lane_chunk = _tile(width, 512)
    row = lambda b, t: (b * tpb + t, 0)
    vec = lambda a: a.reshape(1, width)
    small = lambda shape: pl.BlockSpec(shape, lambda b, t: (0,) * len(shape))
    nbytes = (4 * ts * width * 4 + 2 * ts * width * 2 + (3 * ts + 16) * width * 4
              + 4 * n_blocks * bd * bd * 2)
    return pl.pallas_call(
        functools.partial(_lru_scan_kernel, lane_chunk=lane_chunk),
        grid=(n_batch, tpb),
        in_specs=[
            pl.BlockSpec((ts, width), row),
            pl.BlockSpec((ts, width), lambda b, t: (b * tpb + t, 1)),
            small(conv_w.shape), small((1, width)),
            small(w_a.shape), small((1, width)),
            small(w_i.shape), small((1, width)),
            small((1, width)),
        ],
        out_specs=pl.BlockSpec((ts, width), row),
        out_shape=jax.ShapeDtypeStruct((n, width), BF16),
        scratch_shapes=[
            pltpu.VMEM((ts + 2 * V7X_SUBLANES, width), F32),
            pltpu.VMEM((ts, width), F32),
            pltpu.VMEM((ts, width), F32),
            pltpu.VMEM((V7X_SUBLANES, width), F32),
        ],
        compiler_params=_params(2, nbytes),
        name="lru_scan",
    )(xg, xg, conv_w, vec(conv_b), w_a, vec(b_a), w_i, vec(b_i), vec(lam))


def _proj_res_kernel(a_ref, w_ref, x_ref, g1_ref, ng_ref, sc_ref, sh_ref, xo_ref, h_ref, *, rc):
    tm = a_ref.shape[0]
    for r in range(tm // rc):
        rows = slice(r * rc, (r + 1) * rc)
        acc = jnp.dot(a_ref[rows, :], w_ref[...], preferred_element_type=F32)
        xn = x_ref[rows, :] + g1_ref[0] * acc
        xo_ref[rows, :] = xn
        h_ref[rows, :] = _rms_mod(xn, ng_ref[...], sc_ref[0], sh_ref[0]).astype(h_ref.dtype)


def _proj_res(a, w, layer, x2, gate, norm_g, scale, shift, seq):
    n, k = a.shape
    d = w.shape[2]
    tm = _tile(seq, 512)
    rc = _tile(tm, 256)
    tpb = seq // tm
    row = lambda i: (i, 0)
    per_batch = lambda i: (i // tpb, 0, 0)
    nbytes = 2 * tm * k * 2 + k * d * 2 + 4 * tm * d * 4 + 2 * tm * d * 2 + 3 * rc * d * 4
    return pl.pallas_call(
        functools.partial(_proj_res_kernel, rc=rc),
        grid=(n // tm,),
        in_specs=[
            pl.BlockSpec((tm, k), row), _resident((k, d), layer), pl.BlockSpec((tm, d), row),
            pl.BlockSpec((1, 1, d), per_batch), pl.BlockSpec((1, d), lambda i: (0, 0)),
            pl.BlockSpec((1, 1, d), per_batch), pl.BlockSpec((1, 1, d), per_batch),
        ],
        out_specs=[pl.BlockSpec((tm, d), row), pl.BlockSpec((tm, d), row)],
        out_shape=[jax.ShapeDtypeStruct((n, d), F32), jax.ShapeDtypeStruct((n, d), BF16)],
        compiler_params=_params(1, nbytes),
        name="mix_out_proj",
    )(a, w, x2, gate, norm_g.reshape(1, d), scale, shift)


def _mlp_kernel(*refs, n_norm, rc, fc):
    h_ref, win_ref, wout_ref, x_ref, g2_ref = refs[:5]
    norm_refs = refs[5:5 + 3 * n_norm]
    xo_ref = refs[5 + 3 * n_norm]
    ho_refs = refs[6 + 3 * n_norm:6 + 4 * n_norm]
    acc_s = refs[6 + 4 * n_norm]
    k = pl.program_id(1)

    h = h_ref[...]
    tf = win_ref.shape[1]
    pieces = []
    for c in range(tf // fc):
        a = jnp.dot(h, win_ref[:, c * fc:(c + 1) * fc], preferred_element_type=F32)
        pieces.append(jnp.square(jnp.maximum(a, 0.0)).astype(BF16))
    a = pieces[0] if len(pieces) == 1 else jnp.concatenate(pieces, axis=1)

    prev = jnp.where(k > 0, acc_s[...], 0.0)
    acc_s[...] = prev + jnp.dot(a, wout_ref[...], preferred_element_type=F32)

    @pl.when(k == pl.num_programs(1) - 1)
    def _():
        tm = x_ref.shape[0]

        def rows_body(r, carry):
            rows = pl.ds(pl.multiple_of(r * rc, rc), rc)
            xn = x_ref[rows, :] + g2_ref[0] * acc_s[rows, :]
            xo_ref[rows, :] = xn
            for j in range(n_norm):
                ng, sc, sh = norm_refs[3 * j:3 * j + 3]
                ho_refs[j][rows, :] = _rms_mod(xn, ng[...], sc[0], sh[0]).astype(BF16)
            return carry

        lax.fori_loop(0, tm // rc, rows_body, 0)


def _mlp(h2, x2, w_in, w_out, layer, gate, norms, seq):
    n, d = x2.shape
    f = w_in.shape[2]
    tm = _tile(seq, 512)
    tf = _tile(f, 1024)
    fc = _tile(tf, 512)
    rc = _tile(tm, 64)
    tpb = seq // tm
    n_norm = len(norms)
    row = lambda i, k: (i, 0)
    per_batch = lambda i, k: (i // tpb, 0, 0)
    in_specs = [
        pl.BlockSpec((tm, d), row),
        pl.BlockSpec((pl.Squeezed(), d, tf), lambda i, k: (layer, 0, k)),
        pl.BlockSpec((pl.Squeezed(), tf, d), lambda i, k: (layer, k, 0)),
        pl.BlockSpec((tm, d), row),
        pl.BlockSpec((1, 1, d), per_batch),
    ]
    args = [h2, w_in, w_out, x2, gate]
    for ng, sc, sh in norms:
        in_specs += [pl.BlockSpec((1, d), lambda i, k: (0, 0)),
                     pl.BlockSpec((1, 1, d), per_batch), pl.BlockSpec((1, 1, d), per_batch)]
        args += [ng.reshape(1, d), sc, sh]
    out_specs = [pl.BlockSpec((tm, d), row)] * (1 + n_norm)
    out_shape = [jax.ShapeDtypeStruct((n, d), F32)] + [jax.ShapeDtypeStruct((n, d), BF16)] * n_norm
    nbytes = (2 * tm * d * 2 + 4 * d * tf * 2 + 4 * tm * d * 4 + tm * d * 4
              + 2 * n_norm * tm * d * 2 + 2 * tm * tf * 4 + tm * d * 4)
    outs = pl.pallas_call(
        functools.partial(_mlp_kernel, n_norm=n_norm, rc=rc, fc=fc),
        grid=(n // tm, f // tf),
        in_specs=in_specs,
        out_specs=out_specs,
        out_shape=out_shape,
        scratch_shapes=[pltpu.VMEM((tm, d), F32)],
        compiler_params=_params(2, nbytes),
        name="relu2_mlp",
    )(*args)
    return outs[0], list(outs[1:])


def _head_norm(a, g):
    ms = jnp.mean(a * a, axis=-1, keepdims=True)
    return a * lax.rsqrt(ms + NORM_EPS) * g


def _q_kernel(h_ref, w_ref, g_ref, o_ref, *, hc, scale):
    tm = h_ref.shape[0]
    n_heads, dh = o_ref.shape[1], g_ref.shape[1]
    h = h_ref[...]
    row = lax.broadcasted_iota(jnp.int32, (dh, tm), 0)
    ones_rows = jnp.where(row < N_F_PARTS, 1.0, 0.0).astype(o_ref.dtype)
    gs = g_ref[...] * scale
    for c in range(n_heads // hc):
        acc = jnp.dot(h, w_ref[:, c * hc * dh:(c + 1) * hc * dh], preferred_element_type=F32)
        for hh in range(hc):
            head = c * hc + hh
            qn = _head_norm(acc[:, hh * dh:(hh + 1) * dh], gs)
            o_ref[0, head, 0:dh, :] = qn.T.astype(o_ref.dtype)
            o_ref[0, head, dh:2 * dh, :] = ones_rows


def _q_proj(h, w, layer, g, n_batch, seq, n_heads):
    n, d = h.shape
    dh = d // n_heads
    tm = _tile(seq, 512)
    tpb = seq // tm
    hc = _tile(n_heads, 4)
    nbytes = 2 * tm * d * 2 + d * d * 2 + 2 * n_heads * tm * 2 * dh * 2 + 2 * tm * hc * dh * 4
    return pl.pallas_call(
        functools.partial(_q_kernel, hc=hc, scale=dh ** -0.5 * LOG2_E),
        grid=(n // tm,),
        in_specs=[pl.BlockSpec((tm, d), lambda i: (i, 0)), _resident((d, d), layer),
                  pl.BlockSpec((1, dh), lambda i: (0, 0))],
        out_specs=pl.BlockSpec((1, n_heads, 2 * dh, tm), lambda i: (i // tpb, 0, 0, i % tpb)),
        out_shape=jax.ShapeDtypeStruct((n_batch, n_heads, 2 * dh, seq), BF16),
        compiler_params=_params(1, nbytes),
        name="q_proj",
    )(h, w, g.reshape(1, dh))


def _kv_kernel(h_ref, w_ref, wf_ref, bf_ref, g_ref, ko_ref, vo_ref, fe_ref, fq_ref, carry_s, *, hc, tpb):
    tm, d = h_ref.shape
    n_heads, dh = ko_ref.shape[1], g_ref.shape[1]

    @pl.when(pl.program_id(0) % tpb == 0)
    def _():
        carry_s[...] = jnp.zeros(carry_s.shape, F32)

    h = h_ref[...]

    z = jnp.dot(h, wf_ref[...], preferred_element_type=F32) + bf_ref[...]
    log_f = _log_sigmoid(z)
    ri = lax.broadcasted_iota(jnp.int32, (tm, tm), 0)
    ci = lax.broadcasted_iota(jnp.int32, (tm, tm), 1)
    tri = jnp.where(ri >= ci, 1.0, 0.0).astype(F32)
    cum = jnp.dot(tri, log_f, preferred_element_type=F32, precision=lax.Precision.HIGHEST)
    cum = cum + carry_s[0:1, :]
    carry_s[...] = jnp.broadcast_to(cum[tm - 1:tm, :], carry_s.shape)
    fe_ref[0] = cum[tm - 1:tm, :]

    parts = []
    rest = -LOG2_E * cum
    fq_ref[0] = rest.T
    for _ in range(N_F_PARTS):
        piece = rest.astype(BF16).astype(F32)
        parts.append(piece)
        rest = rest - piece

    lane = lax.broadcasted_iota(jnp.int32, (tm, dh), 1)
    g = g_ref[...]
    for c in range(n_heads // hc):
        acc = jnp.dot(h, w_ref[:, c * hc * dh:(c + 1) * hc * dh], preferred_element_type=F32)
        for hh in range(hc):
            head = c * hc + hh
            kn = _head_norm(acc[:, hh * dh:(hh + 1) * dh], g)
            ko_ref[0, head, :, 0:dh] = kn.astype(ko_ref.dtype)
            aug = jnp.zeros((tm, dh), F32)
            for p in range(N_F_PARTS):
                col = jnp.broadcast_to(parts[p][:, head:head + 1], (tm, dh))
                aug = jnp.where(lane == p, col, aug)
            ko_ref[0, head, :, dh:2 * dh] = aug.astype(ko_ref.dtype)
    for c in range(n_heads // hc):
        acc = jnp.dot(h, w_ref[:, d + c * hc * dh:d + (c + 1) * hc * dh], preferred_element_type=F32)
        for hh in range(hc):
            head = c * hc + hh
            vo_ref[0, head, 0] = acc[:, hh * dh:(hh + 1) * dh].T.astype(vo_ref.dtype)


def _kv_proj(h, w, w_forget, b_forget, g, n_batch, seq, n_heads, tk):
    n, d = h.shape
    dh = d // n_heads
    tm = tk
    tpb = seq // tm
    hc = _tile(n_heads, 4)
    assert n_heads <= V7X_LANES
    wf = jnp.zeros((d, V7X_LANES), BF16).at[:, :n_heads].set(w_forget.astype(BF16))
    bf = jnp.zeros((1, V7X_LANES), F32).at[0, :n_heads].set(b_forget)
    nbytes = (2 * tm * d * 2 + 2 * d * d * 2 + 2 * n_heads * tm * 3 * dh * 2
              + 2 * tm * hc * dh * 4 + 3 * tm * tm * 4)
    return pl.pallas_call(
        functools.partial(_kv_kernel, hc=hc, tpb=tpb),
        grid=(n // tm,),
        in_specs=[pl.BlockSpec((tm, d), lambda i: (i, 0)), _resident((d, 2 * d)),
                  _resident((d, V7X_LANES)), pl.BlockSpec((1, V7X_LANES), lambda i: (0, 0)),
                  pl.BlockSpec((1, dh), lambda i: (0, 0))],
        out_specs=[
            pl.BlockSpec((1, n_heads, tm, 2 * dh), lambda i: (i // tpb, 0, i % tpb, 0)),
            pl.BlockSpec((1, n_heads, 1, dh, tm), lambda i: (i // tpb, 0, i % tpb, 0, 0)),
            pl.BlockSpec((1, 1, V7X_LANES), lambda i: (i, 0, 0)),
            pl.BlockSpec((1, V7X_LANES, tm), lambda i: (i // tpb, 0, i % tpb)),
        ],
        out_shape=[jax.ShapeDtypeStruct((n_batch, n_heads, seq, 2 * dh), BF16),
                   jax.ShapeDtypeStruct((n_batch, n_heads, seq // tk, dh, tk), BF16),
                   jax.ShapeDtypeStruct((n // tm, 1, V7X_LANES), F32),
                   jax.ShapeDtypeStruct((n_batch, V7X_LANES, seq), F32)],
        scratch_shapes=[pltpu.VMEM((V7X_SUBLANES, V7X_LANES), F32)],
        compiler_params=_params(1, nbytes),
        name="kv_proj",
    )(h, w, wf, bf, g.reshape(1, dh))


def _first_key_blocks(f_end, q_gain, k_gain, dh):
    n_batch, n_kv, n_heads = f_end.shape
    nq = n_kv // 2
    qk_max = (QK_BOUND_MARGIN * LOG2_E * math.sqrt(dh)
              * jnp.max(jnp.abs(q_gain)) * jnp.max(jnp.abs(k_gain)))
    first_own = 2 * jnp.arange(nq) - 1
    ref = f_end[:, jnp.maximum(first_own, 0), :]
    gap = LOG2_E * (ref[:, :, None, :] - f_end[:, None, :, :]) + 2.0 * qk_max + 1.0
    below = jnp.arange(n_kv)[None, :] < first_own[:, None]
    needed = jnp.logical_not((gap < -F32_UNDERFLOW_LOG2) & below[None, :, :, None])
    first_block = jnp.argmax(needed, axis=2)
    first_block = jnp.transpose(first_block, (0, 2, 1)).reshape(-1).astype(jnp.int32)

    k_bound = QK_BOUND_MARGIN * math.sqrt(dh) * jnp.max(jnp.abs(k_gain))
    use_bound = (2.0 * qk_max <= MAX_SHIFT_OVERSHOOT_LOG2).astype(F32)
    block_term = jnp.transpose(-LOG2_E * f_end + F_BOUND_SLACK, (0, 2, 1)).reshape(-1)
    bounds = jnp.concatenate([block_term, jnp.stack([k_bound, use_bound])]).astype(F32)
    return first_block, bounds


def _attn_kernel(first_ref, bnd_ref, q_ref, qn_ref, k_ref, vt_ref, fq_ref, o_ref, s_s, p_s, acc_s, *, qc):
    tq = q_ref.shape[3]
    n_kv, dh, tk = vt_ref.shape[2:]
    assert tq == 2 * tk
    qi = pl.program_id(2)
    n_q = pl.num_programs(2)
    tile = (pl.program_id(0) * pl.num_programs(1) + pl.program_id(1)) * n_q + qi
    j0 = first_ref[tile]
    whole = [slice(0, tq)]
    groups = [slice(g * qc, (g + 1) * qc) for g in range(tq // qc)]
    key_minus_query = (lax.broadcasted_iota(jnp.int32, (tk, qc), 0)
                       - lax.broadcasted_iota(jnp.int32, (tk, qc), 1))

    def scores(j, cols, query_ref=q_ref):
        kblk = k_ref[0, 0, pl.ds(pl.multiple_of(j * tk, tk), tk), :]
        return jnp.dot(kblk, query_ref[0, 0, :, cols], preferred_element_type=F32)

    head = pl.program_id(0) * pl.num_programs(1) + pl.program_id(1)
    n_bnd = pl.num_programs(0) * pl.num_programs(1) * n_kv
    use_bound = bnd_ref[n_bnd + 1] > 0.5
    q_sq = jnp.square(q_ref[0, 0, 0:dh, :].astype(F32))
    q_reach = jnp.sqrt(jnp.sum(q_sq, axis=0, keepdims=True)) * bnd_ref[n_bnd]
    at_query = fq_ref[0, pl.ds(pl.program_id(1) % V7X_SUBLANES, 1), :] + F_BOUND_SLACK

    def softmax(s, state, j, cols, masked, bounded=False):
        m_old, _ = state
        if masked:
            s = jnp.where(key_minus_query <= qi * tq + cols.start - j * tk, s, NEG_INF)
        if bounded:
            m_blk = q_reach[:, cols] + jnp.minimum(at_query[:, cols], bnd_ref[head * n_kv + j])
        else:
            m_blk = jnp.max(s, axis=0, keepdims=True)
        m_new = jnp.maximum(m_old, m_blk)
        p = jnp.exp2(s - m_new)
        alpha = jnp.exp2(m_old - m_new)
        return p.astype(p_s.dtype), (m_new, alpha)

    ones_rows = jnp.ones((V7X_BF16_ROWS, tk), p_s.dtype)

    def add_values(p, alpha, j, cols):
        vt1 = jnp.concatenate([vt_ref[0, 0, j], ones_rows], axis=0)
        acc_s[:, cols] = alpha * acc_s[:, cols] + jnp.dot(vt1, p, preferred_element_type=F32)

    def pair(a, carry, last, bounded=False):
        new_carry = []
        for cols, state in zip(groups if last else whole, carry):
            mask_a = last and cols.start < tk
            skip_b = last and cols.stop <= tk
            alpha_prev = state[1]
            s_a = s_s[:, cols]
            if not skip_b:
                s_b = scores(a + 1, cols)
            p_a, state = softmax(s_a, state, a, cols, mask_a, bounded)
            add_values(p_s[:, cols], alpha_prev, jnp.maximum(a - 1, 0), cols)
            if not last:
                s_s[:, cols] = scores(a + 2, cols)
            alpha_a = state[1]
            if not skip_b:
                p_b, state = softmax(s_b, state, a + 1, cols, last, bounded)
            add_values(p_a, alpha_a, a, cols)
            if not last:
                p_s[:, cols] = p_b
            elif not skip_b:
                add_values(p_b, state[1], a + 1, cols)
            new_carry.append(state)
        return tuple(new_carry)

    odd = j0 % 2
    a0 = j0 + odd
    n_unmasked = qi - a0 // 2

    def unmasked_blocks(carry, bounded):
        def single(_, cr):
            p, state = softmax(s_s[...], cr[0], j0, whole[0], False, bounded)
            s_s[...] = scores(j0 + 1, whole[0])
            p_s[...] = p
            return (state,)

        def two_pairs(v, cr):
            return pair(a0 + 4 * v + 2, pair(a0 + 4 * v, cr, False, bounded), False, bounded)

        carry = lax.fori_loop(0, odd, single, carry)
        carry = lax.fori_loop(0, n_unmasked // 2, two_pairs, carry)
        return lax.fori_loop(0, n_unmasked % 2,
                             lambda _, cr: pair(2 * qi - 2, cr, False, bounded), carry)

    @pl.when(qi == 0)
    def _():
        s_s[...] = scores(j0, whole[0])

    p_s[...] = jnp.zeros(p_s.shape, p_s.dtype)
    acc_s[...] = jnp.zeros(acc_s.shape, F32)

    def whole_tile(bounded):
        carry = ((jnp.full((1, tq), NEG_INF, F32), jnp.ones((1, tq), F32)),)
        carry = unmasked_blocks(carry, bounded)
        carry = tuple(tuple(v[:, cols] for v in carry[0]) for cols in groups)
        pair(2 * qi, carry, True, bounded)

        next_tile = jnp.minimum(tile + 1, pl.num_programs(0) * pl.num_programs(1) * n_q - 1)
        s_s[...] = scores(first_ref[next_tile], whole[0], qn_ref)

        for cols in groups:
            out = acc_s[0:dh, cols] * (1.0 / acc_s[dh:dh + 1, cols])
            o_ref[0, cols, :] = out.T.astype(o_ref.dtype)

    pl.when(use_bound)(lambda: whole_tile(True))
    pl.when(jnp.logical_not(use_bound))(lambda: whole_tile(False))


def _attention(first_blocks, bounds, qp, kp, vt, fq):
    n_batch, n_heads, dh2, seq = qp.shape
    n_kv, dh, tk = vt.shape[2:]
    tq = 2 * tk
    qc = _tile(tq, 512)
    nbytes = (2 * tq * dh2 * 2 + 2 * seq * dh2 * 2 + 2 * seq * dh * 2 + 2 * tq * dh * 2
              + dh * tq * 4 + tk * tq * 6 + 4 * tk * tq * 4)
    return pl.pallas_call(
        functools.partial(_attn_kernel, qc=qc),
        grid_spec=pltpu.PrefetchScalarGridSpec(
            num_scalar_prefetch=1,
            grid=(n_batch, n_heads, seq // tq),
            in_specs=[
                pl.BlockSpec(memory_space=pltpu.SMEM),
                pl.BlockSpec((1, 1, dh2, tq), lambda b, h, i, first: (b, h, 0, i)),
                pl.BlockSpec((1, 1, dh2, tq),
                             lambda b, h, i, first: (b, h, 0, jnp.minimum(i + 1, seq // tq - 1))),
                pl.BlockSpec((1, 1, seq, dh2), lambda b, h, i, first: (b, h, 0, 0)),
                pl.BlockSpec((1, 1, n_kv, dh, tk), lambda b, h, i, first: (b, h, 0, 0, 0)),
                pl.BlockSpec((1, V7X_SUBLANES, tq), lambda b, h, i, first: (b, h // V7X_SUBLANES, i)),
            ],
            out_specs=pl.BlockSpec((1, tq, dh), lambda b, h, i, first: (b, i, h)),
            scratch_shapes=[pltpu.VMEM((tk, tq), F32), pltpu.VMEM((tk, tq), BF16),
                            pltpu.VMEM((dh + V7X_BF16_ROWS, tq), F32)],
        ),
        out_shape=jax.ShapeDtypeStruct((n_batch, seq, n_heads * dh), BF16),
        compiler_params=_params(3, nbytes),
        name="forgetting_attention",
    )(first_blocks, bounds, qp, qp, kp, vt, fq)


def kernel(x, c, mix_norm_g, mlp_norm_g, w_mod, b_mod, w_mlp_in, w_mlp_out, lru_w_in, lru_conv_w, lru_conv_b, lru_w_a, lru_b_a, lru_w_i, lru_b_i, lru_lambda, lru_w_out, kv_norm_g, kv_w_mod, kv_b_mod, w_kv, k_norm_g, w_forget, b_forget, attn_w_q, q_norm_g, attn_w_o):
    n_batch, seq, d = x.shape
    depth = w_mod.shape[0]
    n_a = lru_w_in.shape[0]
    n_heads = w_forget.shape[1]
    assert 1 <= n_a < depth and w_mod.shape[2] == N_MOD * d
    n = n_batch * seq
    tk = _tile(seq, 512)

    mod = _modulation(c, w_mod, b_mod)
    kv_mod = _modulation(c, kv_w_mod[None], kv_b_mod[None])[0]
    per_batch = lambda a: a.reshape(n_batch, 1, d)
    sh1, sc1, g1, sh2, sc2, g2 = [[per_batch(mod[l, :, j * d:(j + 1) * d]) for l in range(depth)]
                                  for j in range(N_MOD)]
    kv_shift, kv_scale = per_batch(kv_mod[:, :d]), per_batch(kv_mod[:, d:])

    bf = lambda w: w.astype(BF16)
    w_mlp_in, w_mlp_out, lru_w_in, lru_w_out = bf(w_mlp_in), bf(w_mlp_out), bf(lru_w_in), bf(lru_w_out)
    attn_w_q, attn_w_o = bf(attn_w_q), bf(attn_w_o)
    x2 = x.reshape(n, d)
    h = _norm_mod(x2, mix_norm_g[0], sc1[0], sh1[0], seq)
    h_kv = kp = vt = f_end = fq = None
    for layer in range(depth):
        if layer < n_a:
            a = layer
            xg = _lru_in(h, lru_w_in, a)
            mix_in = _lru_scan(xg, lru_conv_w[a], lru_conv_b[a], bf(lru_w_a[a]), lru_b_a[a],
                               bf(lru_w_i[a]), lru_b_i[a], lru_lambda[a], n_batch, seq)
            w_o, w_o_layer = lru_w_out, a
        else:
            bl = layer - n_a
            if layer == n_a:
                kp, vt, f_end, fq = _kv_proj(h_kv, bf(w_kv), w_forget, b_forget, k_norm_g, n_batch, seq,
                                             n_heads, tk)
                f_end = f_end.reshape(n_batch, seq // tk, V7X_LANES)[:, :, :n_heads]
            qp = _q_proj(h, attn_w_q, bl, q_norm_g[bl], n_batch, seq, n_heads)
            first_blocks, bounds = _first_key_blocks(f_end, q_norm_g[bl], k_norm_g, d // n_heads)
            mix_in = _attention(first_blocks, bounds, qp, kp, vt, fq).reshape(n, d)
            w_o, w_o_layer = attn_w_o, bl
        x2, h2 = _proj_res(mix_in, w_o, w_o_layer, x2, g1[layer], mlp_norm_g[layer], sc2[layer],
                           sh2[layer], seq)
        norms = []
        if layer + 1 < depth:
            norms.append((mix_norm_g[layer + 1], sc1[layer + 1], sh1[layer + 1]))
        if layer + 1 == n_a:
            norms.append((kv_norm_g, kv_scale, kv_shift))
        x2, hs = _mlp(h2, x2, w_mlp_in, w_mlp_out, layer, g2[layer], norms, seq)
        if hs:
            h = hs[0]
        if layer + 1 == n_a:
            h_kv = hs[1]
    return x2.reshape(n_batch, seq, d)
```

```python
import functools
import math

import jax
import jax.numpy as jnp
from jax import lax
from jax.experimental import pallas as pl
from jax.experimental.pallas import tpu as pltpu

F32 = jnp.float32
BF16 = jnp.bfloat16

NORM_EPS = 1e-6
LRU_C = 8.0
NEG_INF = -1e30
N_MOD = 6

V7X_LANES = 128
V7X_SUBLANES = 8
V7X_BF16_ROWS = 16
V7X_VMEM_BYTES = 64 * 1024 * 1024
V7X_VMEM_RESERVE = 6 * 1024 * 1024
V7X_SCOPED_VMEM_DEFAULT = 32 * 1024 * 1024
MATMUL_TEMP_BYTES = 8 * 1024 * 1024

N_F_PARTS = 3
LOG2_E = 1.4426950408889634
F32_UNDERFLOW_LOG2 = 150.0
QK_BOUND_MARGIN = 1.02
F_BOUND_SLACK = 0.05
MAX_SHIFT_OVERSHOOT_LOG2 = 64.0


def _vmem_limit(nbytes):
    want = nbytes * 5 // 4 + MATMUL_TEMP_BYTES
    return int(min(V7X_VMEM_BYTES - V7X_VMEM_RESERVE, max(want, V7X_SCOPED_VMEM_DEFAULT)))


def _params(n_grid, nbytes):
    return pltpu.CompilerParams(dimension_semantics=("arbitrary",) * n_grid,
                                vmem_limit_bytes=_vmem_limit(nbytes))


def _tile(n, want):
    t = min(n, want)
    while n % t:
        t -= 1
    return t


def _resident(shape, layer=None):
    nd = len(shape)
    if layer is None:
        return pl.BlockSpec(shape, lambda *_: (0,) * nd, pipeline_mode=pl.Buffered(1))
    return pl.BlockSpec((pl.Squeezed(),) + tuple(shape), lambda *_: (layer,) + (0,) * nd,
                        pipeline_mode=pl.Buffered(1))


def _rms_mod(x, g, scale, shift):
    ms = jnp.mean(x * x, axis=-1, keepdims=True)
    return (x * lax.rsqrt(ms + NORM_EPS)) * (g * (1.0 + scale)) + shift


def _gelu_tanh(x):
    c = math.sqrt(2.0 / math.pi)
    return x * (0.5 * (1.0 + jnp.tanh(c * (x + 0.044715 * (x * x * x)))))


def _sigmoid(x):
    return 0.5 * jnp.tanh(0.5 * x) + 0.5


def _sqrt_nonneg(y):
    return jnp.where(y > 0.0, y * lax.rsqrt(y), 0.0)


def _log_sigmoid(x):
    return jnp.minimum(x, 0.0) - jnp.log1p(jnp.exp(-jnp.abs(x)))


def _softplus(x):
    return jnp.maximum(x, 0.0) + jnp.log1p(jnp.exp(-jnp.abs(x)))


def _mod_kernel(cb_ref, w_ref, b_ref, o_ref):
    n_batch = cb_ref.shape[0]
    tn = w_ref.shape[2]
    for b in range(n_batch):
        cv = cb_ref[b]
        cs = cv * _sigmoid(cv)
        for ch in range(tn // V7X_LANES):
            cols = slice(ch * V7X_LANES, (ch + 1) * V7X_LANES)
            s = jnp.sum(w_ref[0, :, cols] * cs, axis=0, keepdims=True)
            o_ref[0, b:b + 1, cols] = s + b_ref[0, :, cols]


def _modulation(c, w, b):
    n_l, d, m = w.shape
    n_batch = c.shape[0]
    tn = _tile(m, 1024)
    cb = jnp.broadcast_to(c[:, :, None], (n_batch, d, V7X_LANES))
    nbytes = cb.size * 4 + 2 * d * tn * 4 + 4 * n_batch * tn * 4
    return pl.pallas_call(
        _mod_kernel,
        grid=(n_l, m // tn),
        in_specs=[
            pl.BlockSpec((n_batch, d, V7X_LANES), lambda l, j: (0, 0, 0)),
            pl.BlockSpec((1, d, tn), lambda l, j: (l, 0, j)),
            pl.BlockSpec((1, 1, tn), lambda l, j: (l, 0, j)),
        ],
        out_specs=pl.BlockSpec((1, n_batch, tn), lambda l, j: (l, 0, j)),
        out_shape=jax.ShapeDtypeStruct((n_l, n_batch, m), F32),
        compiler_params=_params(2, nbytes),
        name="modulation",
    )(cb, w, b.reshape(n_l, 1, m))


def _norm_kernel(x_ref, g_ref, sc_ref, sh_ref, o_ref):
    o_ref[...] = _rms_mod(x_ref[...], g_ref[...], sc_ref[0], sh_ref[0]).astype(o_ref.dtype)


def _norm_mod(x2, g, scale, shift, seq):
    n, d = x2.shape
    tm = _tile(seq, 512)
    tpb = seq // tm
    row = lambda i: (i, 0)
    per_batch = lambda i: (i // tpb, 0, 0)
    nbytes = 2 * tm * d * 4 + 2 * tm * d * 2
    return pl.pallas_call(
        _norm_kernel,
        grid=(n // tm,),
        in_specs=[
            pl.BlockSpec((tm, d), row),
            pl.BlockSpec((1, d), lambda i: (0, 0)),
            pl.BlockSpec((1, 1, d), per_batch),
            pl.BlockSpec((1, 1, d), per_batch),
        ],
        out_specs=pl.BlockSpec((tm, d), row),
        out_shape=jax.ShapeDtypeStruct((n, d), BF16),
        compiler_params=_params(1, nbytes),
        name="norm_mod",
    )(x2, g.reshape(1, d), scale, shift)


def _lru_in_kernel(h_ref, w_ref, o_ref, *, tc):
    h = h_ref[...]
    n_cols = w_ref.shape[1]
    for c in range(n_cols // tc):
        cols = slice(c * tc, (c + 1) * tc)
        acc = jnp.dot(h, w_ref[:, cols], preferred_element_type=F32)
        if c * tc >= n_cols // 2:
            acc = _gelu_tanh(acc)
        o_ref[:, cols] = acc


def _lru_in(h, w, layer):
    n, d = h.shape
    m = w.shape[2]
    tm = _tile(n, 512)
    tc = _tile(m // 2, 512)
    nbytes = 2 * tm * d * 2 + d * m * 2 + 2 * tm * m * 4 + 2 * tm * tc * 4
    return pl.pallas_call(
        functools.partial(_lru_in_kernel, tc=tc),
        grid=(n // tm,),
        in_specs=[pl.BlockSpec((tm, d), lambda i: (i, 0)), _resident((d, m), layer)],
        out_specs=pl.BlockSpec((tm, m), lambda i: (i, 0)),
        out_shape=jax.ShapeDtypeStruct((n, m), F32),
        compiler_params=_params(1, nbytes),
        name="lru_in_proj",
    )(h, w)


def _lru_scan_kernel(xb_ref, gy_ref, cw_ref, cb_ref, wa_ref, ba_ref, wi_ref, bi_ref, lam_ref,
                     o_ref, xpad_s, a_s, u_s, h_s, *, lane_chunk):
    ts, width = xb_ref.shape
    conv_width = cw_ref.shape[0]
    n_blocks, bd, _ = wa_ref.shape
    halo = V7X_SUBLANES
    assert conv_width - 1 <= halo

    @pl.when(pl.program_id(1) == 0)
    def _():
        xpad_s[0:halo, :] = jnp.zeros((halo, width), F32)
        h_s[...] = jnp.zeros(h_s.shape, F32)

    xpad_s[halo:halo + ts, :] = xb_ref[...]

    decay = -LRU_C * _softplus(-lam_ref[...])

    for nb in range(n_blocks):
        cols = slice(nb * bd, (nb + 1) * bd)
        xc = cb_ref[:, cols] + cw_ref[conv_width - 1:conv_width, cols] * xpad_s[halo:halo + ts, cols]
        for k in range(conv_width - 1):
            back = conv_width - 1 - k
            xc = xc + cw_ref[k:k + 1, cols] * xpad_s[halo - back:halo - back + ts, cols]
        xcb = xc.astype(BF16)
        r = _sigmoid(jnp.dot(xcb, wa_ref[nb], preferred_element_type=F32) + ba_ref[:, cols])
        gi = _sigmoid(jnp.dot(xcb, wi_ref[nb], preferred_element_type=F32) + bi_ref[:, cols])
        log_a = decay[:, cols] * r
        a = jnp.exp(log_a)
        a_s[:, cols] = a
        u_s[:, cols] = _sqrt_nonneg(-jnp.tanh(log_a) * (a * a + 1.0)) * (gi * xc)

    xpad_s[0:halo, :] = xpad_s[ts:ts + halo, :]

    rows16 = V7X_BF16_ROWS
    row_id = lax.broadcasted_iota(jnp.int32, (V7X_SUBLANES, lane_chunk), 0)

    def group(g, carry):
        r0 = pl.multiple_of(g * rows16, rows16)
        for lc in range(width // lane_chunk):
            cols = slice(lc * lane_chunk, (lc + 1) * lane_chunk)
            hprev = h_s[:, cols]
            halves = []
            for half in range(rows16 // V7X_SUBLANES):
                rr = pl.ds(r0 + half * V7X_SUBLANES, V7X_SUBLANES)
                a = a_s[rr, cols]
                u = u_s[rr, cols]
                for sh in (1, 2, 4):
                    a_sh = pltpu.roll(a, sh, axis=0)
                    u_sh = pltpu.roll(u, sh, axis=0)
                    live = row_id >= sh
                    u = jnp.where(live, a * u_sh + u, u)
                    a = jnp.where(live, a * a_sh, a)
                hcur = a * hprev + u
                hprev = jnp.broadcast_to(hcur[V7X_SUBLANES - 1:V7X_SUBLANES, :], hcur.shape)
                halves.append(hcur)
            h_s[:, cols] = hprev
            h16 = jnp.concatenate(halves, axis=0)
            o_ref[pl.ds(r0, rows16), cols] = (h16 * gy_ref[pl.ds(r0, rows16), cols]).astype(o_ref.dtype)
        return carry

    lax.fori_loop(0, ts // rows16, group, 0)


def _lru_scan(xg, conv_w, conv_b, w_a, b_a, w_i, b_i, lam, n_batch, seq):
    n, two_w = xg.shape
    width = two_w // 2
    ts = _tile(seq, 256)
    tpb = seq // ts
    n_blocks, bd, _ = w_a.shape
    lane_chunk = _tile(width, 512)
    row = lambda b, t: (b * tpb + t, 0)
    vec = lambda a: a.reshape(1, width)
    small = lambda shape: pl.BlockSpec(shape, lambda b, t: (0,) * len(shape))
    nbytes = (4 * ts * width * 4 + 2 * ts * width * 2 + (3 * ts + 16) * width * 4
              + 4 * n_blocks * bd * bd * 2)
    return pl.pallas_call(
        functools.partial(_lru_scan_kernel, lane_chunk=lane_chunk),
        grid=(n_batch, tpb),
        in_specs=[
            pl.BlockSpec((ts, width), row),
            pl.BlockSpec((ts, width), lambda b, t: (b * tpb + t, 1)),
            small(conv_w.shape), small((1, width)),
            small(w_a.shape), small((1, width)),
            small(w_i.shape), small((1, width)),
            small((1, width)),
        ],
        out_specs=pl.BlockSpec((ts, width), row),
        out_shape=jax.ShapeDtypeStruct((n, width), BF16),
        scratch_shapes=[
            pltpu.VMEM((ts + 2 * V7X_SUBLANES, width), F32),
            pltpu.VMEM((ts, width), F32),
            pltpu.VMEM((ts, width), F32),
            pltpu.VMEM((V7X_SUBLANES, width), F32),
        ],
        compiler_params=_params(2, nbytes),
        name="lru_scan",
    )(xg, xg, conv_w, vec(conv_b), w_a, vec(b_a), w_i, vec(b_i), vec(lam))


def _proj_res_kernel(a_ref, w_ref, x_ref, g1_ref, ng_ref, sc_ref, sh_ref, xo_ref, h_ref, *, rc):
    tm = a_ref.shape[0]
    for r in range(tm // rc):
        rows = slice(r * rc, (r + 1) * rc)
        acc = jnp.dot(a_ref[rows, :], w_ref[...], preferred_element_type=F32)
        xn = x_ref[rows, :] + g1_ref[0] * acc
        xo_ref[rows, :] = xn
        h_ref[rows, :] = _rms_mod(xn, ng_ref[...], sc_ref[0], sh_ref[0]).astype(h_ref.dtype)


def _proj_res(a, w, layer, x2, gate, norm_g, scale, shift, seq):
    n, k = a.shape
    d = w.shape[2]
    tm = _tile(seq, 512)
    rc = _tile(tm, 256)
    tpb = seq // tm
    row = lambda i: (i, 0)
    per_batch = lambda i: (i // tpb, 0, 0)
    nbytes = 2 * tm * k * 2 + k * d * 2 + 4 * tm * d * 4 + 2 * tm * d * 2 + 3 * rc * d * 4
    return pl.pallas_call(
        functools.partial(_proj_res_kernel, rc=rc),
        grid=(n // tm,),
        in_specs=[
            pl.BlockSpec((tm, k), row), _resident((k, d), layer), pl.BlockSpec((tm, d), row),
            pl.BlockSpec((1, 1, d), per_batch), pl.BlockSpec((1, d), lambda i: (0, 0)),
            pl.BlockSpec((1, 1, d), per_batch), pl.BlockSpec((1, 1, d), per_batch),
        ],
        out_specs=[pl.BlockSpec((tm, d), row), pl.BlockSpec((tm, d), row)],
        out_shape=[jax.ShapeDtypeStruct((n, d), F32), jax.ShapeDtypeStruct((n, d), BF16)],
        compiler_params=_params(1, nbytes),
        name="mix_out_proj",
    )(a, w, x2, gate, norm_g.reshape(1, d), scale, shift)


def _mlp_kernel(*refs, n_norm, rc, fc):
    h_ref, win_ref, wout_ref, x_ref, g2_ref = refs[:5]
    norm_refs = refs[5:5 + 3 * n_norm]
    xo_ref = refs[5 + 3 * n_norm]
    ho_refs = refs[6 + 3 * n_norm:6 + 4 * n_norm]
    acc_s = refs[6 + 4 * n_norm]
    k = pl.program_id(1)

    h = h_ref[...]
    tf = win_ref.shape[1]
    pieces = []
    for c in range(tf // fc):
        a = jnp.dot(h, win_ref[:, c * fc:(c + 1) * fc], preferred_element_type=F32)
        pieces.append(jnp.square(jnp.maximum(a, 0.0)).astype(BF16))
    a = pieces[0] if len(pieces) == 1 else jnp.concatenate(pieces, axis=1)

    prev = jnp.where(k > 0, acc_s[...], 0.0)
    acc_s[...] = prev + jnp.dot(a, wout_ref[...], preferred_element_type=F32)

    @pl.when(k == pl.num_programs(1) - 1)
    def _():
        tm = x_ref.shape[0]

        def rows_body(r, carry):
            rows = pl.ds(pl.multiple_of(r * rc, rc), rc)
            xn = x_ref[rows, :] + g2_ref[0] * acc_s[rows, :]
            xo_ref[rows, :] = xn
            for j in range(n_norm):
                ng, sc, sh = norm_refs[3 * j:3 * j + 3]
                ho_refs[j][rows, :] = _rms_mod(xn, ng[...], sc[0], sh[0]).astype(BF16)
            return carry

        lax.fori_loop(0, tm // rc, rows_body, 0)


def _mlp(h2, x2, w_in, w_out, layer, gate, norms, seq):
    n, d = x2.shape
    f = w_in.shape[2]
    tm = _tile(seq, 512)
    tf = _tile(f, 1024)
    fc = _tile(tf, 512)
    rc = _tile(tm, 64)
    tpb = seq // tm
    n_norm = len(norms)
    row = lambda i, k: (i, 0)
    per_batch = lambda i, k: (i // tpb, 0, 0)
    in_specs = [
        pl.BlockSpec((tm, d), row),
        pl.BlockSpec((pl.Squeezed(), d, tf), lambda i, k: (layer, 0, k)),
        pl.BlockSpec((pl.Squeezed(), tf, d), lambda i, k: (layer, k, 0)),
        pl.BlockSpec((tm, d), row),
        pl.BlockSpec((1, 1, d), per_batch),
    ]
    args = [h2, w_in, w_out, x2, gate]
    for ng, sc, sh in norms:
        in_specs += [pl.BlockSpec((1, d), lambda i, k: (0, 0)),
                     pl.BlockSpec((1, 1, d), per_batch), pl.BlockSpec((1, 1, d), per_batch)]
        args += [ng.reshape(1, d), sc, sh]
    out_specs = [pl.BlockSpec((tm, d), row)] * (1 + n_norm)
    out_shape = [jax.ShapeDtypeStruct((n, d), F32)] + [jax.ShapeDtypeStruct((n, d), BF16)] * n_norm
    nbytes = (2 * tm * d * 2 + 4 * d * tf * 2 + 4 * tm * d * 4 + tm * d * 4
              + 2 * n_norm * tm * d * 2 + 2 * tm * tf * 4 + tm * d * 4)
    outs = pl.pallas_call(
        functools.partial(_mlp_kernel, n_norm=n_norm, rc=rc, fc=fc),
        grid=(n // tm, f // tf),
        in_specs=in_specs,
        out_specs=out_specs,
        out_shape=out_shape,
        scratch_shapes=[pltpu.VMEM((tm, d), F32)],
        compiler_params=_params(2, nbytes),
        name="relu2_mlp",
    )(*args)
    return outs[0], list(outs[1:])


def _head_norm(a, g):
    ms = jnp.mean(a * a, axis=-1, keepdims=True)
    return a * lax.rsqrt(ms + NORM_EPS) * g


def _q_kernel(h_ref, w_ref, g_ref, o_ref, *, hc, scale):
    tm = h_ref.shape[0]
    n_heads, dh = o_ref.shape[1], g_ref.shape[1]
    h = h_ref[...]
    row = lax.broadcasted_iota(jnp.int32, (dh, tm), 0)
    ones_rows = jnp.where(row < N_F_PARTS, 1.0, 0.0).astype(o_ref.dtype)
    gs = g_ref[...] * scale
    for c in range(n_heads // hc):
        acc = jnp.dot(h, w_ref[:, c * hc * dh:(c + 1) * hc * dh], preferred_element_type=F32)
        for hh in range(hc):
            head = c * hc + hh
            qn = _head_norm(acc[:, hh * dh:(hh + 1) * dh], gs)
            o_ref[0, head, 0:dh, :] = qn.T.astype(o_ref.dtype)
            o_ref[0, head, dh:2 * dh, :] = ones_rows


def _q_proj(h, w, layer, g, n_batch, seq, n_heads):
    n, d = h.shape
    dh = d // n_heads
    tm = _tile(seq, 512)
    tpb = seq // tm
    hc = _tile(n_heads, 4)
    nbytes = 2 * tm * d * 2 + d * d * 2 + 2 * n_heads * tm * 2 * dh * 2 + 2 * tm * hc * dh * 4
    return pl.pallas_call(
        functools.partial(_q_kernel, hc=hc, scale=dh ** -0.5 * LOG2_E),
        grid=(n // tm,),
        in_specs=[pl.BlockSpec((tm, d), lambda i: (i, 0)), _resident((d, d), layer),
                  pl.BlockSpec((1, dh), lambda i: (0, 0))],
        out_specs=pl.BlockSpec((1, n_heads, 2 * dh, tm), lambda i: (i // tpb, 0, 0, i % tpb)),
        out_shape=jax.ShapeDtypeStruct((n_batch, n_heads, 2 * dh, seq), BF16),
        compiler_params=_params(1, nbytes),
        name="q_proj",
    )(h, w, g.reshape(1, dh))


def _kv_kernel(h_ref, w_ref, wf_ref, bf_ref, g_ref, ko_ref, vo_ref, fe_ref, fq_ref, carry_s, *, hc, tpb):
    tm, d = h_ref.shape
    n_heads, dh = ko_ref.shape[1], g_ref.shape[1]

    @pl.when(pl.program_id(0) % tpb == 0)
    def _():
        carry_s[...] = jnp.zeros(carry_s.shape, F32)

    h = h_ref[...]

    z = jnp.dot(h, wf_ref[...], preferred_element_type=F32) + bf_ref[...]
    log_f = _log_sigmoid(z)
    ri = lax.broadcasted_iota(jnp.int32, (tm, tm), 0)
    ci = lax.broadcasted_iota(jnp.int32, (tm, tm), 1)
    tri = jnp.where(ri >= ci, 1.0, 0.0).astype(F32)
    cum = jnp.dot(tri, log_f, preferred_element_type=F32, precision=lax.Precision.HIGHEST)
    cum = cum + carry_s[0:1, :]
    carry_s[...] = jnp.broadcast_to(cum[tm - 1:tm, :], carry_s.shape)
    fe_ref[0] = cum[tm - 1:tm, :]

    parts = []
    rest = -LOG2_E * cum
    fq_ref[0] = rest.T
    for _ in range(N_F_PARTS):
        piece = rest.astype(BF16).astype(F32)
        parts.append(piece)
        rest = rest - piece

    lane = lax.broadcasted_iota(jnp.int32, (tm, dh), 1)
    g = g_ref[...]
    for c in range(n_heads // hc):
        acc = jnp.dot(h, w_ref[:, c * hc * dh:(c + 1) * hc * dh], preferred_element_type=F32)
        for hh in range(hc):
            head = c * hc + hh
            kn = _head_norm(acc[:, hh * dh:(hh + 1) * dh], g)
            ko_ref[0, head, :, 0:dh] = kn.astype(ko_ref.dtype)
            aug = jnp.zeros((tm, dh), F32)
            for p in range(N_F_PARTS):
                col = jnp.broadcast_to(parts[p][:, head:head + 1], (tm, dh))
                aug = jnp.where(lane == p, col, aug)
            ko_ref[0, head, :, dh:2 * dh] = aug.astype(ko_ref.dtype)
    for c in range(n_heads // hc):
        acc = jnp.dot(h, w_ref[:, d + c * hc * dh:d + (c + 1) * hc * dh], preferred_element_type=F32)
        for hh in range(hc):
            head = c * hc + hh
            vo_ref[0, head, 0] = acc[:, hh * dh:(hh + 1) * dh].T.astype(vo_ref.dtype)


def _kv_proj(h, w, w_forget, b_forget, g, n_batch, seq, n_heads, tk):
    n, d = h.shape
    dh = d // n_heads
    tm = tk
    tpb = seq // tm
    hc = _tile(n_heads, 4)
    assert n_heads <= V7X_LANES
    wf = jnp.zeros((d, V7X_LANES), BF16).at[:, :n_heads].set(w_forget.astype(BF16))
    bf = jnp.zeros((1, V7X_LANES), F32).at[0, :n_heads].set(b_forget)
    nbytes = (2 * tm * d * 2 + 2 * d * d * 2 + 2 * n_heads * tm * 3 * dh * 2
              + 2 * tm * hc * dh * 4 + 3 * tm * tm * 4)
    return pl.pallas_call(
        functools.partial(_kv_kernel, hc=hc, tpb=tpb),
        grid=(n // tm,),
        in_specs=[pl.BlockSpec((tm, d), lambda i: (i, 0)), _resident((d, 2 * d)),
                  _resident((d, V7X_LANES)), pl.BlockSpec((1, V7X_LANES), lambda i: (0, 0)),
                  pl.BlockSpec((1, dh), lambda i: (0, 0))],
        out_specs=[
            pl.BlockSpec((1, n_heads, tm, 2 * dh), lambda i: (i // tpb, 0, i % tpb, 0)),
            pl.BlockSpec((1, n_heads, 1, dh, tm), lambda i: (i // tpb, 0, i % tpb, 0, 0)),
            pl.BlockSpec((1, 1, V7X_LANES), lambda i: (i, 0, 0)),
            pl.BlockSpec((1, V7X_LANES, tm), lambda i: (i // tpb, 0, i % tpb)),
        ],
        out_shape=[jax.ShapeDtypeStruct((n_batch, n_heads, seq, 2 * dh), BF16),
                   jax.ShapeDtypeStruct((n_batch, n_heads, seq // tk, dh, tk), BF16),
                   jax.ShapeDtypeStruct((n // tm, 1, V7X_LANES), F32),
                   jax.ShapeDtypeStruct((n_batch, V7X_LANES, seq), F32)],
        scratch_shapes=[pltpu.VMEM((V7X_SUBLANES, V7X_LANES), F32)],
        compiler_params=_params(1, nbytes),
        name="kv_proj",
    )(h, w, wf, bf, g.reshape(1, dh))


def _first_key_blocks(f_end, q_gain, k_gain, dh):
    n_batch, n_kv, n_heads = f_end.shape
    nq = n_kv // 2
    qk_max = (QK_BOUND_MARGIN * LOG2_E * math.sqrt(dh)
              * jnp.max(jnp.abs(q_gain)) * jnp.max(jnp.abs(k_gain)))
    first_own = 2 * jnp.arange(nq) - 1
    ref = f_end[:, jnp.maximum(first_own, 0), :]
    gap = LOG2_E * (ref[:, :, None, :] - f_end[:, None, :, :]) + 2.0 * qk_max + 1.0
    below = jnp.arange(n_kv)[None, :] < first_own[:, None]
    needed = jnp.logical_not((gap < -F32_UNDERFLOW_LOG2) & below[None, :, :, None])
    first_block = jnp.argmax(needed, axis=2)
    first_block = jnp.transpose(first_block, (0, 2, 1)).reshape(-1).astype(jnp.int32)

    k_bound = QK_BOUND_MARGIN * math.sqrt(dh) * jnp.max(jnp.abs(k_gain))
    use_bound = (2.0 * qk_max <= MAX_SHIFT_OVERSHOOT_LOG2).astype(F32)
    block_term = jnp.transpose(-LOG2_E * f_end + F_BOUND_SLACK, (0, 2, 1)).reshape(-1)
    bounds = jnp.concatenate([block_term, jnp.stack([k_bound, use_bound])]).astype(F32)
    return first_block, bounds


def _attn_kernel(first_ref, bnd_ref, q_ref, qn_ref, k_ref, vt_ref, fq_ref, o_ref, s_s, p_s, acc_s, *, qc):
    tq = q_ref.shape[3]
    n_kv, dh, tk = vt_ref.shape[2:]
    assert tq == 2 * tk
    qi = pl.program_id(2)
    n_q = pl.num_programs(2)
    tile = (pl.program_id(0) * pl.num_programs(1) + pl.program_id(1)) * n_q + qi
    j0 = first_ref[tile]
    whole = [slice(0, tq)]
    groups = [slice(g * qc, (g + 1) * qc) for g in range(tq // qc)]
    key_minus_query = (lax.broadcasted_iota(jnp.int32, (tk, qc), 0)
                       - lax.broadcasted_iota(jnp.int32, (tk, qc), 1))

    def scores(j, cols, query_ref=q_ref):
        kblk = k_ref[0, 0, pl.ds(pl.multiple_of(j * tk, tk), tk), :]
        return jnp.dot(kblk, query_ref[0, 0, :, cols], preferred_element_type=F32)

    head = pl.program_id(0) * pl.num_programs(1) + pl.program_id(1)
    n_bnd = pl.num_programs(0) * pl.num_programs(1) * n_kv
    use_bound = bnd_ref[n_bnd + 1] > 0.5
    q_sq = jnp.square(q_ref[0, 0, 0:dh, :].astype(F32))
    q_reach = jnp.sqrt(jnp.sum(q_sq, axis=0, keepdims=True)) * bnd_ref[n_bnd]
    at_query = fq_ref[0, pl.ds(pl.program_id(1) % V7X_SUBLANES, 1), :] + F_BOUND_SLACK

    def softmax(s, state, j, cols, masked, bounded=False):
        m_old, _ = state
        if masked:
            s = jnp.where(key_minus_query <= qi * tq + cols.start - j * tk, s, NEG_INF)
        if bounded:
            m_blk = q_reach[:, cols] + jnp.minimum(at_query[:, cols], bnd_ref[head * n_kv + j])
        else:
            m_blk = jnp.max(s, axis=0, keepdims=True)
        m_new = jnp.maximum(m_old, m_blk)
        p = jnp.exp2(s - m_new)
        alpha = jnp.exp2(m_old - m_new)
        return p.astype(p_s.dtype), (m_new, alpha)

    ones_rows = jnp.ones((V7X_BF16_ROWS, tk), p_s.dtype)

    def add_values(p, alpha, j, cols):
        vt1 = jnp.concatenate([vt_ref[0, 0, j], ones_rows], axis=0)
        acc_s[:, cols] = alpha * acc_s[:, cols] + jnp.dot(vt1, p, preferred_element_type=F32)

    def pair(a, carry, last, bounded=False):
        new_carry = []
        for cols, state in zip(groups if last else whole, carry):
            mask_a = last and cols.start < tk
            skip_b = last and cols.stop <= tk
            alpha_prev = state[1]
            s_a = s_s[:, cols]
            if not skip_b:
                s_b = scores(a + 1, cols)
            p_a, state = softmax(s_a, state, a, cols, mask_a, bounded)
            add_values(p_s[:, cols], alpha_prev, jnp.maximum(a - 1, 0), cols)
            if not last:
                s_s[:, cols] = scores(a + 2, cols)
            alpha_a = state[1]
            if not skip_b:
                p_b, state = softmax(s_b, state, a + 1, cols, last, bounded)
            add_values(p_a, alpha_a, a, cols)
            if not last:
                p_s[:, cols] = p_b
            elif not skip_b:
                add_values(p_b, state[1], a + 1, cols)
            new_carry.append(state)
        return tuple(new_carry)

    odd = j0 % 2
    a0 = j0 + odd
    n_unmasked = qi - a0 // 2

    def unmasked_blocks(carry, bounded):
        def single(_, cr):
            p, state = softmax(s_s[...], cr[0], j0, whole[0], False, bounded)
            s_s[...] = scores(j0 + 1, whole[0])
            p_s[...] = p
            return (state,)

        def two_pairs(v, cr):
            return pair(a0 + 4 * v + 2, pair(a0 + 4 * v, cr, False, bounded), False, bounded)

        carry = lax.fori_loop(0, odd, single, carry)
        carry = lax.fori_loop(0, n_unmasked // 2, two_pairs, carry)
        return lax.fori_loop(0, n_unmasked % 2,
                             lambda _, cr: pair(2 * qi - 2, cr, False, bounded), carry)

    @pl.when(qi == 0)
    def _():
        s_s[...] = scores(j0, whole[0])

    p_s[...] = jnp.zeros(p_s.shape, p_s.dtype)
    acc_s[...] = jnp.zeros(acc_s.shape, F32)

    def whole_tile(bounded):
        carry = ((jnp.full((1, tq), NEG_INF, F32), jnp.ones((1, tq), F32)),)
        carry = unmasked_blocks(carry, bounded)
        carry = tuple(tuple(v[:, cols] for v in carry[0]) for cols in groups)
        pair(2 * qi, carry, True, bounded)

        next_tile = jnp.minimum(tile + 1, pl.num_programs(0) * pl.num_programs(1) * n_q - 1)
        s_s[...] = scores(first_ref[next_tile], whole[0], qn_ref)

        for cols in groups:
            out = acc_s[0:dh, cols] * (1.0 / acc_s[dh:dh + 1, cols])
            o_ref[0, cols, :] = out.T.astype(o_ref.dtype)

    pl.when(use_bound)(lambda: whole_tile(True))
    pl.when(jnp.logical_not(use_bound))(lambda: whole_tile(False))


def _attention(first_blocks, bounds, qp, kp, vt, fq):
    n_batch, n_heads, dh2, seq = qp.shape
    n_kv, dh, tk = vt.shape[2:]
    tq = 2 * tk
    qc = _tile(tq, 512)
    nbytes = (2 * tq * dh2 * 2 + 2 * seq * dh2 * 2 + 2 * seq * dh * 2 + 2 * tq * dh * 2
              + dh * tq * 4 + tk * tq * 6 + 4 * tk * tq * 4)
    return pl.pallas_call(
        functools.partial(_attn_kernel, qc=qc),
        grid_spec=pltpu.PrefetchScalarGridSpec(
            num_scalar_prefetch=1,
            grid=(n_batch, n_heads, seq // tq),
            in_specs=[
                pl.BlockSpec(memory_space=pltpu.SMEM),
                pl.BlockSpec((1, 1, dh2, tq), lambda b, h, i, first: (b, h, 0, i)),
                pl.BlockSpec((1, 1, dh2, tq),
                             lambda b, h, i, first: (b, h, 0, jnp.minimum(i + 1, seq // tq - 1))),
                pl.BlockSpec((1, 1, seq, dh2), lambda b, h, i, first: (b, h, 0, 0)),
                pl.BlockSpec((1, 1, n_kv, dh, tk), lambda b, h, i, first: (b, h, 0, 0, 0)),
                pl.BlockSpec((1, V7X_SUBLANES, tq), lambda b, h, i, first: (b, h // V7X_SUBLANES, i)),
            ],
            out_specs=pl.BlockSpec((1, tq, dh), lambda b, h, i, first: (b, i, h)),
            scratch_shapes=[pltpu.VMEM((tk, tq), F32), pltpu.VMEM((tk, tq), BF16),
                            pltpu.VMEM((dh + V7X_BF16_ROWS, tq), F32)],
        ),
        out_shape=jax.ShapeDtypeStruct((n_batch, seq, n_heads * dh), BF16),
        compiler_params=_params(3, nbytes),
        name="forgetting_attention",
    )(first_blocks, bounds, qp, qp, kp, vt, fq)


def kernel(x, c, mix_norm_g, mlp_norm_g, w_mod, b_mod, w_mlp_in, w_mlp_out, lru_w_in, lru_conv_w, lru_conv_b, lru_w_a, lru_b_a, lru_w_i, lru_b_i, lru_lambda, lru_w_out, kv_norm_g, kv_w_mod, kv_b_mod, w_kv, k_norm_g, w_forget, b_forget, attn_w_q, q_norm_g, attn_w_o):
    n_batch, seq, d = x.shape
    depth = w_mod.shape[0]
    n_a = lru_w_in.shape[0]
    n_heads = w_forget.shape[1]
    assert 1 <= n_a < depth and w_mod.shape[2] == N_MOD * d
    n = n_batch * seq
    tk = _tile(seq, 512)

    mod = _modulation(c, w_mod, b_mod)
    kv_mod = _modulation(c, kv_w_mod[None], kv_b_mod[None])[0]
    per_batch = lambda a: a.reshape(n_batch, 1, d)
    sh1, sc1, g1, sh2, sc2, g2 = [[per_batch(mod[l, :, j * d:(j + 1) * d]) for l in range(depth)]
                                  for j in range(N_MOD)]
    kv_shift, kv_scale = per_batch(kv_mod[:, :d]), per_batch(kv_mod[:, d:])

    bf = lambda w: w.astype(BF16)
    w_mlp_in, w_mlp_out, lru_w_in, lru_w_out = bf(w_mlp_in), bf(w_mlp_out), bf(lru_w_in), bf(lru_w_out)
    attn_w_q, attn_w_o = bf(attn_w_q), bf(attn_w_o)
    x2 = x.reshape(n, d)
    h = _norm_mod(x2, mix_norm_g[0], sc1[0], sh1[0], seq)
    h_kv = kp = vt = f_end = fq = None
    for layer in range(depth):
        if layer < n_a:
            a = layer
            xg = _lru_in(h, lru_w_in, a)
            mix_in = _lru_scan(xg, lru_conv_w[a], lru_conv_b[a], bf(lru_w_a[a]), lru_b_a[a],
                               bf(lru_w_i[a]), lru_b_i[a], lru_lambda[a], n_batch, seq)
            w_o, w_o_layer = lru_w_out, a
        else:
            bl = layer - n_a
            if layer == n_a:
                kp, vt, f_end, fq = _kv_proj(h_kv, bf(w_kv), w_forget, b_forget, k_norm_g, n_batch, seq,
                                             n_heads, tk)
                f_end = f_end.reshape(n_batch, seq // tk, V7X_LANES)[:, :, :n_heads]
            qp = _q_proj(h, attn_w_q, bl, q_norm_g[bl], n_batch, seq, n_heads)
            first_blocks, bounds = _first_key_blocks(f_end, q_norm_g[bl], k_norm_g, d // n_heads)
            mix_in = _attention(first_blocks, bounds, qp, kp, vt, fq).reshape(n, d)
            w_o, w_o_layer = attn_w_o, bl
        x2, h2 = _proj_res(mix_in, w_o, w_o_layer, x2, g1[layer], mlp_norm_g[layer], sc2[layer],
                           sh2[layer], seq)
        norms = []
        if layer + 1 < depth:
            norms.append((mix_norm_g[layer + 1], sc1[layer + 1], sh1[layer + 1]))
        if layer + 1 == n_a:
            norms.append((kv_norm_g, kv_scale, kv_shift))
        x2, hs = _mlp(h2, x2, w_mlp_in, w_mlp_out, layer, g2[layer], norms, seq)
        if hs:
            h = hs[0]
        if layer + 1 == n_a:
            h_kv = hs[1]
    return x2.reshape(n_batch, seq, d)
```

```python
import functools
import math

import jax
import jax.numpy as jnp
from jax import lax
from jax.experimental import pallas as pl
from jax.experimental.pallas import tpu as pltpu

F32 = jnp.float32
BF16 = jnp.bfloat16

NORM_EPS = 1e-6
LRU_C = 8.0
NEG_INF = -1e30
N_MOD = 6

V7X_LANES = 128
V7X_SUBLANES = 8
V7X_BF16_ROWS = 16
V7X_VMEM_BYTES = 64 * 1024 * 1024
V7X_VMEM_RESERVE = 6 * 1024 * 1024
V7X_SCOPED_VMEM_DEFAULT = 32 * 1024 * 1024
MATMUL_TEMP_BYTES = 8 * 1024 * 1024

N_F_PARTS = 3
LOG2_E = 1.4426950408889634
F32_UNDERFLOW_LOG2 = 150.0
QK_BOUND_MARGIN = 1.02
F_BOUND_SLACK = 0.05
MAX_SHIFT_OVERSHOOT_LOG2 = 64.0


def _vmem_limit(nbytes):
    want = nbytes * 5 // 4 + MATMUL_TEMP_BYTES
    return int(min(V7X_VMEM_BYTES - V7X_VMEM_RESERVE, max(want, V7X_SCOPED_VMEM_DEFAULT)))


def _params(n_grid, nbytes):
    return pltpu.CompilerParams(dimension_semantics=("arbitrary",) * n_grid,
                                vmem_limit_bytes=_vmem_limit(nbytes))


def _tile(n, want):
    t = min(n, want)
    while n % t:
        t -= 1
    return t


def _resident(shape, layer=None):
    nd = len(shape)
    if layer is None:
        return pl.BlockSpec(shape, lambda *_: (0,) * nd, pipeline_mode=pl.Buffered(1))
    return pl.BlockSpec((pl.Squeezed(),) + tuple(shape), lambda *_: (layer,) + (0,) * nd,
                        pipeline_mode=pl.Buffered(1))


def _rms_mod(x, g, scale, shift):
    ms = jnp.mean(x * x, axis=-1, keepdims=True)
    return (x * lax.rsqrt(ms + NORM_EPS)) * (g * (1.0 + scale)) + shift


def _gelu_tanh(x):
    c = math.sqrt(2.0 / math.pi)
    return x * (0.5 * (1.0 + jnp.tanh(c * (x + 0.044715 * (x * x * x)))))


def _sigmoid(x):
    return 0.5 * jnp.tanh(0.5 * x) + 0.5


def _sqrt_nonneg(y):
    return jnp.where(y > 0.0, y * lax.rsqrt(y), 0.0)


def _log_sigmoid(x):
    return jnp.minimum(x, 0.0) - jnp.log1p(jnp.exp(-jnp.abs(x)))


def _softplus(x):
    return jnp.maximum(x, 0.0) + jnp.log1p(jnp.exp(-jnp.abs(x)))


def _mod_kernel(cb_ref, w_ref, b_ref, o_ref):
    n_batch = cb_ref.shape[0]
    tn = w_ref.shape[2]
    for b in range(n_batch):
        cv = cb_ref[b]
        cs = cv * _sigmoid(cv)
        for ch in range(tn // V7X_LANES):
            cols = slice(ch * V7X_LANES, (ch + 1) * V7X_LANES)
            s = jnp.sum(w_ref[0, :, cols] * cs, axis=0, keepdims=True)
            o_ref[0, b:b + 1, cols] = s + b_ref[0, :, cols]


def _modulation(c, w, b):
    n_l, d, m = w.shape
    n_batch = c.shape[0]
    tn = _tile(m, 1024)
    cb = jnp.broadcast_to(c[:, :, None], (n_batch, d, V7X_LANES))
    nbytes = cb.size * 4 + 2 * d * tn * 4 + 4 * n_batch * tn * 4
    return pl.pallas_call(
        _mod_kernel,
        grid=(n_l, m // tn),
        in_specs=[
            pl.BlockSpec((n_batch, d, V7X_LANES), lambda l, j: (0, 0, 0)),
            pl.BlockSpec((1, d, tn), lambda l, j: (l, 0, j)),
            pl.BlockSpec((1, 1, tn), lambda l, j: (l, 0, j)),
        ],
        out_specs=pl.BlockSpec((1, n_batch, tn), lambda l, j: (l, 0, j)),
        out_shape=jax.ShapeDtypeStruct((n_l, n_batch, m), F32),
        compiler_params=_params(2, nbytes),
        name="modulation",
    )(cb, w, b.reshape(n_l, 1, m))


def _lru_in_kernel(*refs, tc, fused_norm):
    if fused_norm:
        x_ref, g_ref, sc_ref, sh_ref, w_ref, o_ref = refs
        h = _rms_mod(x_ref[...], g_ref[...], sc_ref[0], sh_ref[0]).astype(BF16)
    else:
        h_ref, w_ref, o_ref = refs
        h = h_ref[...]
    n_cols = w_ref.shape[1]
    for c in range(n_cols // tc):
        cols = slice(c * tc, (c + 1) * tc)
        acc = jnp.dot(h, w_ref[:, cols], preferred_element_type=F32)
        if c * tc >= n_cols // 2:
            acc = _gelu_tanh(acc)
        o_ref[:, cols] = acc


def _lru_in(h, w, layer, seq, norm=None):
    n, d = h.shape
    m = w.shape[2]
    tm = _tile(seq, 512)
    tpb = seq // tm
    tc = _tile(m // 2, 512)
    nbytes = 2 * tm * d * 4 + d * m * 2 + 2 * tm * m * 4 + 2 * tm * tc * 4
    row = pl.BlockSpec((tm, d), lambda i: (i, 0))
    per_batch = pl.BlockSpec((1, 1, d), lambda i: (i // tpb, 0, 0))
    if norm is None:
        in_specs, args = [row], [h]
    else:
        in_specs = [row, pl.BlockSpec((1, d), lambda i: (0, 0)), per_batch, per_batch]
        args = [h, norm[0].reshape(1, d), norm[1], norm[2]]
    return pl.pallas_call(
        functools.partial(_lru_in_kernel, tc=tc, fused_norm=norm is not None),
        grid=(n // tm,),
        in_specs=in_specs + [_resident((d, m), layer)],
        out_specs=pl.BlockSpec((tm, m), lambda i: (i, 0)),
        out_shape=jax.ShapeDtypeStruct((n, m), F32),
        compiler_params=_params(1, nbytes),
        name="lru_in_proj",
    )(*args, w)


def _lru_scan_kernel(xb_ref, gy_ref, cw_ref, cb_ref, wa_ref, ba_ref, wi_ref, bi_ref, lam_ref,
                     o_ref, xpad_s, a_s, u_s, h_s, *, lane_chunk):
    ts, width = xb_ref.shape
    conv_width = cw_ref.shape[0]
    n_blocks, bd, _ = wa_ref.shape
    halo = V7X_SUBLANES
    assert conv_width - 1 <= halo

    @pl.when(pl.program_id(1) == 0)
    def _():
        xpad_s[0:halo, :] = jnp.zeros((halo, width), F32)
        h_s[...] = jnp.zeros(h_s.shape, F32)

    xpad_s[halo:halo + ts, :] = xb_ref[...]

    decay = -LRU_C * _softplus(-lam_ref[...])

    for nb in range(n_blocks):
        cols = slice(nb * bd, (nb + 1) * bd)
        xc = cb_ref[:, cols] + cw_ref[conv_width - 1:conv_width, cols] * xpad_s[halo:halo + ts, cols]
        for k in range(conv_width - 1):
            back = conv_width - 1 - k
            xc = xc + cw_ref[k:k + 1, cols] * xpad_s[halo - back:halo - back + ts, cols]
        xcb = xc.astype(BF16)
        r = _sigmoid(jnp.dot(xcb, wa_ref[nb], preferred_element_type=F32) + ba_ref[:, cols])
        gi = _sigmoid(jnp.dot(xcb, wi_ref[nb], preferred_element_type=F32) + bi_ref[:, cols])
        log_a = decay[:, cols] * r
        a = jnp.exp(log_a)
        a_s[:, cols] = a
        u_s[:, cols] = _sqrt_nonneg(-jnp.tanh(log_a) * (a * a + 1.0)) * (gi * xc)

    xpad_s[0:halo, :] = xpad_s[ts:ts + halo, :]

    rows16 = V7X_BF16_ROWS
    row_id = lax.broadcasted_iota(jnp.int32, (V7X_SUBLANES, lane_chunk), 0)

    def group(g, carry):
        r0 = pl.multiple_of(g * rows16, rows16)
        for lc in range(width // lane_chunk):
            cols = slice(lc * lane_chunk, (lc + 1) * lane_chunk)
            hprev = h_s[:, cols]
            halves = []
            for half in range(rows16 // V7X_SUBLANES):
                rr = pl.ds(r0 + half * V7X_SUBLANES, V7X_SUBLANES)
                a = a_s[rr, cols]
                u = u_s[rr, cols]
                for sh in (1, 2, 4):
                    a_sh = pltpu.roll(a, sh, axis=0)
                    u_sh = pltpu.roll(u, sh, axis=0)
                    live = row_id >= sh
                    u = jnp.where(live, a * u_sh + u, u)
                    a = jnp.where(live, a * a_sh, a)
                hcur = a * hprev + u
                hprev = jnp.broadcast_to(hcur[V7X_SUBLANES - 1:V7X_SUBLANES, :], hcur.shape)
                halves.append(hcur)
            h_s[:, cols] = hprev
            h16 = jnp.concatenate(halves, axis=0)
            o_ref[pl.ds(r0, rows16), cols] = (h16 * gy_ref[pl.ds(r0, rows16), cols]).astype(o_ref.dtype)
        return carry

    lax.fori_loop(0, ts // rows16, group, 0)


def _lru_scan(xg, conv_w, conv_b, w_a, b_a, w_i, b_i, lam, n_batch, seq):
    n, two_w = xg.shape
    width = two_w // 2
    ts = _tile(seq, 256)
    tpb = seq // ts
    n_blocks, bd, _ = w_a.shape
    lane_chunk = _tile(width, 512)
    row = lambda b, t: (b * tpb + t, 0)
    vec = lambda a: a.reshape(1, width)
    small = lambda shape: pl.BlockSpec(shape, lambda b, t: (0,) * len(shape))
    nbytes = (4 * ts * width * 4 + 2 * ts * width * 2 + (3 * ts + 16) * width * 4
              + 4 * n_blocks * bd * bd * 2)
    return pl.pallas_call(
        functools.partial(_lru_scan_kernel, lane_chunk=lane_chunk),
        grid=(n_batch, tpb),
        in_specs=[
            pl.BlockSpec((ts, width), row),
            pl.BlockSpec((ts, width), lambda b, t: (b * tpb + t, 1)),
            small(conv_w.shape), small((1, width)),
            small(w_a.shape), small((1, width)),
            small(w_i.shape), small((1, width)),
            small((1, width)),
        ],
        out_specs=pl.BlockSpec((ts, width), row),
        out_shape=jax.ShapeDtypeStruct((n, width), BF16),
        scratch_shapes=[
            pltpu.VMEM((ts + 2 * V7X_SUBLANES, width), F32),
            pltpu.VMEM((ts, width), F32),
            pltpu.VMEM((ts, width), F32),
            pltpu.VMEM((V7X_SUBLANES, width), F32),
        ],
        compiler_params=_params(2, nbytes),
        name="lru_scan",
    )(xg, xg, conv_w, vec(conv_b), w_a, vec(b_a), w_i, vec(b_i), vec(lam))


def _proj_res_kernel(a_ref, w_ref, x_ref, g1_ref, ng_ref, sc_ref, sh_ref, xo_ref, h_ref, *, rc):
    tm = a_ref.shape[0]
    for r in range(tm // rc):
        rows = slice(r * rc, (r + 1) * rc)
        acc = jnp.dot(a_ref[rows, :], w_ref[...], preferred_element_type=F32)
        xn = x_ref[rows, :] + g1_ref[0] * acc
        xo_ref[rows, :] = xn
        h_ref[rows, :] = _rms_mod(xn, ng_ref[...], sc_ref[0], sh_ref[0]).astype(h_ref.dtype)


def _proj_res(a, w, layer, x2, gate, norm_g, scale, shift, seq):
    n, k = a.shape
    d = w.shape[2]
    tm = _tile(seq, 512)
    rc = _tile(tm, 256)
    tpb = seq // tm
    row = lambda i: (i, 0)
    per_batch = lambda i: (i // tpb, 0, 0)
    nbytes = 2 * tm * k * 2 + k * d * 2 + 4 * tm * d * 4 + 2 * tm * d * 2 + 3 * rc * d * 4
    return pl.pallas_call(
        functools.partial(_proj_res_kernel, rc=rc),
        grid=(n // tm,),
        in_specs=[
            pl.BlockSpec((tm, k), row), _resident((k, d), layer), pl.BlockSpec((tm, d), row),
            pl.BlockSpec((1, 1, d), per_batch), pl.BlockSpec((1, d), lambda i: (0, 0)),
            pl.BlockSpec((1, 1, d), per_batch), pl.BlockSpec((1, 1, d), per_batch),
        ],
        out_specs=[pl.BlockSpec((tm, d), row), pl.BlockSpec((tm, d), row)],
        out_shape=[jax.ShapeDtypeStruct((n, d), F32), jax.ShapeDtypeStruct((n, d), BF16)],
        compiler_params=_params(1, nbytes),
        name="mix_out_proj",
    )(a, w, x2, gate, norm_g.reshape(1, d), scale, shift)


def _mlp_kernel(*refs, n_norm, rc, fc):
    h_ref, win_ref, wout_ref, x_ref, g2_ref = refs[:5]
    norm_refs = refs[5:5 + 3 * n_norm]
    xo_ref = refs[5 + 3 * n_norm]
    ho_refs = refs[6 + 3 * n_norm:6 + 4 * n_norm]
    acc_s = refs[6 + 4 * n_norm]
    k = pl.program_id(1)

    h = h_ref[...]
    tf = win_ref.shape[1]
    pieces = []
    for c in range(tf // fc):
        a = jnp.dot(h, win_ref[:, c * fc:(c + 1) * fc], preferred_element_type=F32)
        pieces.append(jnp.square(jnp.maximum(a, 0.0)).astype(BF16))
    a = pieces[0] if len(pieces) == 1 else jnp.concatenate(pieces, axis=1)

    prev = jnp.where(k > 0, acc_s[...], 0.0)
    acc_s[...] = prev + jnp.dot(a, wout_ref[...], preferred_element_type=F32)

    @pl.when(k == pl.num_programs(1) - 1)
    def _():
        tm = x_ref.shape[0]

        def rows_body(r, carry):
            rows = pl.ds(pl.multiple_of(r * rc, rc), rc)
            xn = x_ref[rows, :] + g2_ref[0] * acc_s[rows, :]
            xo_ref[rows, :] = xn
            for j in range(n_norm):
                ng, sc, sh = norm_refs[3 * j:3 * j + 3]
                ho_refs[j][rows, :] = _rms_mod(xn, ng[...], sc[0], sh[0]).astype(BF16)
            return carry

        lax.fori_loop(0, tm // rc, rows_body, 0)


def _mlp(h2, x2, w_in, w_out, layer, gate, norms, seq):
    n, d = x2.shape
    f = w_in.shape[2]
    tm = _tile(seq, 512)
    tf = _tile(f, 1024)
    fc = _tile(tf, 512)
    rc = _tile(tm, 64)
    tpb = seq // tm
    n_norm = len(norms)
    row = lambda i, k: (i, 0)
    per_batch = lambda i, k: (i // tpb, 0, 0)
    in_specs = [
        pl.BlockSpec((tm, d), row),
        pl.BlockSpec((pl.Squeezed(), d, tf), lambda i, k: (layer, 0, k)),
        pl.BlockSpec((pl.Squeezed(), tf, d), lambda i, k: (layer, k, 0)),
        pl.BlockSpec((tm, d), row),
        pl.BlockSpec((1, 1, d), per_batch),
    ]
    args = [h2, w_in, w_out, x2, gate]
    for ng, sc, sh in norms:
        in_specs += [pl.BlockSpec((1, d), lambda i, k: (0, 0)),
                     pl.BlockSpec((1, 1, d), per_batch), pl.BlockSpec((1, 1, d), per_batch)]
        args += [ng.reshape(1, d), sc, sh]
    out_specs = [pl.BlockSpec((tm, d), row)] * (1 + n_norm)
    out_shape = [jax.ShapeDtypeStruct((n, d), F32)] + [jax.ShapeDtypeStruct((n, d), BF16)] * n_norm
    nbytes = (2 * tm * d * 2 + 4 * d * tf * 2 + 4 * tm * d * 4 + tm * d * 4
              + 2 * n_norm * tm * d * 2 + 2 * tm * tf * 4 + tm * d * 4)
    outs = pl.pallas_call(
        functools.partial(_mlp_kernel, n_norm=n_norm, rc=rc, fc=fc),
        grid=(n // tm, f // tf),
        in_specs=in_specs,
        out_specs=out_specs,
        out_shape=out_shape,
        scratch_shapes=[pltpu.VMEM((tm, d), F32)],
        compiler_params=_params(2, nbytes),
        name="relu2_mlp",
    )(*args)
    return outs[0], list(outs[1:])


def _head_norm(a, g):
    ms = jnp.mean(a * a, axis=-1, keepdims=True)
    return a * lax.rsqrt(ms + NORM_EPS) * g


def _q_kernel(h_ref, w_ref, g_ref, o_ref, *, hc, scale):
    tm = h_ref.shape[0]
    n_heads, dh = o_ref.shape[1], g_ref.shape[1]
    h = h_ref[...]
    row = lax.broadcasted_iota(jnp.int32, (dh, tm), 0)
    ones_rows = jnp.where(row < N_F_PARTS, 1.0, 0.0).astype(o_ref.dtype)
    gs = g_ref[...] * scale
    for c in range(n_heads // hc):
        acc = jnp.dot(h, w_ref[:, c * hc * dh:(c + 1) * hc * dh], preferred_element_type=F32)
        for hh in range(hc):
            head = c * hc + hh
            qn = _head_norm(acc[:, hh * dh:(hh + 1) * dh], gs)
            o_ref[0, head, 0:dh, :] = qn.T.astype(o_ref.dtype)
            o_ref[0, head, dh:2 * dh, :] = ones_rows


def _q_proj(h, w, layer, g, n_batch, seq, n_heads):
    n, d = h.shape
    dh = d // n_heads
    tm = _tile(seq, 512)
    tpb = seq // tm
    hc = _tile(n_heads, 4)
    nbytes = 2 * tm * d * 2 + d * d * 2 + 2 * n_heads * tm * 2 * dh * 2 + 2 * tm * hc * dh * 4
    return pl.pallas_call(
        functools.partial(_q_kernel, hc=hc, scale=dh ** -0.5 * LOG2_E),
        grid=(n // tm,),
        in_specs=[pl.BlockSpec((tm, d), lambda i: (i, 0)), _resident((d, d), layer),
                  pl.BlockSpec((1, dh), lambda i: (0, 0))],
        out_specs=pl.BlockSpec((1, n_heads, 2 * dh, tm), lambda i: (i // tpb, 0, 0, i % tpb)),
        out_shape=jax.ShapeDtypeStruct((n_batch, n_heads, 2 * dh, seq), BF16),
        compiler_params=_params(1, nbytes),
        name="q_proj",
    )(h, w, g.reshape(1, dh))


def _kv_kernel(h_ref, w_ref, wf_ref, bf_ref, g_ref, ko_ref, vo_ref, fe_ref, fq_ref, carry_s, *, hc, tpb):
    tm, d = h_ref.shape
    n_heads, dh = ko_ref.shape[1], g_ref.shape[1]

    @pl.when(pl.program_id(0) % tpb == 0)
    def _():
        carry_s[...] = jnp.zeros(carry_s.shape, F32)

    h = h_ref[...]

    z = jnp.dot(h, wf_ref[...], preferred_element_type=F32) + bf_ref[...]
    log_f = _log_sigmoid(z)
    ri = lax.broadcasted_iota(jnp.int32, (tm, tm), 0)
    ci = lax.broadcasted_iota(jnp.int32, (tm, tm), 1)
    tri = jnp.where(ri >= ci, 1.0, 0.0).astype(F32)
    cum = jnp.dot(tri, log_f, preferred_element_type=F32, precision=lax.Precision.HIGHEST)
    cum = cum + carry_s[0:1, :]
    carry_s[...] = jnp.broadcast_to(cum[tm - 1:tm, :], carry_s.shape)
    fe_ref[0] = cum[tm - 1:tm, :]

    parts = []
    rest = -LOG2_E * cum
    fq_ref[0] = rest.T
    for _ in range(N_F_PARTS):
        piece = rest.astype(BF16).astype(F32)
        parts.append(piece)
        rest = rest - piece

    lane = lax.broadcasted_iota(jnp.int32, (tm, dh), 1)
    g = g_ref[...]
    for c in range(n_heads // hc):
        acc = jnp.dot(h, w_ref[:, c * hc * dh:(c + 1) * hc * dh], preferred_element_type=F32)
        for hh in range(hc):
            head = c * hc + hh
            kn = _head_norm(acc[:, hh * dh:(hh + 1) * dh], g)
            ko_ref[0, head, :, 0:dh] = kn.astype(ko_ref.dtype)
            aug = jnp.zeros((tm, dh), F32)
            for p in range(N_F_PARTS):
                col = jnp.broadcast_to(parts[p][:, head:head + 1], (tm, dh))
                aug = jnp.where(lane == p, col, aug)
            ko_ref[0, head, :, dh:2 * dh] = aug.astype(ko_ref.dtype)
    for c in range(n_heads // hc):
        acc = jnp.dot(h, w_ref[:, d + c * hc * dh:d + (c + 1) * hc * dh], preferred_element_type=F32)
        for hh in range(hc):
            head = c * hc + hh
            vo_ref[0, head, 0] = acc[:, hh * dh:(hh + 1) * dh].T.astype(vo_ref.dtype)


def _kv_proj(h, w, w_forget, b_forget, g, n_batch, seq, n_heads, tk):
    n, d = h.shape
    dh = d // n_heads
    tm = tk
    tpb = seq // tm
    hc = _tile(n_heads, 4)
    assert n_heads <= V7X_LANES
    wf = jnp.zeros((d, V7X_LANES), BF16).at[:, :n_heads].set(w_forget.astype(BF16))
    bf = jnp.zeros((1, V7X_LANES), F32).at[0, :n_heads].set(b_forget)
    nbytes = (2 * tm * d * 2 + 2 * d * d * 2 + 2 * n_heads * tm * 3 * dh * 2
              + 2 * tm * hc * dh * 4 + 3 * tm * tm * 4)
    return pl.pallas_call(
        functools.partial(_kv_kernel, hc=hc, tpb=tpb),
        grid=(n // tm,),
        in_specs=[pl.BlockSpec((tm, d), lambda i: (i, 0)), _resident((d, 2 * d)),
                  _resident((d, V7X_LANES)), pl.BlockSpec((1, V7X_LANES), lambda i: (0, 0)),
                  pl.BlockSpec((1, dh), lambda i: (0, 0))],
        out_specs=[
            pl.BlockSpec((1, n_heads, tm, 2 * dh), lambda i: (i // tpb, 0, i % tpb, 0)),
            pl.BlockSpec((1, n_heads, 1, dh, tm), lambda i: (i // tpb, 0, i % tpb, 0, 0)),
            pl.BlockSpec((1, 1, V7X_LANES), lambda i: (i, 0, 0)),
            pl.BlockSpec((1, V7X_LANES, tm), lambda i: (i // tpb, 0, i % tpb)),
        ],
        out_shape=[jax.ShapeDtypeStruct((n_batch, n_heads, seq, 2 * dh), BF16),
                   jax.ShapeDtypeStruct((n_batch, n_heads, seq // tk, dh, tk), BF16),
                   jax.ShapeDtypeStruct((n // tm, 1, V7X_LANES), F32),
                   jax.ShapeDtypeStruct((n_batch, V7X_LANES, seq), F32)],
        scratch_shapes=[pltpu.VMEM((V7X_SUBLANES, V7X_LANES), F32)],
        compiler_params=_params(1, nbytes),
        name="kv_proj",
    )(h, w, wf, bf, g.reshape(1, dh))


def _first_key_blocks(f_end, q_gain, k_gain, dh):
    n_batch, n_kv, n_heads = f_end.shape
    nq = n_kv // 2
    qk_max = (QK_BOUND_MARGIN * LOG2_E * math.sqrt(dh)
              * jnp.max(jnp.abs(q_gain)) * jnp.max(jnp.abs(k_gain)))
    first_own = 2 * jnp.arange(nq) - 1
    ref = f_end[:, jnp.maximum(first_own, 0), :]
    gap = LOG2_E * (ref[:, :, None, :] - f_end[:, None, :, :]) + 2.0 * qk_max + 1.0
    below = jnp.arange(n_kv)[None, :] < first_own[:, None]
    needed = jnp.logical_not((gap < -F32_UNDERFLOW_LOG2) & below[None, :, :, None])
    first_block = jnp.argmax(needed, axis=2)
    first_block = jnp.transpose(first_block, (0, 2, 1)).reshape(-1).astype(jnp.int32)

    k_bound = QK_BOUND_MARGIN * math.sqrt(dh) * jnp.max(jnp.abs(k_gain))
    use_bound = (2.0 * qk_max <= MAX_SHIFT_OVERSHOOT_LOG2).astype(F32)
    block_term = jnp.transpose(-LOG2_E * f_end + F_BOUND_SLACK, (0, 2, 1)).reshape(-1)
    bounds = jnp.concatenate([block_term, jnp.stack([k_bound, use_bound])]).astype(F32)
    return first_block, bounds


def _attn_kernel(first_ref, bnd_ref, q_ref, qn_ref, k_ref, vt_ref, fq_ref, o_ref, s_s, p_s, acc_s, *, qc):
    tq = q_ref.shape[3]
    n_kv, dh, tk = vt_ref.shape[2:]
    assert tq == 2 * tk
    qi = pl.program_id(2)
    n_q = pl.num_programs(2)
    tile = (pl.program_id(0) * pl.num_programs(1) + pl.program_id(1)) * n_q + qi
    j0 = first_ref[tile]
    whole = [slice(0, tq)]
    groups = [slice(g * qc, (g + 1) * qc) for g in range(tq // qc)]
    key_minus_query = (lax.broadcasted_iota(jnp.int32, (tk, qc), 0)
                       - lax.broadcasted_iota(jnp.int32, (tk, qc), 1))

    def scores(j, cols, query_ref=q_ref):
        kblk = k_ref[0, 0, pl.ds(pl.multiple_of(j * tk, tk), tk), :]
        return jnp.dot(kblk, query_ref[0, 0, :, cols], preferred_element_type=F32)

    head = pl.program_id(0) * pl.num_programs(1) + pl.program_id(1)
    n_bnd = pl.num_programs(0) * pl.num_programs(1) * n_kv
    use_bound = bnd_ref[n_bnd + 1] > 0.5
    q_sq = jnp.square(q_ref[0, 0, 0:dh, :].astype(F32))
    q_reach = jnp.sqrt(jnp.sum(q_sq, axis=0, keepdims=True)) * bnd_ref[n_bnd]
    at_query = fq_ref[0, pl.ds(pl.program_id(1) % V7X_SUBLANES, 1), :] + F_BOUND_SLACK

    def softmax(s, state, j, cols, masked, bounded=False):
        m_old, _ = state
        if masked:
            s = jnp.where(key_minus_query <= qi * tq + cols.start - j * tk, s, NEG_INF)
        if bounded:
            m_blk = q_reach[:, cols] + jnp.minimum(at_query[:, cols], bnd_ref[head * n_kv + j])
        else:
            m_blk = jnp.max(s, axis=0, keepdims=True)
        m_new = jnp.maximum(m_old, m_blk)
        p = jnp.exp2(s - m_new)
        alpha = jnp.exp2(m_old - m_new)
        return p.astype(p_s.dtype), (m_new, alpha)

    ones_rows = jnp.ones((V7X_BF16_ROWS, tk), p_s.dtype)

    def add_values(p, alpha, j, cols):
        vt1 = jnp.concatenate([vt_ref[0, 0, j], ones_rows], axis=0)
        acc_s[:, cols] = alpha * acc_s[:, cols] + jnp.dot(vt1, p, preferred_element_type=F32)

    def pair(a, carry, last, bounded=False):
        new_carry = []
        for cols, state in zip(groups if last else whole, carry):
            mask_a = last and cols.start < tk
            skip_b = last and cols.stop <= tk
            alpha_prev = state[1]
            s_a = s_s[:, cols]
            if not skip_b:
                s_b = scores(a + 1, cols)
            p_a, state = softmax(s_a, state, a, cols, mask_a, bounded)
            add_values(p_s[:, cols], alpha_prev, jnp.maximum(a - 1, 0), cols)
            if not last:
                s_s[:, cols] = scores(a + 2, cols)
            alpha_a = state[1]
            if not skip_b:
                p_b, state = softmax(s_b, state, a + 1, cols, last, bounded)
            add_values(p_a, alpha_a, a, cols)
            if not last:
                p_s[:, cols] = p_b
            elif not skip_b:
                add_values(p_b, state[1], a + 1, cols)
            new_carry.append(state)
        return tuple(new_carry)

    odd = j0 % 2
    a0 = j0 + odd
    n_unmasked = qi - a0 // 2

    def unmasked_blocks(carry, bounded):
        def single(_, cr):
            p, state = softmax(s_s[...], cr[0], j0, whole[0], False, bounded)
            s_s[...] = scores(j0 + 1, whole[0])
            p_s[...] = p
            return (state,)

        def two_pairs(v, cr):
            return pair(a0 + 4 * v + 2, pair(a0 + 4 * v, cr, False, bounded), False, bounded)

        carry = lax.fori_loop(0, odd, single, carry)
        carry = lax.fori_loop(0, n_unmasked // 2, two_pairs, carry)
        return lax.fori_loop(0, n_unmasked % 2,
                             lambda _, cr: pair(2 * qi - 2, cr, False, bounded), carry)

    @pl.when(qi == 0)
    def _():
        s_s[...] = scores(j0, whole[0])

    p_s[...] = jnp.zeros(p_s.shape, p_s.dtype)
    acc_s[...] = jnp.zeros(acc_s.shape, F32)

    def whole_tile(bounded):
        carry = ((jnp.full((1, tq), NEG_INF, F32), jnp.ones((1, tq), F32)),)
        carry = unmasked_blocks(carry, bounded)
        carry = tuple(tuple(v[:, cols] for v in carry[0]) for cols in groups)
        pair(2 * qi, carry, True, bounded)

        next_tile = jnp.minimum(tile + 1, pl.num_programs(0) * pl.num_programs(1) * n_q - 1)
        s_s[...] = scores(first_ref[next_tile], whole[0], qn_ref)

        for cols in groups:
            out = acc_s[0:dh, cols] * (1.0 / acc_s[dh:dh + 1, cols])
            o_ref[0, cols, :] = out.T.astype(o_ref.dtype)

    pl.when(use_bound)(lambda: whole_tile(True))
    pl.when(jnp.logical_not(use_bound))(lambda: whole_tile(False))


def _attention(first_blocks, bounds, qp, kp, vt, fq):
    n_batch, n_heads, dh2, seq = qp.shape
    n_kv, dh, tk = vt.shape[2:]
    tq = 2 * tk
    qc = _tile(tq, 512)
    nbytes = (2 * tq * dh2 * 2 + 2 * seq * dh2 * 2 + 2 * seq * dh * 2 + 2 * tq * dh * 2
              + dh * tq * 4 + tk * tq * 6 + 4 * tk * tq * 4)
    return pl.pallas_call(
        functools.partial(_attn_kernel, qc=qc),
        grid_spec=pltpu.PrefetchScalarGridSpec(
            num_scalar_prefetch=1,
            grid=(n_batch, n_heads, seq // tq),
            in_specs=[
                pl.BlockSpec(memory_space=pltpu.SMEM),
                pl.BlockSpec((1, 1, dh2, tq), lambda b, h, i, first: (b, h, 0, i)),
                pl.BlockSpec((1, 1, dh2, tq),
                             lambda b, h, i, first: (b, h, 0, jnp.minimum(i + 1, seq // tq - 1))),
                pl.BlockSpec((1, 1, seq, dh2), lambda b, h, i, first: (b, h, 0, 0)),
                pl.BlockSpec((1, 1, n_kv, dh, tk), lambda b, h, i, first: (b, h, 0, 0, 0)),
                pl.BlockSpec((1, V7X_SUBLANES, tq), lambda b, h, i, first: (b, h // V7X_SUBLANES, i)),
            ],
            out_specs=pl.BlockSpec((1, tq, dh), lambda b, h, i, first: (b, i, h)),
            scratch_shapes=[pltpu.VMEM((tk, tq), F32), pltpu.VMEM((tk, tq), BF16),
                            pltpu.VMEM((dh + V7X_BF16_ROWS, tq), F32)],
        ),
        out_shape=jax.ShapeDtypeStruct((n_batch, seq, n_heads * dh), BF16),
        compiler_params=_params(3, nbytes),
        name="forgetting_attention",
    )(first_blocks, bounds, qp, qp, kp, vt, fq)


def kernel(x, c, mix_norm_g, mlp_norm_g, w_mod, b_mod, w_mlp_in, w_mlp_out, lru_w_in, lru_conv_w, lru_conv_b, lru_w_a, lru_b_a, lru_w_i, lru_b_i, lru_lambda, lru_w_out, kv_norm_g, kv_w_mod, kv_b_mod, w_kv, k_norm_g, w_forget, b_forget, attn_w_q, q_norm_g, attn_w_o):
    n_batch, seq, d = x.shape
    depth = w_mod.shape[0]
    n_a = lru_w_in.shape[0]
    n_heads = w_forget.shape[1]
    assert 1 <= n_a < depth and w_mod.shape[2] == N_MOD * d
    n = n_batch * seq
    tk = _tile(seq, 512)

    mod = _modulation(c, w_mod, b_mod)
    kv_mod = _modulation(c, kv_w_mod[None], kv_b_mod[None])[0]
    per_batch = lambda a: a.reshape(n_batch, 1, d)
    sh1, sc1, g1, sh2, sc2, g2 = [[per_batch(mod[l, :, j * d:(j + 1) * d]) for l in range(depth)]
                                  for j in range(N_MOD)]
    kv_shift, kv_scale = per_batch(kv_mod[:, :d]), per_batch(kv_mod[:, d:])

    bf = lambda w: w.astype(BF16)
    w_mlp_in, w_mlp_out, lru_w_in, lru_w_out = bf(w_mlp_in), bf(w_mlp_out), bf(lru_w_in), bf(lru_w_out)
    attn_w_q, attn_w_o = bf(attn_w_q), bf(attn_w_o)
    x2 = x.reshape(n, d)
    h = h_kv = kp = vt = f_end = fq = None
    for layer in range(depth):
        if layer < n_a:
            a = layer
            if layer == 0:
                xg = _lru_in(x2, lru_w_in, a, seq, norm=(mix_norm_g[0], sc1[0], sh1[0]))
            else:
                xg = _lru_in(h, lru_w_in, a, seq)
            mix_in = _lru_scan(xg, lru_conv_w[a], lru_conv_b[a], bf(lru_w_a[a]), lru_b_a[a],
                               bf(lru_w_i[a]), lru_b_i[a], lru_lambda[a], n_batch, seq)
            w_o, w_o_layer = lru_w_out, a
        else:
            bl = layer - n_a
            if layer == n_a:
                kp, vt, f_end, fq = _kv_proj(h_kv, bf(w_kv), w_forget, b_forget, k_norm_g, n_batch, seq,
                                             n_heads, tk)
                f_end = f_end.reshape(n_batch, seq // tk, V7X_LANES)[:, :, :n_heads]
            qp = _q_proj(h, attn_w_q, bl, q_norm_g[bl], n_batch, seq, n_heads)
            first_blocks, bounds = _first_key_blocks(f_end, q_norm_g[bl], k_norm_g, d // n_heads)
            mix_in = _attention(first_blocks, bounds, qp, kp, vt, fq).reshape(n, d)
            w_o, w_o_layer = attn_w_o, bl
        x2, h2 = _proj_res(mix_in, w_o, w_o_layer, x2, g1[layer], mlp_norm_g[layer], sc2[layer],
                           sh2[layer], seq)
        norms = []
        if layer + 1 < depth:
            norms.append((mix_norm_g[layer + 1], sc1[layer + 1], sh1[layer + 1]))
        if layer + 1 == n_a:
            norms.append((kv_norm_g, kv_scale, kv_shift))
        x2, hs = _mlp(h2, x2, w_mlp_in, w_mlp_out, layer, g2[layer], norms, seq)
        if hs:
            h = hs[0]
        if layer + 1 == n_a:
            h_kv = hs[1]
    return x2.reshape(n_batch, seq, d)
```
